```python
import jax, jax.numpy as jnp
from jax import lax
import numpy as np

D_MODEL = 2048
BATCH = 32
SEQ = 256
DEPTH = 4
DEC_BATCH = 4
DEC_SEQ = 2048
PAST_LEN = 512

GRID_W = 64
N_MIXERS = 3
N_CONV_LAYERS = (DEPTH + 2) // 3
N_GMLP_LAYERS = (DEPTH + 1) // 3
N_MLA_LAYERS = DEPTH // 3
CONV_WIDTH = 3
CHUNK = 128
GMLP_WIDTH = D_MODEL
GMLP_GROUPS = 16
GMLP_GROUP_DIM = GMLP_WIDTH // GMLP_GROUPS
N_HEADS = 16
QK_NOPE = 128
QK_ROPE = 64
V_DIM = 128
Q_RANK = 512
KV_RANK = 512
ROPE_THETA = 10000.0
D_FF = 4 * D_MODEL
N_MOD = 6
EPS = 1e-6
Q_BLOCK = 128

kernel_name = 'hybrid_dit_conv_gmlp_mla_step'


def rms_norm(x, g):
    x32 = x.astype(jnp.float32)
    y = x32 * lax.rsqrt(jnp.mean(x32 * x32, axis=-1, keepdims=True) + EPS)
    return (y * g.astype(jnp.float32)).astype(x.dtype)


def modulate(h, shift, scale):
    return h * (1 + scale) + shift


def short_conv_mixer(h, w_in, w_conv, w_out):
    L = h.shape[1]
    bg, cg, hv = jnp.split(h @ w_in, 3, axis=-1)
    pad = CONV_WIDTH // 2
    zp = jnp.pad(cg * hv, ((0, 0), (pad, CONV_WIDTH - 1 - pad), (0, 0)))
    conv = sum(zp[:, k:k + L] * w_conv[k] for k in range(CONV_WIDTH))
    return (bg * conv) @ w_out


def chunk_gmlp_mixer(h, w_in, g_v, w_s, b_s, w_out):
    Bn, L, _ = h.shape
    u, v = jnp.split(h @ w_in, 2, axis=-1)
    v = rms_norm(v, g_v).reshape(Bn, L // CHUNK, CHUNK, GMLP_GROUPS, GMLP_GROUP_DIM)
    v = jnp.einsum('gpq,bnqgd->bnpgd', w_s, v) + b_s.T[:, :, None]
    return (u * v.reshape(Bn, L, GMLP_WIDTH)) @ w_out


def rope_tables(L):
    n_rows = L // GRID_W
    rows = jnp.broadcast_to(jnp.arange(n_rows)[:, None], (n_rows, GRID_W)).reshape(-1)
    cols = jnp.broadcast_to(jnp.arange(GRID_W)[None, :], (n_rows, GRID_W)).reshape(-1)
    nf = QK_ROPE // 4
    inv = 1.0 / (ROPE_THETA ** (jnp.arange(nf, dtype=jnp.float32) / nf))
    ang_r = rows.astype(jnp.float32)[:, None] * inv
    ang_c = cols.astype(jnp.float32)[:, None] * inv
    return (jnp.cos(ang_r), jnp.sin(ang_r), jnp.cos(ang_c), jnp.sin(ang_c))


def rotate(x, cos, sin):
    x1, x2 = jnp.split(x, 2, axis=-1)
    return jnp.concatenate([x1 * cos - x2 * sin, x1 * sin + x2 * cos], axis=-1)


def apply_rope_2d(x, tables):
    cr, sr, cc, sc = [t.reshape((t.shape[0],) + (1,) * (x.ndim - 3) + (t.shape[1],)).astype(x.dtype) for t in tables]
    xr, xc = jnp.split(x, 2, axis=-1)
    return jnp.concatenate([rotate(xr, cr, sr), rotate(xc, cc, sc)], axis=-1)


def mla_project(h, w_q_a, g_q, w_q_b, w_kv_a, g_kv):
    Bn, L, _ = h.shape
    q = (rms_norm(h @ w_q_a, g_q) @ w_q_b).reshape(Bn, L, N_HEADS, QK_NOPE + QK_ROPE)
    kv = h @ w_kv_a
    ckv = rms_norm(kv[..., :KV_RANK], g_kv)
    kpe = kv[..., KV_RANK:]
    return q[..., :QK_NOPE], q[..., QK_NOPE:], ckv, kpe


def mla_attend(q_nope, q_pe, ckv, kpe, w_kv_b, w_o):
    Bn, Lq = q_nope.shape[:2]
    Lk = ckv.shape[1]
    kv = (ckv @ w_kv_b).reshape(Bn, Lk, N_HEADS, QK_NOPE + V_DIM)
    k_nope, v = kv[..., :QK_NOPE], kv[..., QK_NOPE:]
    nb = Lq // Q_BLOCK
    scale = (QK_NOPE + QK_ROPE) ** -0.5

    def to_blocks(t):
        return jnp.moveaxis(t.reshape((Bn, nb, Q_BLOCK) + t.shape[2:]), 1, 0)

    def block(args):
        qn, qp = args
        s = jnp.einsum('bqhd,bkhd->bhqk', qn, k_nope) + jnp.einsum('bqhr,bkr->bhqk', qp, kpe)
        p = jax.nn.softmax(s.astype(jnp.float32) * scale, axis=-1).astype(v.dtype)
        return jnp.einsum('bhqk,bkhd->bqhd', p, v)

    o = lax.map(block, (to_blocks(q_nope), to_blocks(q_pe)))
    o = jnp.moveaxis(o, 0, 1).reshape(Bn, Lq, N_HEADS * V_DIM)
    return o @ w_o


def sq_relu_mlp(h, w1, w2):
    return jnp.square(jax.nn.relu(h @ w1)) @ w2


def setup_inputs(seed: int = 0) -> dict:
    key = jax.random.key(seed)
    ks = iter(jax.random.split(key, 32))

    def nrm(shape, scale=1.0):
        return jax.random.normal(next(ks), shape, jnp.float32) * scale

    D = D_MODEL
    return {
        'x_prompt': nrm((BATCH, SEQ, D)),
        'x_sample': nrm((DEC_BATCH, DEC_SEQ, D)),
        'cache_ckv': nrm((DEC_BATCH, N_MLA_LAYERS, PAST_LEN, KV_RANK)),
        'cache_kpe': nrm((DEC_BATCH, N_MLA_LAYERS, PAST_LEN, QK_ROPE)),
        'c': nrm((DEC_BATCH, D)),
        'c_ctx': nrm((D,)),
        'ada_w': nrm((DEPTH, D, N_MOD * D), 0.5 * D ** -0.5),
        'ada_b': nrm((DEPTH, N_MOD * D), 0.02),
        'norm1': 1.0 + nrm((DEPTH, D), 0.02),
        'norm2': 1.0 + nrm((DEPTH, D), 0.02),
        'conv_w_in': nrm((N_CONV_LAYERS, D, 3 * D), D ** -0.5),
        'conv_w': nrm((N_CONV_LAYERS, CONV_WIDTH, D), 0.5),
        'conv_w_out': nrm((N_CONV_LAYERS, D, D), D ** -0.5),
        'gmlp_w_in': nrm((N_GMLP_LAYERS, D, 2 * GMLP_WIDTH), D ** -0.5),
        'gmlp_g_v': 1.0 + nrm((N_GMLP_LAYERS, GMLP_WIDTH), 0.02),
        'gmlp_w_s': nrm((N_GMLP_LAYERS, GMLP_GROUPS, CHUNK, CHUNK), CHUNK ** -0.5),
        'gmlp_b_s': 1.0 + nrm((N_GMLP_LAYERS, GMLP_GROUPS, CHUNK), 0.1),
        'gmlp_w_out': nrm((N_GMLP_LAYERS, GMLP_WIDTH, D), GMLP_WIDTH ** -0.5),
        'mla_w_q_a': nrm((N_MLA_LAYERS, D, Q_RANK), D ** -0.5),
        'mla_g_q': 1.0 + nrm((N_MLA_LAYERS, Q_RANK), 0.02),
        'mla_w_q_b': nrm((N_MLA_LAYERS, Q_RANK, N_HEADS * (QK_NOPE + QK_ROPE)), Q_RANK ** -0.5),
        'mla_w_kv_a': nrm((N_MLA_LAYERS, D, KV_RANK + QK_ROPE), D ** -0.5),
        'mla_g_kv': 1.0 + nrm((N_MLA_LAYERS, KV_RANK), 0.02),
        'mla_w_kv_b': nrm((N_MLA_LAYERS, KV_RANK, N_HEADS * (QK_NOPE + V_DIM)), KV_RANK ** -0.5),
        'mla_w_o': nrm((N_MLA_LAYERS, N_HEADS * V_DIM, D), (N_HEADS * V_DIM) ** -0.5),
        'mlp_w1': nrm((DEPTH, D, D_FF), D ** -0.5),
        'mlp_w2': nrm((DEPTH, D_FF, D), 0.7 * D_FF ** -0.5),
        'final_norm': 1.0 + nrm((D,), 0.02),
    }


def reference(x_prompt, x_sample, cache_ckv, cache_kpe, c, c_ctx, ada_w, ada_b, norm1, norm2,
              conv_w_in, conv_w, conv_w_out, gmlp_w_in, gmlp_g_v, gmlp_w_s, gmlp_b_s, gmlp_w_out,
              mla_w_q_a, mla_g_q, mla_w_q_b, mla_w_kv_a, mla_g_kv, mla_w_kv_b, mla_w_o,
              mlp_w1, mlp_w2, final_norm):
    rope = rope_tables(x_sample.shape[1])
    xp, xs = x_prompt, x_sample
    new_ckv, new_kpe = [], []
    for i in range(DEPTH):
        mod_p = (jax.nn.silu(c_ctx) @ ada_w[i] + ada_b[i])[None, None, :]
        mod_s = (jax.nn.silu(c) @ ada_w[i] + ada_b[i])[:, None, :]
        sh1p, sc1p, g1p, sh2p, sc2p, g2p = jnp.split(mod_p, N_MOD, axis=-1)
        sh1s, sc1s, g1s, sh2s, sc2s, g2s = jnp.split(mod_s, N_MOD, axis=-1)
        hp = modulate(rms_norm(xp, norm1[i]), sh1p, sc1p)
        hs = modulate(rms_norm(xs, norm1[i]), sh1s, sc1s)
        kind, j = i % N_MIXERS, i // N_MIXERS
        if kind == 0:
            yp = short_conv_mixer(hp, conv_w_in[j], conv_w[j], conv_w_out[j])
            ys = short_conv_mixer(hs, conv_w_in[j], conv_w[j], conv_w_out[j])
        elif kind == 1:
            yp = chunk_gmlp_mixer(hp, gmlp_w_in[j], gmlp_g_v[j], gmlp_w_s[j], gmlp_b_s[j], gmlp_w_out[j])
            ys = chunk_gmlp_mixer(hs, gmlp_w_in[j], gmlp_g_v[j], gmlp_w_s[j], gmlp_b_s[j], gmlp_w_out[j])
        else:
            qn_p, qp_p, ckv_p, kpe_p = mla_project(hp, mla_w_q_a[j], mla_g_q[j], mla_w_q_b[j], mla_w_kv_a[j], mla_g_kv[j])
            yp = mla_attend(qn_p, qp_p, ckv_p, kpe_p, mla_w_kv_b[j], mla_w_o[j])
            new_ckv.append(ckv_p)
            new_kpe.append(kpe_p)
            qn_s, qp_s, ckv_s, kpe_s = mla_project(hs, mla_w_q_a[j], mla_g_q[j], mla_w_q_b[j], mla_w_kv_a[j], mla_g_kv[j])
            qp_s = apply_rope_2d(qp_s, rope)
            kpe_s = apply_rope_2d(kpe_s, rope)
            ckv_all = jnp.concatenate([cache_ckv[:, j], ckv_s], axis=1)
            kpe_all = jnp.concatenate([cache_kpe[:, j], kpe_s], axis=1)
            ys = mla_attend(qn_s, qp_s, ckv_all, kpe_all, mla_w_kv_b[j], mla_w_o[j])
        xp = xp + g1p * yp
        xs = xs + g1s * ys
        hp = modulate(rms_norm(xp, norm2[i]), sh2p, sc2p)
        hs = modulate(rms_norm(xs, norm2[i]), sh2s, sc2s)
        xp = xp + g2p * sq_relu_mlp(hp, mlp_w1[i], mlp_w2[i])
        xs = xs + g2s * sq_relu_mlp(hs, mlp_w1[i], mlp_w2[i])
    y_prompt = rms_norm(xp, final_norm)
    y_sample = rms_norm(xs, final_norm)
    return (y_prompt, y_sample, jnp.stack(new_ckv, axis=1), jnp.stack(new_kpe, axis=1))
```

```python
import functools

import jax
import jax.numpy as jnp
from jax import lax
from jax.experimental import pallas as pl
from jax.experimental.pallas import tpu as pltpu

D_MODEL = 2048
BATCH = 32
SEQ = 256
DEPTH = 4
DEC_BATCH = 4
DEC_SEQ = 2048
PAST_LEN = 512
GRID_W = 64
N_MIXERS = 3
CONV_WIDTH = 3
CHUNK = 128
GMLP_GROUPS = 16
N_HEADS = 16
QK_NOPE = 128
QK_ROPE = 64
V_DIM = 128
Q_RANK = 512
KV_RANK = 512
ROPE_THETA = 10000.0
D_FF = 4 * D_MODEL
N_MOD = 6
EPS = 1e-6

T_PROMPT = BATCH * SEQ
T_SAMPLE = DEC_BATCH * DEC_SEQ
T_ALL = T_PROMPT + T_SAMPLE
N_GROUPS = 8
HEAD_PAD = 256
LANE = 128
VMEM_LIMIT = 56 * 1024 * 1024

BF16 = jnp.bfloat16
F32 = jnp.float32


def _params(*sem):
    return pltpu.CompilerParams(dimension_semantics=sem, vmem_limit_bytes=VMEM_LIMIT)


def _dot(a, b):
    return jnp.dot(a, b, preferred_element_type=F32)


def _mod_spec(layer, chunk, tm):
    n_prompt = T_PROMPT // tm
    per_seq = DEC_SEQ // tm

    def index(i, *_):
        grp = jnp.where(i < n_prompt, 0, 1 + (i - n_prompt) // per_seq)
        return ((layer * N_MOD + chunk) * N_GROUPS + grp, 0, 0)

    return pl.BlockSpec((1, 1, D_MODEL), index)


def _norm_mod_rows(x_ref, g_ref, sh_ref, sc_ref, h_ref, tm, rows=16):
    g = g_ref[...]
    sc = 1.0 + sc_ref[0]
    sh = sh_ref[0]

    def body(c, carry):
        r = pl.multiple_of(c * rows, rows)
        x = x_ref[pl.ds(r, rows), :]
        ms = jnp.mean(x * x, axis=-1, keepdims=True)
        y = x * lax.rsqrt(ms + EPS) * g
        h_ref[pl.ds(r, rows), :] = (y * sc + sh).astype(BF16)
        return carry

    lax.fori_loop(0, tm // rows, body, 0)


def _modvec_kernel(c_ref, w_ref, b_ref, o_ref):
    s = jax.nn.silu(c_ref[...]).astype(BF16)
    o_ref[0, 0] = _dot(s, w_ref[0].astype(BF16)) + b_ref[0]


def _modvec(cond, ada_w, ada_b, tn=1024):
    per_chunk = D_MODEL // tn
    out = pl.pallas_call(
        _modvec_kernel,
        grid=(DEPTH, N_MOD * per_chunk),
        in_specs=[
            pl.BlockSpec((N_GROUPS, D_MODEL), lambda l, j: (0, 0)),
            pl.BlockSpec((1, D_MODEL, tn), lambda l, j: (l, 0, j)),
            pl.BlockSpec((1, 1, tn), lambda l, j: (l, 0, j)),
        ],
        out_specs=pl.BlockSpec((1, 1, N_GROUPS, tn), lambda l, j: (l, j // per_chunk, 0, j % per_chunk)),
        out_shape=jax.ShapeDtypeStruct((DEPTH, N_MOD, N_GROUPS, D_MODEL), F32),
        compiler_params=_params("arbitrary", "arbitrary"),
        name="modvec",
    )(cond, ada_w, ada_b.reshape(DEPTH, 1, N_MOD * D_MODEL))
    return out.reshape(DEPTH * N_MOD * N_GROUPS, 1, D_MODEL)


def _conv_in_kernel(x_ref, g_ref, sh_ref, sc_ref, wb_ref, wc_ref, wh_ref, bg_ref, p_ref, h_ref, *, tm):
    @pl.when(pl.program_id(1) == 0)
    def _():
        _norm_mod_rows(x_ref, g_ref, sh_ref, sc_ref, h_ref, tm)

    h = h_ref[...]
    bg_ref[...] = _dot(h, wb_ref[...])
    p_ref[...] = _dot(h, wc_ref[...]) * _dot(h, wh_ref[...])


def _conv_in(x, mod, norm_g, w_in, layer, tm=512, tn=512):
    nj = D_MODEL // tn
    row = lambda i, j: (i, 0)
    return pl.pallas_call(
        functools.partial(_conv_in_kernel, tm=tm),
        grid=(T_ALL // tm, nj),
        in_specs=[
            pl.BlockSpec((tm, D_MODEL), row),
            pl.BlockSpec((1, D_MODEL), lambda i, j: (0, 0)),
            _mod_spec(layer, 0, tm),
            _mod_spec(layer, 1, tm),
            pl.BlockSpec((D_MODEL, tn), lambda i, j: (0, j)),
            pl.BlockSpec((D_MODEL, tn), lambda i, j: (0, nj + j)),
            pl.BlockSpec((D_MODEL, tn), lambda i, j: (0, 2 * nj + j)),
        ],
        out_specs=[pl.BlockSpec((tm, tn), lambda i, j: (i, j))] * 2,
        out_shape=[jax.ShapeDtypeStruct((T_ALL, D_MODEL), F32)] * 2,
        scratch_shapes=[pltpu.VMEM((tm, D_MODEL), BF16)],
        compiler_params=_params("parallel", "arbitrary"),
        name="conv_in",
    )(x, norm_g, mod, mod, w_in, w_in, w_in)


def _gmlp_in_kernel(x_ref, g_ref, sh_ref, sc_ref, w_ref, z_ref, h_ref, *, tm):
    @pl.when(pl.program_id(1) == 0)
    def _():
        _norm_mod_rows(x_ref, g_ref, sh_ref, sc_ref, h_ref, tm)

    z_ref[...] = _dot(h_ref[...], w_ref[...])


def _gmlp_in(x, mod, norm_g, w_in, layer, tm=512, tn=1024):
    n_out = w_in.shape[1]
    return pl.pallas_call(
        functools.partial(_gmlp_in_kernel, tm=tm),
        grid=(T_ALL // tm, n_out // tn),
        in_specs=[
            pl.BlockSpec((tm, D_MODEL), lambda i, j: (i, 0)),
            pl.BlockSpec((1, D_MODEL), lambda i, j: (0, 0)),
            _mod_spec(layer, 0, tm),
            _mod_spec(layer, 1, tm),
            pl.BlockSpec((D_MODEL, tn), lambda i, j: (0, j)),
        ],
        out_specs=pl.BlockSpec((tm, tn), lambda i, j: (i, j)),
        out_shape=jax.ShapeDtypeStruct((T_ALL, n_out), F32),
        scratch_shapes=[pltpu.VMEM((tm, D_MODEL), BF16)],
        compiler_params=_params("parallel", "arbitrary"),
        name="gmlp_in",
    )(x, norm_g, mod, mod, w_in)


def _conv_gate_rows(bg_ref, p_ref, lo_ref, hi_ref, cw_ref, pbuf_ref, a_ref, tm, rows=16):
    halo = lo_ref.shape[0]
    pbuf_ref[0:halo, :] = lo_ref[...]
    pbuf_ref[halo + tm:2 * halo + tm, :] = hi_ref[...]

    def copy(c, carry):
        r = pl.multiple_of(c * rows, rows)
        pbuf_ref[pl.ds(halo + r, rows), :] = p_ref[pl.ds(r, rows), :]
        return carry

    lax.fori_loop(0, tm // rows, copy, 0)

    base = pl.program_id(0) * tm
    width = 4 * LANE

    def body(c, carry):
        r = pl.multiple_of(c * rows, rows)
        t = base + r + lax.broadcasted_iota(jnp.int32, (rows, 1), 0)
        pos = jnp.where(t < T_PROMPT, t % SEQ, (t - T_PROMPT) % DEC_SEQ)
        last = jnp.where(t < T_PROMPT, SEQ - 1, DEC_SEQ - 1)
        for lo in range(0, D_MODEL, width):
            cols = slice(lo, lo + width)
            ext = pbuf_ref[pl.ds(r, rows + 2 * halo), cols]
            prev = jnp.where(pos == 0, 0.0, ext[halo - 1:halo - 1 + rows])
            nxt = jnp.where(pos == last, 0.0, ext[halo + 1:halo + 1 + rows])
            conv = prev * cw_ref[0:1, cols] + ext[halo:halo + rows] * cw_ref[1:2, cols] + nxt * cw_ref[2:3, cols]
            a_ref[pl.ds(r, rows), cols] = (bg_ref[pl.ds(r, rows), cols] * conv).astype(BF16)
        return carry

    lax.fori_loop(0, tm // rows, body, 0)


def _conv_out_kernel(bg_ref, p_ref, lo_ref, hi_ref, cw_ref, x_ref, gate_ref, w_ref, o_ref, a_ref, pbuf_ref, *, tm):
    @pl.when(pl.program_id(1) == 0)
    def _():
        _conv_gate_rows(bg_ref, p_ref, lo_ref, hi_ref, cw_ref, pbuf_ref, a_ref, tm)

    o_ref[...] = x_ref[...] + gate_ref[0] * _dot(a_ref[...], w_ref[...])


def _conv_out(bg, p, conv_w, x, mod, w_out, layer, tm=512, tn=512, halo=8):
    nb = tm // halo
    last_halo = T_ALL // halo - 1
    row = lambda i, j: (i, 0)
    return pl.pallas_call(
        functools.partial(_conv_out_kernel, tm=tm),
        grid=(T_ALL // tm, D_MODEL // tn),
        in_specs=[
            pl.BlockSpec((tm, D_MODEL), row),
            pl.BlockSpec((tm, D_MODEL), row),
            pl.BlockSpec((halo, D_MODEL), lambda i, j: (jnp.maximum(i * nb - 1, 0), 0)),
            pl.BlockSpec((halo, D_MODEL), lambda i, j: (jnp.minimum((i + 1) * nb, last_halo), 0)),
            pl.BlockSpec((CONV_WIDTH, D_MODEL), lambda i, j: (0, 0)),
            pl.BlockSpec((tm, tn), lambda i, j: (i, j)),
            pl.BlockSpec((1, 1, tn), _gate_index(layer, 2, tm)),
            pl.BlockSpec((D_MODEL, tn), lambda i, j: (0, j)),
        ],
        out_specs=pl.BlockSpec((tm, tn), lambda i, j: (i, j)),
        out_shape=jax.ShapeDtypeStruct((T_ALL, D_MODEL), F32),
        scratch_shapes=[pltpu.VMEM((tm, D_MODEL), BF16), pltpu.VMEM((tm + 2 * halo, D_MODEL), F32)],
        compiler_params=_params("parallel", "arbitrary"),
        name="conv_out",
    )(bg, p, p, p, conv_w, x, mod, w_out)


def _gate_index(layer, chunk, tm):
    n_prompt = T_PROMPT // tm
    per_seq = DEC_SEQ // tm

    def index(i, j):
        grp = jnp.where(i < n_prompt, 0, 1 + (i - n_prompt) // per_seq)
        return ((layer * N_MOD + chunk) * N_GROUPS + grp, 0, j)

    return index


def _gmlp_gate_rows(z_u_ref, z_v_ref, gv_ref, ws_ref, bs_ref, a_ref, vn_ref, tm, rows=16):
    gv = gv_ref[...]

    def norm(c, carry):
        r = pl.multiple_of(c * rows, rows)
        v = z_v_ref[pl.ds(r, rows), :]
        ms = jnp.mean(v * v, axis=-1, keepdims=True)
        vn_ref[pl.ds(r, rows), :] = (v * lax.rsqrt(ms + EPS) * gv).astype(BF16)
        return carry

    lax.fori_loop(0, tm // rows, norm, 0)

    n_chunks = tm // CHUNK
    for g in range(GMLP_GROUPS):
        cols = slice(g * LANE, (g + 1) * LANE)
        rhs = jnp.concatenate([vn_ref[c * CHUNK:(c + 1) * CHUNK, cols] for c in range(n_chunks)], axis=1)
        mixed = _dot(ws_ref[g], rhs)
        for c in range(n_chunks):
            rs = slice(c * CHUNK, (c + 1) * CHUNK)
            vv = mixed[:, c * CHUNK:(c + 1) * CHUNK] + bs_ref[:, cols]
            a_ref[rs, cols] = (z_u_ref[rs, cols] * vv).astype(BF16)


def _gmlp_out_kernel(zu_ref, zv_ref, gv_ref, ws_ref, bs_ref, x_ref, gate_ref, w_ref, o_ref, a_ref, vn_ref, *, tm):
    @pl.when(pl.program_id(1) == 0)
    def _():
        _gmlp_gate_rows(zu_ref, zv_ref, gv_ref, ws_ref, bs_ref, a_ref, vn_ref, tm)

    o_ref[...] = x_ref[...] + gate_ref[0] * _dot(a_ref[...], w_ref[...])


def _gmlp_out(z, g_v, w_s, b_full, x, mod, w_out, layer, tm=512, tn=512):
    return pl.pallas_call(
        functools.partial(_gmlp_out_kernel, tm=tm),
        grid=(T_ALL // tm, D_MODEL // tn),
        in_specs=[
            pl.BlockSpec((tm, D_MODEL), lambda i, j: (i, 0)),
            pl.BlockSpec((tm, D_MODEL), lambda i, j: (i, 1)),
            pl.BlockSpec((1, D_MODEL), lambda i, j: (0, 0)),
            pl.BlockSpec((GMLP_GROUPS, CHUNK, CHUNK), lambda i, j: (0, 0, 0)),
            pl.BlockSpec((CHUNK, D_MODEL), lambda i, j: (0, 0)),
            pl.BlockSpec((tm, tn), lambda i, j: (i, j)),
            pl.BlockSpec((1, 1, tn), _gate_index(layer, 2, tm)),
            pl.BlockSpec((D_MODEL, tn), lambda i, j: (0, j)),
        ],
        out_specs=pl.BlockSpec((tm, tn), lambda i, j: (i, j)),
        out_shape=jax.ShapeDtypeStruct((T_ALL, D_MODEL), F32),
        scratch_shapes=[pltpu.VMEM((tm, D_MODEL), BF16), pltpu.VMEM((tm, D_MODEL), BF16)],
        compiler_params=_params("parallel", "arbitrary"),
        name="gmlp_out",
    )(z, z, g_v, w_s, b_full, x, mod, w_out)


def _attn_out_kernel(a_ref, x_ref, gate_ref, w_ref, o_ref):
    o_ref[...] = x_ref[...] + gate_ref[0] * _dot(a_ref[...], w_ref[...])


def _attn_out(a, x, mod, w_o, layer, tm=512, tn=512):
    return pl.pallas_call(
        _attn_out_kernel,
        grid=(T_ALL // tm, D_MODEL // tn),
        in_specs=[
            pl.BlockSpec((tm, D_MODEL), lambda i, j: (i, 0)),
            pl.BlockSpec((tm, tn), lambda i, j: (i, j)),
            pl.BlockSpec((1, 1, tn), _gate_index(layer, 2, tm)),
            pl.BlockSpec((D_MODEL, tn), lambda i, j: (0, j)),
        ],
        out_specs=pl.BlockSpec((tm, tn), lambda i, j: (i, j)),
        out_shape=jax.ShapeDtypeStruct((T_ALL, D_MODEL), F32),
        compiler_params=_params("parallel", "arbitrary"),
        name="attn_out",
    )(a, x, mod, w_o)


def _mla_proj_kernel(x_ref, g_ref, sh_ref, sc_ref, wqa_ref, gq_ref, wq_ref, wqs_ref, wkv_ref, gkv_ref,
                     cos_ref, sin_ref, q_ref, ckv_ref, kpe_ref, kper_ref, h_ref, *, tm):
    _norm_mod_rows(x_ref, g_ref, sh_ref, sc_ref, h_ref, tm)
    h = h_ref[...]
    cos = cos_ref[...]
    sin = sin_ref[...]

    qa = _dot(h, wqa_ref[...])
    qa = qa * lax.rsqrt(jnp.mean(qa * qa, axis=-1, keepdims=True) + EPS) * gq_ref[...]
    qa = qa.astype(BF16)
    q = _dot(qa, wq_ref[...])
    qs = _dot(qa, wqs_ref[...])
    for hd in range(N_HEADS):
        lo = hd * HEAD_PAD
        q_ref[:, lo:lo + LANE] = q[:, lo:lo + LANE].astype(BF16)
        rot = q[:, lo + LANE:lo + HEAD_PAD] * cos + qs[:, hd * LANE:(hd + 1) * LANE] * sin
        q_ref[:, lo + LANE:lo + HEAD_PAD] = rot.astype(BF16)

    kv = _dot(h, wkv_ref[...])
    c = kv[:, :KV_RANK]
    ckv_ref[...] = c * lax.rsqrt(jnp.mean(c * c, axis=-1, keepdims=True) + EPS) * gkv_ref[...]
    kpe = kv[:, KV_RANK:KV_RANK + LANE]
    kpe_ref[...] = kpe
    kper_ref[...] = (kpe * cos + kv[:, KV_RANK + LANE:] * sin).astype(BF16)


def _mla_proj(x, mod, norm_g, wqa, gq, wq, wqs, wkv, gkv, cos_t, sin_t, layer, tm=256):
    n_prompt = T_PROMPT // tm
    per_seq = DEC_SEQ // tm
    const = lambda i: (0, 0)
    rope_idx = lambda i: (jnp.where(i < n_prompt, 0, 1 + (i - n_prompt) % per_seq), 0)
    row = lambda i: (i, 0)
    return pl.pallas_call(
        functools.partial(_mla_proj_kernel, tm=tm),
        grid=(T_ALL // tm,),
        in_specs=[
            pl.BlockSpec((tm, D_MODEL), row),
            pl.BlockSpec((1, D_MODEL), const),
            _mod_spec(layer, 0, tm),
            _mod_spec(layer, 1, tm),
            pl.BlockSpec(wqa.shape, const),
            pl.BlockSpec(gq.shape, const),
            pl.BlockSpec(wq.shape, const),
            pl.BlockSpec(wqs.shape, const),
            pl.BlockSpec(wkv.shape, const),
            pl.BlockSpec(gkv.shape, const),
            pl.BlockSpec((tm, LANE), rope_idx),
            pl.BlockSpec((tm, LANE), rope_idx),
        ],
        out_specs=[
            pl.BlockSpec((tm, N_HEADS * HEAD_PAD), row),
            pl.BlockSpec((tm, KV_RANK), row),
            pl.BlockSpec((tm, LANE), row),
            pl.BlockSpec((tm, LANE), row),
        ],
        out_shape=[
            jax.ShapeDtypeStruct((T_ALL, N_HEADS * HEAD_PAD), BF16),
            jax.ShapeDtypeStruct((T_ALL, KV_RANK), F32),
            jax.ShapeDtypeStruct((T_ALL, LANE), F32),
            jax.ShapeDtypeStruct((T_ALL, LANE), BF16),
        ],
        scratch_shapes=[pltpu.VMEM((tm, D_MODEL), BF16)],
        compiler_params=_params("parallel"),
        name="mla_proj",
    )(x, norm_g, mod, mod, wqa, gq, wq, wqs, wkv, gkv, cos_t, sin_t)


def _kv_expand_kernel(c_ref, kpe_ref, wk_ref, wv_ref, k_ref, v_ref):
    c = c_ref[...].astype(BF16)
    kn = _dot(c, wk_ref[...])
    v_ref[...] = _dot(c, wv_ref[...]).astype(BF16)
    kpe = kpe_ref[...]
    for hd in range(N_HEADS):
        lo = hd * HEAD_PAD
        k_ref[:, lo:lo + LANE] = kn[:, hd * LANE:(hd + 1) * LANE].astype(BF16)
        k_ref[:, lo + LANE:lo + HEAD_PAD] = kpe


def _kv_expand(ckv, kpe, wk, wv, n_rows, tr=512):
    const = lambda i: (0, 0)
    row = lambda i: (i, 0)
    return pl.pallas_call(
        _kv_expand_kernel,
        grid=(n_rows // tr,),
        in_specs=[
            pl.BlockSpec((tr, KV_RANK), row),
            pl.BlockSpec((tr, LANE), row),
            pl.BlockSpec(wk.shape, const),
            pl.BlockSpec(wv.shape, const),
        ],
        out_specs=[
            pl.BlockSpec((tr, N_HEADS * HEAD_PAD), row),
            pl.BlockSpec((tr, N_HEADS * V_DIM), row),
        ],
        out_shape=[
            jax.ShapeDtypeStruct((n_rows, N_HEADS * HEAD_PAD), BF16),
            jax.ShapeDtypeStruct((n_rows, N_HEADS * V_DIM), BF16),
        ],
        compiler_params=_params("parallel"),
        name="kv_expand",
    )(ckv, kpe, wk, wv)


def _attn_kernel(q_ref, k_ref, v_ref, o_ref, *, heads):
    scale = (QK_NOPE + QK_ROPE) ** -0.5
    for hd in range(heads):
        q = q_ref[:, hd * HEAD_PAD:(hd + 1) * HEAD_PAD]
        k = k_ref[:, hd * HEAD_PAD:(hd + 1) * HEAD_PAD]
        s = lax.dot_general(q, k, (((1,), (1,)), ((), ())), preferred_element_type=F32) * scale
        p = jnp.exp(s - jnp.max(s, axis=-1, keepdims=True))
        denom = jnp.sum(p, axis=-1, keepdims=True)
        o = _dot(p.astype(BF16), v_ref[:, hd * V_DIM:(hd + 1) * V_DIM])
        o_ref[:, hd * V_DIM:(hd + 1) * V_DIM] = (o / denom).astype(BF16)


def _attention(q, k, v, n_batch, lq, lk, q_row0, heads, tq=256):
    nq = lq // tq
    q0 = q_row0 // tq
    return pl.pallas_call(
        functools.partial(_attn_kernel, heads=heads),
        grid=(n_batch, N_HEADS // heads, nq),
        in_specs=[
            pl.BlockSpec((tq, heads * HEAD_PAD), lambda b, g, i: (q0 + b * nq + i, g)),
            pl.BlockSpec((lk, heads * HEAD_PAD), lambda b, g, i: (b, g)),
            pl.BlockSpec((lk, heads * V_DIM), lambda b, g, i: (b, g)),
        ],
        out_specs=pl.BlockSpec((tq, heads * V_DIM), lambda b, g, i: (b * nq + i, g)),
        out_shape=jax.ShapeDtypeStruct((n_batch * lq, N_HEADS * V_DIM), BF16),
        compiler_params=_params("parallel", "parallel", "arbitrary"),
        name="attention",
    )(q, k, v)


def _mlp_kernel(x_ref, g_ref, sh_ref, sc_ref, gate_ref, w1_ref, w2_ref, gf_ref, o_ref, h_ref, *, tm, final):
    f = pl.program_id(1)

    @pl.when(f == 0)
    def _():
        _norm_mod_rows(x_ref, g_ref, sh_ref, sc_ref, h_ref, tm)

    hid = _dot(h_ref[...], w1_ref[...])
    hid = jnp.square(jnp.maximum(hid, 0.0)).astype(BF16)
    part = _dot(hid, w2_ref[...])

    @pl.when(f == 0)
    def _():
        o_ref[...] = part

    @pl.when(f > 0)
    def _():
        o_ref[...] += part

    @pl.when(f == pl.num_programs(1) - 1)
    def _():
        gate = gate_ref[0]
        gf = gf_ref[...]
        rows = 16

        def body(c, carry):
            r = pl.multiple_of(c * rows, rows)
            y = x_ref[pl.ds(r, rows), :] + gate * o_ref[pl.ds(r, rows), :]
            if final:
                y = y * lax.rsqrt(jnp.mean(y * y, axis=-1, keepdims=True) + EPS) * gf
            o_ref[pl.ds(r, rows), :] = y
            return carry

        lax.fori_loop(0, tm // rows, body, 0)


def _mlp(x, mod, norm_g, w1, w2, final_g, layer, final, tm=512, tf=1024):
    row = lambda i, f: (i, 0)
    const = lambda i, f: (0, 0)
    return pl.pallas_call(
        functools.partial(_mlp_kernel, tm=tm, final=final),
        grid=(T_ALL // tm, D_FF // tf),
        in_specs=[
            pl.BlockSpec((tm, D_MODEL), row),
            pl.BlockSpec((1, D_MODEL), const),
            _mod_spec(layer, 3, tm),
            _mod_spec(layer, 4, tm),
            _mod_spec(layer, 5, tm),
            pl.BlockSpec((D_MODEL, tf), lambda i, f: (0, f)),
            pl.BlockSpec((tf, D_MODEL), lambda i, f: (f, 0)),
            pl.BlockSpec((1, D_MODEL), const),
        ],
        out_specs=pl.BlockSpec((tm, D_MODEL), row),
        out_shape=jax.ShapeDtypeStruct((T_ALL, D_MODEL), F32),
        scratch_shapes=[pltpu.VMEM((tm, D_MODEL), BF16)],
        compiler_params=_params("parallel", "arbitrary"),
        name="mlp",
    )(x, norm_g, mod, mod, mod, w1, w2, final_g)


def _rope_tables(tm):
    n_rows = DEC_SEQ // GRID_W
    rows = jnp.broadcast_to(jnp.arange(n_rows)[:, None], (n_rows, GRID_W)).reshape(-1)
    cols = jnp.broadcast_to(jnp.arange(GRID_W)[None, :], (n_rows, GRID_W)).reshape(-1)
    nf = QK_ROPE // 4
    inv = 1.0 / (ROPE_THETA ** (jnp.arange(nf, dtype=F32) / nf))
    ang_r = rows.astype(F32)[:, None] * inv
    ang_c = cols.astype(F32)[:, None] * inv
    cr, sr, cc, sc = jnp.cos(ang_r), jnp.sin(ang_r), jnp.cos(ang_c), jnp.sin(ang_c)
    zeros = jnp.zeros((DEC_SEQ, LANE - QK_ROPE), F32)
    cos = jnp.concatenate([cr, cr, cc, cc, zeros], axis=1)
    sin = jnp.concatenate([-sr, sr, -sc, sc, zeros], axis=1)
    ident_cos = jnp.concatenate([jnp.ones((tm, QK_ROPE), F32), jnp.zeros((tm, LANE - QK_ROPE), F32)], axis=1)
    cos = jnp.concatenate([ident_cos, cos], axis=0)
    sin = jnp.concatenate([jnp.zeros((tm, LANE), F32), sin], axis=0)
    return cos, sin


def _swap_rope_partners(w):
    q = QK_ROPE // 4
    return jnp.concatenate([w[..., q:2 * q], w[..., :q], w[..., 3 * q:], w[..., 2 * q:3 * q]], axis=-1)


def _mla_weights(w_q_b, w_kv_a, w_kv_b):
    wq = w_q_b.reshape(Q_RANK, N_HEADS, QK_NOPE + QK_ROPE)
    pad = jnp.zeros((Q_RANK, N_HEADS, HEAD_PAD - QK_NOPE - QK_ROPE), w_q_b.dtype)
    wq_full = jnp.concatenate([wq, pad], axis=-1).reshape(Q_RANK, N_HEADS * HEAD_PAD)
    wq_swap = jnp.concatenate([_swap_rope_partners(wq[..., QK_NOPE:]), pad], axis=-1).reshape(Q_RANK, N_HEADS * LANE)
    kpe_w = w_kv_a[:, KV_RANK:]
    zpad = jnp.zeros((D_MODEL, LANE - QK_ROPE), w_kv_a.dtype)
    wkv = jnp.concatenate([w_kv_a[:, :KV_RANK], kpe_w, zpad, _swap_rope_partners(kpe_w), zpad], axis=1)
    wkvb = w_kv_b.reshape(KV_RANK, N_HEADS, QK_NOPE + V_DIM)
    wk = wkvb[..., :QK_NOPE].reshape(KV_RANK, N_HEADS * QK_NOPE)
    wv = wkvb[..., QK_NOPE:].reshape(KV_RANK, N_HEADS * V_DIM)
    return wq_full.astype(BF16), wq_swap.astype(BF16), wkv.astype(BF16), wk.astype(BF16), wv.astype(BF16)


def kernel(x_prompt, x_sample, cache_ckv, cache_kpe, c, c_ctx, ada_w, ada_b, norm1, norm2, conv_w_in, conv_w, conv_w_out, gmlp_w_in, gmlp_g_v, gmlp_w_s, gmlp_b_s, gmlp_w_out, mla_w_q_a, mla_g_q, mla_w_q_b, mla_w_kv_a, mla_g_kv, mla_w_kv_b, mla_w_o, mlp_w1, mlp_w2, final_norm):
    D = D_MODEL
    x = jnp.concatenate([x_prompt.reshape(T_PROMPT, D), x_sample.reshape(T_SAMPLE, D)], axis=0)
    cond = jnp.concatenate([c_ctx[None, :], c, jnp.zeros((N_GROUPS - 1 - DEC_BATCH, D), F32)], axis=0)
    mod = _modvec(cond, ada_w, ada_b)

    mla_tm = 256
    cos_t, sin_t = _rope_tables(mla_tm)
    new_ckv, new_kpe = [], []
    for i in range(DEPTH):
        kind, j = i % N_MIXERS, i // N_MIXERS
        g1 = norm1[i][None, :]
        if kind == 0:
            bg, p = _conv_in(x, mod, g1, conv_w_in[j].astype(BF16), i)
            x = _conv_out(bg, p, conv_w[j], x, mod, conv_w_out[j].astype(BF16), i)
        elif kind == 1:
            z = _gmlp_in(x, mod, g1, gmlp_w_in[j].astype(BF16), i)
            b_full = jnp.repeat(gmlp_b_s[j].T, LANE, axis=1)
            x = _gmlp_out(z, gmlp_g_v[j][None, :], gmlp_w_s[j].astype(BF16), b_full, x, mod,
                          gmlp_w_out[j].astype(BF16), i)
        else:
            wq, wqs, wkv, wk, wv = _mla_weights(mla_w_q_b[j], mla_w_kv_a[j], mla_w_kv_b[j])
            q, ckv, kpe, kpe_rot = _mla_proj(x, mod, g1, mla_w_q_a[j].astype(BF16), mla_g_q[j][None, :], wq, wqs, wkv,
                                             mla_g_kv[j][None, :], cos_t, sin_t, i, tm=mla_tm)
            new_ckv.append(ckv[:T_PROMPT].reshape(BATCH, SEQ, KV_RANK))
            new_kpe.append(kpe[:T_PROMPT, :QK_ROPE].reshape(BATCH, SEQ, QK_ROPE))
            k_p, v_p = _kv_expand(ckv, kpe_rot, wk, wv, T_PROMPT)
            o_p = _attention(q, k_p, v_p, BATCH, SEQ, SEQ, 0, heads=N_HEADS)
            ckv_s = jnp.concatenate([cache_ckv[:, j], ckv[T_PROMPT:].reshape(DEC_BATCH, DEC_SEQ, KV_RANK)], axis=1)
            cache_pe = jnp.pad(cache_kpe[:, j], ((0, 0), (0, 0), (0, LANE - QK_ROPE))).astype(BF16)
            kpe_s = jnp.concatenate([cache_pe, kpe_rot[T_PROMPT:].reshape(DEC_BATCH, DEC_SEQ, LANE)], axis=1)
            lk = PAST_LEN + DEC_SEQ
            k_s, v_s = _kv_expand(ckv_s.reshape(DEC_BATCH * lk, KV_RANK), kpe_s.reshape(DEC_BATCH * lk, LANE),
                                  wk, wv, DEC_BATCH * lk)
            o_s = _attention(q, k_s, v_s, DEC_BATCH, DEC_SEQ, lk, T_PROMPT, heads=4)
            x = _attn_out(jnp.concatenate([o_p, o_s], axis=0), x, mod, mla_w_o[j].astype(BF16), i)
        x = _mlp(x, mod, norm2[i][None, :], mlp_w1[i].astype(BF16), mlp_w2[i].astype(BF16),
                 final_norm[None, :], i, final=(i == DEPTH - 1))
    y_prompt = x[:T_PROMPT].reshape(BATCH, SEQ, D)
    y_sample = x[T_PROMPT:].reshape(DEC_BATCH, DEC_SEQ, D)
    return (y_prompt, y_sample, jnp.stack(new_ckv, axis=1), jnp.stack(new_kpe, axis=1))
```

```python
import functools
from typing import NamedTuple

import jax
import jax.numpy as jnp
from jax import lax
from jax.experimental import pallas as pl
from jax.experimental.pallas import tpu as pltpu

D_MODEL = 2048
BATCH = 32
SEQ = 256
DEPTH = 4
DEC_BATCH = 4
DEC_SEQ = 2048
PAST_LEN = 512
GRID_W = 64
N_MIXERS = 3
CONV_WIDTH = 3
CHUNK = 128
GMLP_GROUPS = 16
N_HEADS = 16
QK_NOPE = 128
QK_ROPE = 64
V_DIM = 128
Q_RANK = 512
KV_RANK = 512
ROPE_THETA = 10000.0
D_FF = 4 * D_MODEL
N_MOD = 6
EPS = 1e-6

N_GROUPS = 8
HEAD_PAD = 256
LANE = 128
BF16_ROWS = 16
SUB = 128
VMEM_LIMIT = 56 * 1024 * 1024

BF16 = jnp.bfloat16
F32 = jnp.float32


class Stream(NamedTuple):
    rows: int
    seq: int
    group0: int
    group_rows: int
    rope: bool


PROMPT = Stream(BATCH * SEQ, SEQ, 0, BATCH * SEQ, False)
SAMPLE = Stream(DEC_BATCH * DEC_SEQ, DEC_SEQ, 1, DEC_SEQ, True)


def _params(*sem):
    return pltpu.CompilerParams(dimension_semantics=sem, vmem_limit_bytes=VMEM_LIMIT)


def _dot(a, b):
    return jnp.dot(a, b, preferred_element_type=F32)


def _resident(shape):
    zeros = (0,) * len(shape)
    return pl.BlockSpec(shape, lambda *_: zeros, pipeline_mode=pl.Buffered(1))


def _mod_spec(st, layer, chunk, tm):
    def index(i, *_):
        return ((layer * N_MOD + chunk) * N_GROUPS + st.group0 + (i * tm) // st.group_rows, 0, 0)

    return pl.BlockSpec((1, 1, D_MODEL), index)


def _norm_mod_rows(x_ref, g_ref, sh_ref, sc_ref, h_ref, tm, rows=16):
    gsc = g_ref[...] * (1.0 + sc_ref[0])
    sh = sh_ref[0]

    def body(c, carry):
        r = pl.multiple_of(c * rows, rows)
        x = x_ref[pl.ds(r, rows), :]
        ms = jnp.mean(x * x, axis=-1, keepdims=True)
        h_ref[pl.ds(r, rows), :] = (x * lax.rsqrt(ms + EPS) * gsc + sh).astype(BF16)
        return carry

    lax.fori_loop(0, tm // rows, body, 0, unroll=2)


def _modvec_kernel(c_ref, w_ref, b_ref, o_ref):
    s = jax.nn.silu(c_ref[...]).astype(BF16)
    o_ref[0, 0] = _dot(s, w_ref[0].astype(BF16)) + b_ref[0]


def _modvec(cond, ada_w, ada_b, tn=1024):
    per_chunk = D_MODEL // tn
    out = pl.pallas_call(
        _modvec_kernel,
        grid=(DEPTH, N_MOD * per_chunk),
        in_specs=[
            pl.BlockSpec((N_GROUPS, D_MODEL), lambda l, j: (0, 0)),
            pl.BlockSpec((1, D_MODEL, tn), lambda l, j: (l, 0, j)),
            pl.BlockSpec((1, 1, tn), lambda l, j: (l, 0, j)),
        ],
        out_specs=pl.BlockSpec((1, 1, N_GROUPS, tn), lambda l, j: (l, j // per_chunk, 0, j % per_chunk)),
        out_shape=jax.ShapeDtypeStruct((DEPTH, N_MOD, N_GROUPS, D_MODEL), F32),
        compiler_params=_params("arbitrary", "arbitrary"),
        name="modvec",
    )(cond, ada_w, ada_b.reshape(DEPTH, 1, N_MOD * D_MODEL))
    return out.reshape(DEPTH * N_MOD * N_GROUPS, 1, D_MODEL)


def _conv_in_kernel(x_ref, g_ref, sh_ref, sc_ref, wb_ref, wc_ref, wh_ref, bg_ref, p_ref, h_ref, *, tm):
    @pl.when(pl.program_id(1) == 0)
    def _():
        _norm_mod_rows(x_ref, g_ref, sh_ref, sc_ref, h_ref, tm)

    h = h_ref[...]
    bg_ref[...] = _dot(h, wb_ref[...]).astype(BF16)
    p_ref[...] = (_dot(h, wc_ref[...]) * _dot(h, wh_ref[...])).astype(BF16)


def _conv_in(st, x, mod, norm_g, w_in, layer, tm=1024, tn=512):
    nj = D_MODEL // tn
    return pl.pallas_call(
        functools.partial(_conv_in_kernel, tm=tm),
        grid=(st.rows // tm, nj),
        in_specs=[
            pl.BlockSpec((tm, D_MODEL), lambda i, j: (i, 0)),
            _resident((1, D_MODEL)),
            _mod_spec(st, layer, 0, tm),
            _mod_spec(st, layer, 1, tm),
            pl.BlockSpec((D_MODEL, tn), lambda i, j: (0, j)),
            pl.BlockSpec((D_MODEL, tn), lambda i, j: (0, nj + j)),
            pl.BlockSpec((D_MODEL, tn), lambda i, j: (0, 2 * nj + j)),
        ],
        out_specs=[pl.BlockSpec((tm, tn), lambda i, j: (i, j))] * 2,
        out_shape=[jax.ShapeDtypeStruct((st.rows, D_MODEL), BF16)] * 2,
        scratch_shapes=[pltpu.VMEM((tm, D_MODEL), BF16)],
        compiler_params=_params("parallel", "arbitrary"),
        name="conv_in",
    )(x, norm_g, mod, mod, w_in, w_in, w_in)


def _gmlp_in_kernel(x_ref, g_ref, sh_ref, sc_ref, w_ref, z_ref, h_ref, *, tm):
    @pl.when(pl.program_id(1) == 0)
    def _():
        _norm_mod_rows(x_ref, g_ref, sh_ref, sc_ref, h_ref, tm)

    z_ref[...] = _dot(h_ref[...], w_ref[...]).astype(BF16)


def _gmlp_in(st, x, mod, norm_g, w_in, layer, tm=1024, tn=1024):
    n_out = w_in.shape[1]
    return pl.pallas_call(
        functools.partial(_gmlp_in_kernel, tm=tm),
        grid=(st.rows // tm, n_out // tn),
        in_specs=[
            pl.BlockSpec((tm, D_MODEL), lambda i, j: (i, 0)),
            _resident((1, D_MODEL)),
            _mod_spec(st, layer, 0, tm),
            _mod_spec(st, layer, 1, tm),
            pl.BlockSpec((D_MODEL, tn), lambda i, j: (0, j)),
        ],
        out_specs=pl.BlockSpec((tm, tn), lambda i, j: (i, j)),
        out_shape=jax.ShapeDtypeStruct((st.rows, n_out), BF16),
        scratch_shapes=[pltpu.VMEM((tm, D_MODEL), BF16)],
        compiler_params=_params("parallel", "arbitrary"),
        name="gmlp_in",
    )(x, norm_g, mod, mod, w_in)


def _residual_proj(a, rs, x_ref, gate_ref, w_ref, o_ref):
    o_ref[rs, :] = x_ref[rs, :] + gate_ref[0] * _dot(a, w_ref[...])


def _conv_out_kernel(bg_ref, p_ref, lo_ref, hi_ref, cw_ref, x_ref, gate_ref, w_ref, o_ref, pbuf_ref, *, tm, seq):
    halo = BF16_ROWS
    pbuf_ref[0:halo, :] = lo_ref[...].astype(F32)
    pbuf_ref[halo + tm:2 * halo + tm, :] = hi_ref[...].astype(F32)

    def copy(c, carry):
        r = pl.multiple_of(c * halo, halo)
        pbuf_ref[pl.ds(halo + r, halo), :] = p_ref[pl.ds(r, halo), :].astype(F32)
        return carry

    lax.fori_loop(0, tm // halo, copy, 0, unroll=4)

    base = pl.program_id(0) * tm
    width = 4 * LANE
    for s in range(tm // SUB):
        r0 = s * SUB
        pos = (base + r0 + lax.broadcasted_iota(jnp.int32, (SUB, 1), 0)) % seq
        parts = []
        for lo in range(0, D_MODEL, width):
            cols = slice(lo, lo + width)
            prev = jnp.where(pos == 0, 0.0, pbuf_ref[halo - 1 + r0:halo - 1 + r0 + SUB, cols])
            nxt = jnp.where(pos == seq - 1, 0.0, pbuf_ref[halo + 1 + r0:halo + 1 + r0 + SUB, cols])
            cur = pbuf_ref[halo + r0:halo + r0 + SUB, cols]
            conv = prev * cw_ref[0:1, cols] + cur * cw_ref[1:2, cols] + nxt * cw_ref[2:3, cols]
            parts.append((bg_ref[r0:r0 + SUB, cols].astype(F32) * conv).astype(BF16))
        _residual_proj(jnp.concatenate(parts, axis=1), slice(r0, r0 + SUB), x_ref, gate_ref, w_ref, o_ref)


def _conv_out(st, bg, p, conv_w, x, mod, w_out, layer, tm=512):
    nb = tm // BF16_ROWS
    last_halo = st.rows // BF16_ROWS - 1
    row = lambda i: (i, 0)
    return pl.pallas_call(
        functools.partial(_conv_out_kernel, tm=tm, seq=st.seq),
        grid=(st.rows // tm,),
        in_specs=[
            pl.BlockSpec((tm, D_MODEL), row),
            pl.BlockSpec((tm, D_MODEL), row),
            pl.BlockSpec((BF16_ROWS, D_MODEL), lambda i: (jnp.maximum(i * nb - 1, 0), 0)),
            pl.BlockSpec((BF16_ROWS, D_MODEL), lambda i: (jnp.minimum((i + 1) * nb, last_halo), 0)),
            _resident((CONV_WIDTH, D_MODEL)),
            pl.BlockSpec((tm, D_MODEL), row),
            _mod_spec(st, layer, 2, tm),
            _resident((D_MODEL, D_MODEL)),
        ],
        out_specs=pl.BlockSpec((tm, D_MODEL), row),
        out_shape=jax.ShapeDtypeStruct((st.rows, D_MODEL), F32),
        scratch_shapes=[pltpu.VMEM((tm + 2 * BF16_ROWS, D_MODEL), F32)],
        compiler_params=_params("parallel"),
        name="conv_out",
    )(bg, p, p, p, conv_w, x, mod, w_out)


def _gmlp_out_kernel(zu_ref, zv_ref, gv_ref, ws_ref, bs_ref, x_ref, gate_ref, w_ref, o_ref, a_ref, vn_ref, *, tm):
    gv = gv_ref[...]
    rows = BF16_ROWS

    def norm(c, carry):
        r = pl.multiple_of(c * rows, rows)
        v = zv_ref[pl.ds(r, rows), :].astype(F32)
        ms = jnp.mean(v * v, axis=-1, keepdims=True)
        vn_ref[pl.ds(r, rows), :] = (v * lax.rsqrt(ms + EPS) * gv).astype(BF16)
        return carry

    lax.fori_loop(0, tm // rows, norm, 0, unroll=2)

    n_chunks = tm // CHUNK
    for g in range(GMLP_GROUPS):
        cols = slice(g * LANE, (g + 1) * LANE)
        rhs = jnp.concatenate([vn_ref[c * CHUNK:(c + 1) * CHUNK, cols] for c in range(n_chunks)], axis=1)
        mixed = _dot(ws_ref[g], rhs)
        for c in range(n_chunks):
            rs = slice(c * CHUNK, (c + 1) * CHUNK)
            vv = mixed[:, c * CHUNK:(c + 1) * CHUNK] + bs_ref[:, cols]
            a_ref[rs, cols] = (zu_ref[rs, cols].astype(F32) * vv).astype(BF16)

    for s in range(tm // SUB):
        rs = slice(s * SUB, (s + 1) * SUB)
        _residual_proj(a_ref[rs, :], rs, x_ref, gate_ref, w_ref, o_ref)


def _gmlp_out(st, z, g_v, w_s, b_full, x, mod, w_out, layer, tm=512):
    row = lambda i: (i, 0)
    return pl.pallas_call(
        functools.partial(_gmlp_out_kernel, tm=tm),
        grid=(st.rows // tm,),
        in_specs=[
            pl.BlockSpec((tm, D_MODEL), lambda i: (i, 0)),
            pl.BlockSpec((tm, D_MODEL), lambda i: (i, 1)),
            _resident((1, D_MODEL)),
            _resident((GMLP_GROUPS, CHUNK, CHUNK)),
            _resident((CHUNK, D_MODEL)),
            pl.BlockSpec((tm, D_MODEL), row),
            _mod_spec(st, layer, 2, tm),
            _resident((D_MODEL, D_MODEL)),
        ],
        out_specs=pl.BlockSpec((tm, D_MODEL), row),
        out_shape=jax.ShapeDtypeStruct((st.rows, D_MODEL), F32),
        scratch_shapes=[pltpu.VMEM((tm, D_MODEL), BF16), pltpu.VMEM((tm, D_MODEL), BF16)],
        compiler_params=_params("parallel"),
        name="gmlp_out",
    )(z, z, g_v, w_s, b_full, x, mod, w_out)


def _attn_out_kernel(a_ref, x_ref, gate_ref, w_ref, o_ref, *, tm):
    for s in range(tm // SUB):
        rs = slice(s * SUB, (s + 1) * SUB)
        _residual_proj(a_ref[rs, :], rs, x_ref, gate_ref, w_ref, o_ref)


def _attn_out(st, a, x, mod, w_o, layer, tm=512):
    row = lambda i: (i, 0)
    return pl.pallas_call(
        functools.partial(_attn_out_kernel, tm=tm),
        grid=(st.rows // tm,),
        in_specs=[
            pl.BlockSpec((tm, D_MODEL), row),
            pl.BlockSpec((tm, D_MODEL), row),
            _mod_spec(st, layer, 2, tm),
            _resident((D_MODEL, D_MODEL)),
        ],
        out_specs=pl.BlockSpec((tm, D_MODEL), row),
        out_shape=jax.ShapeDtypeStruct((st.rows, D_MODEL), F32),
        compiler_params=_params("parallel"),
        name="attn_out",
    )(a, x, mod, w_o)


def _mla_proj_kernel(x_ref, g_ref, sh_ref, sc_ref, wqa_ref, gq_ref, wq_ref, wqs_ref, wkv_ref, gkv_ref,
                     cos_ref, sin_ref, q_ref, ckv_ref, kpe_ref, kper_ref, h_ref, *, tm, rope):
    _norm_mod_rows(x_ref, g_ref, sh_ref, sc_ref, h_ref, tm)
    h = h_ref[...]

    qa = _dot(h, wqa_ref[...])
    qa = qa * lax.rsqrt(jnp.mean(qa * qa, axis=-1, keepdims=True) + EPS) * gq_ref[...]
    qa = qa.astype(BF16)
    q = _dot(qa, wq_ref[...])
    kv = _dot(h, wkv_ref[...])
    c = kv[:, :KV_RANK]
    ckv_ref[...] = c * lax.rsqrt(jnp.mean(c * c, axis=-1, keepdims=True) + EPS) * gkv_ref[...]
    kpe = kv[:, KV_RANK:KV_RANK + LANE]
    kpe_ref[...] = kpe
    if rope:
        cos = cos_ref[...]
        sin = sin_ref[...]
        qs = _dot(qa, wqs_ref[...])
        for hd in range(N_HEADS):
            lo = hd * HEAD_PAD
            q_ref[:, lo:lo + LANE] = q[:, lo:lo + LANE].astype(BF16)
            rot = q[:, lo + LANE:lo + HEAD_PAD] * cos + qs[:, hd * LANE:(hd + 1) * LANE] * sin
            q_ref[:, lo + LANE:lo + HEAD_PAD] = rot.astype(BF16)
        kper_ref[...] = (kpe * cos + kv[:, KV_RANK + LANE:] * sin).astype(BF16)
    else:
        q_ref[...] = q.astype(BF16)
        kper_ref[...] = kpe.astype(BF16)


def _mla_proj(st, x, mod, norm_g, wqa, gq, wq, wqs, wkv, gkv, cos_t, sin_t, layer, tm=256):
    per_seq = st.seq // tm
    rope_idx = (lambda i: (i % per_seq, 0)) if st.rope else (lambda i: (0, 0))
    row = lambda i: (i, 0)
    return pl.pallas_call(
        functools.partial(_mla_proj_kernel, tm=tm, rope=st.rope),
        grid=(st.rows // tm,),
        in_specs=[
            pl.BlockSpec((tm, D_MODEL), row),
            _resident((1, D_MODEL)),
            _mod_spec(st, layer, 0, tm),
            _mod_spec(st, layer, 1, tm),
            _resident(wqa.shape),
            _resident(gq.shape),
            _resident(wq.shape),
            _resident(wqs.shape),
            _resident(wkv.shape),
            _resident(gkv.shape),
            pl.BlockSpec((tm, LANE), rope_idx),
            pl.BlockSpec((tm, LANE), rope_idx),
        ],
        out_specs=[
            pl.BlockSpec((tm, N_HEADS * HEAD_PAD), row),
            pl.BlockSpec((tm, KV_RANK), row),
            pl.BlockSpec((tm, LANE), row),
            pl.BlockSpec((tm, LANE), row),
        ],
        out_shape=[
            jax.ShapeDtypeStruct((st.rows, N_HEADS * HEAD_PAD), BF16),
            jax.ShapeDtypeStruct((st.rows, KV_RANK), F32),
            jax.ShapeDtypeStruct((st.rows, LANE), F32),
            jax.ShapeDtypeStruct((st.rows, LANE), BF16),
        ],
        scratch_shapes=[pltpu.VMEM((tm, D_MODEL), BF16)],
        compiler_params=_params("parallel"),
        name="mla_proj",
    )(x, norm_g, mod, mod, wqa, gq, wq, wqs, wkv, gkv, cos_t, sin_t)


def _kv_expand_kernel(c_ref, kpe_ref, wk_ref, wv_ref, k_ref, v_ref):
    c = c_ref[...].astype(BF16)
    kn = _dot(c, wk_ref[...])
    v_ref[...] = _dot(c, wv_ref[...]).astype(BF16)
    kpe = kpe_ref[...]
    for hd in range(N_HEADS):
        lo = hd * HEAD_PAD
        k_ref[:, lo:lo + LANE] = kn[:, hd * LANE:(hd + 1) * LANE].astype(BF16)
        k_ref[:, lo + LANE:lo + HEAD_PAD] = kpe


def _kv_expand(ckv, kpe, wk, wv, tr=512):
    n_rows = ckv.shape[0]
    row = lambda i: (i, 0)
    return pl.pallas_call(
        _kv_expand_kernel,
        grid=(n_rows // tr,),
        in_specs=[
            pl.BlockSpec((tr, KV_RANK), row),
            pl.BlockSpec((tr, LANE), row),
            _resident(wk.shape),
            _resident(wv.shape),
        ],
        out_specs=[
            pl.BlockSpec((tr, N_HEADS * HEAD_PAD), row),
            pl.BlockSpec((tr, N_HEADS * V_DIM), row),
        ],
        out_shape=[
            jax.ShapeDtypeStruct((n_rows, N_HEADS * HEAD_PAD), BF16),
            jax.ShapeDtypeStruct((n_rows, N_HEADS * V_DIM), BF16),
        ],
        compiler_params=_params("parallel"),
        name="kv_expand",
    )(ckv, kpe, wk, wv)


def _attn_kernel(q_ref, k_ref, v_ref, o_ref, *, heads):
    scale = (QK_NOPE + QK_ROPE) ** -0.5
    for hd in range(heads):
        q = q_ref[:, hd * HEAD_PAD:(hd + 1) * HEAD_PAD]
        k = k_ref[:, hd * HEAD_PAD:(hd + 1) * HEAD_PAD]
        s = lax.dot_general(q, k, (((1,), (1,)), ((), ())), preferred_element_type=F32) * scale
        p = jnp.exp(s - jnp.max(s, axis=-1, keepdims=True))
        denom = jnp.sum(p, axis=-1, keepdims=True)
        o = _dot(p.astype(BF16), v_ref[:, hd * V_DIM:(hd + 1) * V_DIM])
        o_ref[:, hd * V_DIM:(hd + 1) * V_DIM] = (o / denom).astype(BF16)


def _attention(q, k, v, n_batch, lq, lk, heads, tq=256):
    nq = lq // tq
    return pl.pallas_call(
        functools.partial(_attn_kernel, heads=heads),
        grid=(n_batch, N_HEADS // heads, nq),
        in_specs=[
            pl.BlockSpec((tq, heads * HEAD_PAD), lambda b, g, i: (b * nq + i, g)),
            pl.BlockSpec((lk, heads * HEAD_PAD), lambda b, g, i: (b, g)),
            pl.BlockSpec((lk, heads * V_DIM), lambda b, g, i: (b, g)),
        ],
        out_specs=pl.BlockSpec((tq, heads * V_DIM), lambda b, g, i: (b * nq + i, g)),
        out_shape=jax.ShapeDtypeStruct((n_batch * lq, N_HEADS * V_DIM), BF16),
        compiler_params=_params("parallel", "parallel", "arbitrary"),
        name="attention",
    )(q, k, v)


def _mlp_kernel(x_ref, g_ref, sh_ref, sc_ref, gate_ref, w1_ref, w2_ref, gf_ref, o_ref, h_ref, *, tm, final):
    f = pl.program_id(1)

    @pl.when(f == 0)
    def _():
        _norm_mod_rows(x_ref, g_ref, sh_ref, sc_ref, h_ref, tm)
        o_ref[...] = jnp.zeros_like(o_ref)

    hid = _dot(h_ref[...], w1_ref[...])
    hid = jnp.square(jnp.maximum(hid, 0.0)).astype(BF16)
    o_ref[...] += _dot(hid, w2_ref[...])

    @pl.when(f == pl.num_programs(1) - 1)
    def _():
        gate = gate_ref[0]
        gf = gf_ref[...]
        rows = 16

        def body(c, carry):
            r = pl.multiple_of(c * rows, rows)
            y = x_ref[pl.ds(r, rows), :] + gate * o_ref[pl.ds(r, rows), :]
            if final:
                y = y * lax.rsqrt(jnp.mean(y * y, axis=-1, keepdims=True) + EPS) * gf
            o_ref[pl.ds(r, rows), :] = y
            return carry

        lax.fori_loop(0, tm // rows, body, 0, unroll=2)


def _mlp(st, x, mod, norm_g, w1, w2, final_g, layer, final, tm=512, tf=1024):
    row = lambda i, f: (i, 0)
    return pl.pallas_call(
        functools.partial(_mlp_kernel, tm=tm, final=final),
        grid=(st.rows // tm, D_FF // tf),
        in_specs=[
            pl.BlockSpec((tm, D_MODEL), row),
            _resident((1, D_MODEL)),
            _mod_spec(st, layer, 3, tm),
            _mod_spec(st, layer, 4, tm),
            _mod_spec(st, layer, 5, tm),
            pl.BlockSpec((D_MODEL, tf), lambda i, f: (0, f)),
            pl.BlockSpec((tf, D_MODEL), lambda i, f: (f, 0)),
            _resident((1, D_MODEL)),
        ],
        out_specs=pl.BlockSpec((tm, D_MODEL), row),
        out_shape=jax.ShapeDtypeStruct((st.rows, D_MODEL), F32),
        scratch_shapes=[pltpu.VMEM((tm, D_MODEL), BF16)],
        compiler_params=_params("parallel", "arbitrary"),
        name="mlp",
    )(x, norm_g, mod, mod, mod, w1, w2, final_g)


def _rope_tables():
    n_rows = DEC_SEQ // GRID_W
    rows = jnp.broadcast_to(jnp.arange(n_rows)[:, None], (n_rows, GRID_W)).reshape(-1)
    cols = jnp.broadcast_to(jnp.arange(GRID_W)[None, :], (n_rows, GRID_W)).reshape(-1)
    nf = QK_ROPE // 4
    inv = 1.0 / (ROPE_THETA ** (jnp.arange(nf, dtype=F32) / nf))
    ang_r = rows.astype(F32)[:, None] * inv
    ang_c = cols.astype(F32)[:, None] * inv
    cr, sr, cc, sc = jnp.cos(ang_r), jnp.sin(ang_r), jnp.cos(ang_c), jnp.sin(ang_c)
    zeros = jnp.zeros((DEC_SEQ, LANE - QK_ROPE), F32)
    cos = jnp.concatenate([cr, cr, cc, cc, zeros], axis=1)
    sin = jnp.concatenate([-sr, sr, -sc, sc, zeros], axis=1)
    return cos, sin


def _swap_rope_partners(w):
    q = QK_ROPE // 4
    return jnp.concatenate([w[..., q:2 * q], w[..., :q], w[..., 3 * q:], w[..., 2 * q:3 * q]], axis=-1)


def _mla_weights(w_q_b, w_kv_a, w_kv_b):
    wq = w_q_b.reshape(Q_RANK, N_HEADS, QK_NOPE + QK_ROPE)
    pad = jnp.zeros((Q_RANK, N_HEADS, HEAD_PAD - QK_NOPE - QK_ROPE), w_q_b.dtype)
    wq_full = jnp.concatenate([wq, pad], axis=-1).reshape(Q_RANK, N_HEADS * HEAD_PAD)
    wq_swap = jnp.concatenate([_swap_rope_partners(wq[..., QK_NOPE:]), pad], axis=-1).reshape(Q_RANK, N_HEADS * LANE)
    kpe_w = w_kv_a[:, KV_RANK:]
    zpad = jnp.zeros((D_MODEL, LANE - QK_ROPE), w_kv_a.dtype)
    wkv = jnp.concatenate([w_kv_a[:, :KV_RANK], kpe_w, zpad, _swap_rope_partners(kpe_w), zpad], axis=1)
    wkvb = w_kv_b.reshape(KV_RANK, N_HEADS, QK_NOPE + V_DIM)
    wk = wkvb[..., :QK_NOPE].reshape(KV_RANK, N_HEADS * QK_NOPE)
    wv = wkvb[..., QK_NOPE:].reshape(KV_RANK, N_HEADS * V_DIM)
    return wq_full.astype(BF16), wq_swap.astype(BF16), wkv.astype(BF16), wk.astype(BF16), wv.astype(BF16)


def kernel(x_prompt, x_sample, cache_ckv, cache_kpe, c, c_ctx, ada_w, ada_b, norm1, norm2, conv_w_in, conv_w, conv_w_out, gmlp_w_in, gmlp_g_v, gmlp_w_s, gmlp_b_s, gmlp_w_out, mla_w_q_a, mla_g_q, mla_w_q_b, mla_w_kv_a, mla_g_kv, mla_w_kv_b, mla_w_o, mlp_w1, mlp_w2, final_norm):
    D = D_MODEL
    streams = (PROMPT, SAMPLE)
    xs = [x_prompt.reshape(PROMPT.rows, D), x_sample.reshape(SAMPLE.rows, D)]
    cond = jnp.concatenate([c_ctx[None, :], c, jnp.zeros((N_GROUPS - 1 - DEC_BATCH, D), F32)], axis=0)
    mod = _modvec(cond, ada_w, ada_b)

    cos_t, sin_t = _rope_tables()
    new_ckv, new_kpe = [], []
    for i in range(DEPTH):
        kind, j = i % N_MIXERS, i // N_MIXERS
        g1 = norm1[i][None, :]
        if kind == 0:
            w_in, w_out = conv_w_in[j].astype(BF16), conv_w_out[j].astype(BF16)
            for n, st in enumerate(streams):
                bg, p = _conv_in(st, xs[n], mod, g1, w_in, i)
                xs[n] = _conv_out(st, bg, p, conv_w[j], xs[n], mod, w_out, i)
        elif kind == 1:
            w_in, w_out, w_s = gmlp_w_in[j].astype(BF16), gmlp_w_out[j].astype(BF16), gmlp_w_s[j].astype(BF16)
            b_full = jnp.repeat(gmlp_b_s[j].T, LANE, axis=1)
            for n, st in enumerate(streams):
                z = _gmlp_in(st, xs[n], mod, g1, w_in, i)
                xs[n] = _gmlp_out(st, z, gmlp_g_v[j][None, :], w_s, b_full, xs[n], mod, w_out, i)
        else:
            wq, wqs, wkv, wk, wv = _mla_weights(mla_w_q_b[j], mla_w_kv_a[j], mla_w_kv_b[j])
            wqa, w_o = mla_w_q_a[j].astype(BF16), mla_w_o[j].astype(BF16)
            proj = [_mla_proj(st, xs[n], mod, g1, wqa, mla_g_q[j][None, :], wq, wqs, wkv, mla_g_kv[j][None, :],
                              cos_t, sin_t, i) for n, st in enumerate(streams)]
            q, ckv, kpe, kpe_rot = proj[0]
            new_ckv.append(ckv.reshape(BATCH, SEQ, KV_RANK))
            new_kpe.append(kpe[:, :QK_ROPE].reshape(BATCH, SEQ, QK_ROPE))
            k_p, v_p = _kv_expand(ckv, kpe_rot, wk, wv)
            o_p = _attention(q, k_p, v_p, BATCH, SEQ, SEQ, heads=N_HEADS)
            q, ckv, kpe, kpe_rot = proj[1]
            lk = PAST_LEN + DEC_SEQ
            ckv_s = jnp.concatenate([cache_ckv[:, j], ckv.reshape(DEC_BATCH, DEC_SEQ, KV_RANK)], axis=1)
            cache_pe = jnp.pad(cache_kpe[:, j], ((0, 0), (0, 0), (0, LANE - QK_ROPE))).astype(BF16)
            kpe_s = jnp.concatenate([cache_pe, kpe_rot.reshape(DEC_BATCH, DEC_SEQ, LANE)], axis=1)
            k_s, v_s = _kv_expand(ckv_s.reshape(DEC_BATCH * lk, KV_RANK), kpe_s.reshape(DEC_BATCH * lk, LANE), wk, wv)
            o_s = _attention(q, k_s, v_s, DEC_BATCH, DEC_SEQ, lk, heads=4)
            xs = [_attn_out(st, o, xs[n], mod, w_o, i) for n, (st, o) in enumerate(zip(streams, (o_p, o_s)))]
        w1, w2 = mlp_w1[i].astype(BF16), mlp_w2[i].astype(BF16)
        xs = [_mlp(st, xs[n], mod, norm2[i][None, :], w1, w2, final_norm[None, :], i, final=(i == DEPTH - 1))
              for n, st in enumerate(streams)]
    y_prompt = xs[0].reshape(BATCH, SEQ, D)
    y_sample = xs[1].reshape(DEC_BATCH, DEC_SEQ, D)
    return (y_prompt, y_sample, jnp.stack(new_ckv, axis=1), jnp.stack(new_kpe, axis=1))
```

```python
import functools
from typing import NamedTuple

import jax
import jax.numpy as jnp
from jax import lax
from jax.experimental import pallas as pl
from jax.experimental.pallas import tpu as pltpu

D_MODEL = 2048
BATCH = 32
SEQ = 256
DEPTH = 4
DEC_BATCH = 4
DEC_SEQ = 2048
PAST_LEN = 512
GRID_W = 64
N_MIXERS = 3
CONV_WIDTH = 3
CHUNK = 128
GMLP_GROUPS = 16
N_HEADS = 16
QK_NOPE = 128
QK_ROPE = 64
V_DIM = 128
Q_RANK = 512
KV_RANK = 512
ROPE_THETA = 10000.0
D_FF = 4 * D_MODEL
N_MOD = 6
EPS = 1e-6

N_GROUPS = 8
HEAD_PAD = 256
LANE = 128
BF16_ROWS = 16
SUB = 128
VMEM_LIMIT = 56 * 1024 * 1024

BF16 = jnp.bfloat16
F32 = jnp.float32


class Stream(NamedTuple):
    rows: int
    seq: int
    group0: int
    group_rows: int
    rope: bool


PROMPT = Stream(BATCH * SEQ, SEQ, 0, BATCH * SEQ, False)
SAMPLE = Stream(DEC_BATCH * DEC_SEQ, DEC_SEQ, 1, DEC_SEQ, True)


def _params(*sem):
    return pltpu.CompilerParams(dimension_semantics=sem, vmem_limit_bytes=VMEM_LIMIT)


def _dot(a, b):
    return jnp.dot(a, b, preferred_element_type=F32)


def _resident(shape, layer=None):
    if layer is None:
        zeros = (0,) * len(shape)
        return pl.BlockSpec(shape, lambda *_: zeros, pipeline_mode=pl.Buffered(1))
    index = (layer,) + (0,) * len(shape)
    return pl.BlockSpec((None,) + tuple(shape), lambda *_: index, pipeline_mode=pl.Buffered(1))


def _mod_spec(st, layer, chunk, tm):
    def index(i, *_):
        return ((layer * N_MOD + chunk) * N_GROUPS + st.group0 + (i * tm) // st.group_rows, 0, 0)

    return pl.BlockSpec((1, 1, D_MODEL), index)


def _gain_shift(g_ref, sh_ref, sc_ref):
    return g_ref[...] * (1.0 + sc_ref[0]), sh_ref[0]


def _norm_mod(x, gsc, sh):
    ms = jnp.mean(x * x, axis=-1, keepdims=True)
    return (x * lax.rsqrt(ms + EPS) * gsc + sh).astype(BF16)


def _modvec_kernel(c_ref, w_ref, b_ref, o_ref):
    s = jax.nn.silu(c_ref[...]).astype(BF16)
    o_ref[0, 0] = _dot(s, w_ref[0].astype(BF16)) + b_ref[0]


def _modvec(cond, ada_w, ada_b, tn=1024):
    per_chunk = D_MODEL // tn
    out = pl.pallas_call(
        _modvec_kernel,
        grid=(DEPTH, N_MOD * per_chunk),
        in_specs=[
            pl.BlockSpec((N_GROUPS, D_MODEL), lambda l, j: (0, 0)),
            pl.BlockSpec((1, D_MODEL, tn), lambda l, j: (l, 0, j)),
            pl.BlockSpec((1, 1, tn), lambda l, j: (l, 0, j)),
        ],
        out_specs=pl.BlockSpec((1, 1, N_GROUPS, tn), lambda l, j: (l, j // per_chunk, 0, j % per_chunk)),
        out_shape=jax.ShapeDtypeStruct((DEPTH, N_MOD, N_GROUPS, D_MODEL), F32),
        compiler_params=_params("arbitrary", "arbitrary"),
        name="modvec",
    )(cond, ada_w, ada_b.reshape(DEPTH, 1, N_MOD * D_MODEL))
    return out.reshape(DEPTH * N_MOD * N_GROUPS, 1, D_MODEL)


def _conv_in_kernel(x_ref, g_ref, sh_ref, sc_ref, w_ref, bg_ref, p_ref, *, tm):
    gsc, sh = _gain_shift(g_ref, sh_ref, sc_ref)
    for s in range(tm // SUB):
        rs = slice(s * SUB, (s + 1) * SUB)
        z = _dot(_norm_mod(x_ref[rs, :], gsc, sh), w_ref[...])
        bg_ref[rs, :] = z[:, :D_MODEL].astype(BF16)
        p_ref[rs, :] = (z[:, D_MODEL:2 * D_MODEL] * z[:, 2 * D_MODEL:]).astype(BF16)


def _conv_in(st, x, mod, norm1, w_in, layer, j, tm=512):
    row = lambda i: (i, 0)
    return pl.pallas_call(
        functools.partial(_conv_in_kernel, tm=tm),
        grid=(st.rows // tm,),
        in_specs=[
            pl.BlockSpec((tm, D_MODEL), row),
            _resident((1, D_MODEL), layer),
            _mod_spec(st, layer, 0, tm),
            _mod_spec(st, layer, 1, tm),
            _resident((D_MODEL, 3 * D_MODEL), j),
        ],
        out_specs=[pl.BlockSpec((tm, D_MODEL), row)] * 2,
        out_shape=[jax.ShapeDtypeStruct((st.rows, D_MODEL), BF16)] * 2,
        compiler_params=_params("parallel"),
        name="conv_in",
    )(x, norm1, mod, mod, w_in)


def _gmlp_in_kernel(x_ref, g_ref, sh_ref, sc_ref, w_ref, z_ref, *, tm):
    gsc, sh = _gain_shift(g_ref, sh_ref, sc_ref)
    for s in range(tm // SUB):
        rs = slice(s * SUB, (s + 1) * SUB)
        z_ref[rs, :] = _dot(_norm_mod(x_ref[rs, :], gsc, sh), w_ref[...]).astype(BF16)


def _gmlp_in(st, x, mod, norm1, w_in, layer, j, tm=512):
    n_out = w_in.shape[-1]
    row = lambda i: (i, 0)
    return pl.pallas_call(
        functools.partial(_gmlp_in_kernel, tm=tm),
        grid=(st.rows // tm,),
        in_specs=[
            pl.BlockSpec((tm, D_MODEL), row),
            _resident((1, D_MODEL), layer),
            _mod_spec(st, layer, 0, tm),
            _mod_spec(st, layer, 1, tm),
            _resident((D_MODEL, n_out), j),
        ],
        out_specs=pl.BlockSpec((tm, n_out), row),
        out_shape=jax.ShapeDtypeStruct((st.rows, n_out), BF16),
        compiler_params=_params("parallel"),
        name="gmlp_in",
    )(x, norm1, mod, mod, w_in)


def _residual_proj(a, rs, x_ref, gate_ref, w_ref, gsc2, sh2, o_ref, h_ref):
    x1 = x_ref[rs, :] + gate_ref[0] * _dot(a, w_ref[...])
    o_ref[rs, :] = x1
    h_ref[rs, :] = _norm_mod(x1, gsc2, sh2)


def _proj_specs(st, layer, j, tm):
    row = lambda i: (i, 0)
    in_specs = [
        pl.BlockSpec((tm, D_MODEL), row),
        _mod_spec(st, layer, 2, tm),
        _resident((D_MODEL, D_MODEL), j),
        _resident((1, D_MODEL), layer),
        _mod_spec(st, layer, 3, tm),
        _mod_spec(st, layer, 4, tm),
    ]
    out_specs = [pl.BlockSpec((tm, D_MODEL), row)] * 2
    out_shape = [jax.ShapeDtypeStruct((st.rows, D_MODEL), F32), jax.ShapeDtypeStruct((st.rows, D_MODEL), BF16)]
    return in_specs, out_specs, out_shape


def _conv_out_kernel(bg_ref, p_ref, lo_ref, hi_ref, cw_ref, x_ref, gate_ref, w_ref, g2_ref, sh2_ref, sc2_ref,
                     o_ref, h_ref, pbuf_ref, *, tm, seq):
    halo = BF16_ROWS
    pbuf_ref[0:halo, :] = lo_ref[...].astype(F32)
    pbuf_ref[halo + tm:2 * halo + tm, :] = hi_ref[...].astype(F32)

    def copy(c, carry):
        r = pl.multiple_of(c * halo, halo)
        pbuf_ref[pl.ds(halo + r, halo), :] = p_ref[pl.ds(r, halo), :].astype(F32)
        return carry

    lax.fori_loop(0, tm // halo, copy, 0, unroll=4)

    gsc2, sh2 = _gain_shift(g2_ref, sh2_ref, sc2_ref)
    base = pl.program_id(0) * tm
    width = 4 * LANE
    for s in range(tm // SUB):
        r0 = s * SUB
        pos = (base + r0 + lax.broadcasted_iota(jnp.int32, (SUB, 1), 0)) % seq
        parts = []
        for lo in range(0, D_MODEL, width):
            cols = slice(lo, lo + width)
            prev = jnp.where(pos == 0, 0.0, pbuf_ref[halo - 1 + r0:halo - 1 + r0 + SUB, cols])
            nxt = jnp.where(pos == seq - 1, 0.0, pbuf_ref[halo + 1 + r0:halo + 1 + r0 + SUB, cols])
            cur = pbuf_ref[halo + r0:halo + r0 + SUB, cols]
            conv = prev * cw_ref[0:1, cols] + cur * cw_ref[1:2, cols] + nxt * cw_ref[2:3, cols]
            parts.append((bg_ref[r0:r0 + SUB, cols].astype(F32) * conv).astype(BF16))
        _residual_proj(jnp.concatenate(parts, axis=1), slice(r0, r0 + SUB), x_ref, gate_ref, w_ref, gsc2, sh2,
                       o_ref, h_ref)


def _conv_out(st, bg, p, conv_w, x, mod, w_out, norm2, layer, j, tm=512):
    nb = tm // BF16_ROWS
    last_halo = st.rows // BF16_ROWS - 1
    row = lambda i: (i, 0)
    tail, out_specs, out_shape = _proj_specs(st, layer, j, tm)
    return pl.pallas_call(
        functools.partial(_conv_out_kernel, tm=tm, seq=st.seq),
        grid=(st.rows // tm,),
        in_specs=[
            pl.BlockSpec((tm, D_MODEL), row),
            pl.BlockSpec((tm, D_MODEL), row),
            pl.BlockSpec((BF16_ROWS, D_MODEL), lambda i: (jnp.maximum(i * nb - 1, 0), 0)),
            pl.BlockSpec((BF16_ROWS, D_MODEL), lambda i: (jnp.minimum((i + 1) * nb, last_halo), 0)),
            _resident((CONV_WIDTH, D_MODEL), j),
        ] + tail,
        out_specs=out_specs,
        out_shape=out_shape,
        scratch_shapes=[pltpu.VMEM((tm + 2 * BF16_ROWS, D_MODEL), F32)],
        compiler_params=_params("parallel"),
        name="conv_out",
    )(bg, p, p, p, conv_w, x, mod, w_out, norm2, mod, mod)


def _gmlp_out_kernel(zu_ref, zv_ref, gv_ref, ws_ref, bs_ref, x_ref, gate_ref, w_ref, g2_ref, sh2_ref, sc2_ref,
                     o_ref, h_ref, a_ref, vn_ref, *, tm):
    gv = gv_ref[...]
    rows = BF16_ROWS

    def norm(c, carry):
        r = pl.multiple_of(c * rows, rows)
        v = zv_ref[pl.ds(r, rows), :].astype(F32)
        ms = jnp.mean(v * v, axis=-1, keepdims=True)
        vn_ref[pl.ds(r, rows), :] = (v * lax.rsqrt(ms + EPS) * gv).astype(BF16)
        return carry

    lax.fori_loop(0, tm // rows, norm, 0, unroll=2)

    n_chunks = tm // CHUNK
    for g in range(GMLP_GROUPS):
        cols = slice(g * LANE, (g + 1) * LANE)
        rhs = jnp.concatenate([vn_ref[c * CHUNK:(c + 1) * CHUNK, cols] for c in range(n_chunks)], axis=1)
        mixed = _dot(ws_ref[g], rhs)
        for c in range(n_chunks):
            rs = slice(c * CHUNK, (c + 1) * CHUNK)
            vv = mixed[:, c * CHUNK:(c + 1) * CHUNK] + bs_ref[:, cols]
            a_ref[rs, cols] = (zu_ref[rs, cols].astype(F32) * vv).astype(BF16)

    gsc2, sh2 = _gain_shift(g2_ref, sh2_ref, sc2_ref)
    for s in range(tm // SUB):
        rs = slice(s * SUB, (s + 1) * SUB)
        _residual_proj(a_ref[rs, :], rs, x_ref, gate_ref, w_ref, gsc2, sh2, o_ref, h_ref)


def _gmlp_out(st, z, g_v, w_s, b_full, x, mod, w_out, norm2, layer, j, tm=512):
    tail, out_specs, out_shape = _proj_specs(st, layer, j, tm)
    return pl.pallas_call(
        functools.partial(_gmlp_out_kernel, tm=tm),
        grid=(st.rows // tm,),
        in_specs=[
            pl.BlockSpec((tm, D_MODEL), lambda i: (i, 0)),
            pl.BlockSpec((tm, D_MODEL), lambda i: (i, 1)),
            _resident((1, D_MODEL), j),
            _resident((GMLP_GROUPS, CHUNK, CHUNK), j),
            _resident((CHUNK, D_MODEL)),
        ] + tail,
        out_specs=out_specs,
        out_shape=out_shape,
        scratch_shapes=[pltpu.VMEM((tm, D_MODEL), BF16), pltpu.VMEM((tm, D_MODEL), BF16)],
        compiler_params=_params("parallel"),
        name="gmlp_out",
    )(z, z, g_v, w_s, b_full, x, mod, w_out, norm2, mod, mod)


def _attn_out_kernel(a_ref, x_ref, gate_ref, w_ref, g2_ref, sh2_ref, sc2_ref, o_ref, h_ref, *, tm):
    gsc2, sh2 = _gain_shift(g2_ref, sh2_ref, sc2_ref)
    for s in range(tm // SUB):
        rs = slice(s * SUB, (s + 1) * SUB)
        _residual_proj(a_ref[rs, :], rs, x_ref, gate_ref, w_ref, gsc2, sh2, o_ref, h_ref)


def _attn_out(st, a, x, mod, w_o, norm2, layer, j, tm=512):
    tail, out_specs, out_shape = _proj_specs(st, layer, j, tm)
    return pl.pallas_call(
        functools.partial(_attn_out_kernel, tm=tm),
        grid=(st.rows // tm,),
        in_specs=[pl.BlockSpec((tm, D_MODEL), lambda i: (i, 0))] + tail,
        out_specs=out_specs,
        out_shape=out_shape,
        compiler_params=_params("parallel"),
        name="attn_out",
    )(a, x, mod, w_o, norm2, mod, mod)


def _mla_proj_kernel(x_ref, g_ref, sh_ref, sc_ref, wqa_ref, gq_ref, wq_ref, wqs_ref, wkv_ref, gkv_ref,
                     cos_ref, sin_ref, q_ref, ckv_ref, kpe_ref, kper_ref, *, tm, rope):
    gsc, sh = _gain_shift(g_ref, sh_ref, sc_ref)
    for s in range(tm // SUB):
        rs = slice(s * SUB, (s + 1) * SUB)
        h = _norm_mod(x_ref[rs, :], gsc, sh)
        qa = _dot(h, wqa_ref[...])
        qa = qa * lax.rsqrt(jnp.mean(qa * qa, axis=-1, keepdims=True) + EPS) * gq_ref[...]
        qa = qa.astype(BF16)
        q = _dot(qa, wq_ref[...])
        kv = _dot(h, wkv_ref[...])
        c = kv[:, :KV_RANK]
        ckv_ref[rs, :] = c * lax.rsqrt(jnp.mean(c * c, axis=-1, keepdims=True) + EPS) * gkv_ref[...]
        kpe = kv[:, KV_RANK:KV_RANK + LANE]
        kpe_ref[rs, :] = kpe
        if rope:
            cos = cos_ref[rs, :]
            sin = sin_ref[rs, :]
            qs = _dot(qa, wqs_ref[...])
            for hd in range(N_HEADS):
                lo = hd * HEAD_PAD
                q_ref[rs, lo:lo + LANE] = q[:, lo:lo + LANE].astype(BF16)
                rot = q[:, lo + LANE:lo + HEAD_PAD] * cos + qs[:, hd * LANE:(hd + 1) * LANE] * sin
                q_ref[rs, lo + LANE:lo + HEAD_PAD] = rot.astype(BF16)
            kper_ref[rs, :] = (kpe * cos + kv[:, KV_RANK + LANE:] * sin).astype(BF16)
        else:
            q_ref[rs, :] = q.astype(BF16)
            kper_ref[rs, :] = kpe.astype(BF16)


def _mla_proj(st, x, mod, norm1, wqa, gq, wq, wqs, wkv, gkv, cos_t, sin_t, layer, j, tm=256):
    per_seq = st.seq // tm
    rope_idx = (lambda i: (i % per_seq, 0)) if st.rope else (lambda i: (0, 0))
    row = lambda i: (i, 0)
    return pl.pallas_call(
        functools.partial(_mla_proj_kernel, tm=tm, rope=st.rope),
        grid=(st.rows // tm,),
        in_specs=[
            pl.BlockSpec((tm, D_MODEL), row),
            _resident((1, D_MODEL), layer),
            _mod_spec(st, layer, 0, tm),
            _mod_spec(st, layer, 1, tm),
            _resident(wqa.shape[1:], j),
            _resident((1, Q_RANK), j),
            _resident(wq.shape),
            _resident(wqs.shape),
            _resident(wkv.shape),
            _resident((1, KV_RANK), j),
            pl.BlockSpec((tm, LANE), rope_idx),
            pl.BlockSpec((tm, LANE), rope_idx),
        ],
        out_specs=[
            pl.BlockSpec((tm, N_HEADS * HEAD_PAD), row),
            pl.BlockSpec((tm, KV_RANK), row),
            pl.BlockSpec((tm, LANE), row),
            pl.BlockSpec((tm, LANE), row),
        ],
        out_shape=[
            jax.ShapeDtypeStruct((st.rows, N_HEADS * HEAD_PAD), BF16),
            jax.ShapeDtypeStruct((st.rows, KV_RANK), F32),
            jax.ShapeDtypeStruct((st.rows, LANE), F32),
            jax.ShapeDtypeStruct((st.rows, LANE), BF16),
        ],
        compiler_params=_params("parallel"),
        name="mla_proj",
    )(x, norm1, mod, mod, wqa, gq, wq, wqs, wkv, gkv, cos_t, sin_t)


def _kv_expand_kernel(c_ref, kpe_ref, wk_ref, wv_ref, k_ref, v_ref):
    c = c_ref[...].astype(BF16)
    kn = _dot(c, wk_ref[...])
    v_ref[...] = _dot(c, wv_ref[...]).astype(BF16)
    kpe = kpe_ref[...]
    for hd in range(N_HEADS):
        lo = hd * HEAD_PAD
        k_ref[:, lo:lo + LANE] = kn[:, hd * LANE:(hd + 1) * LANE].astype(BF16)
        k_ref[:, lo + LANE:lo + HEAD_PAD] = kpe


def _kv_expand(ckv, kpe, wk, wv, tr=512):
    n_rows = ckv.shape[0]
    row = lambda i: (i, 0)
    return pl.pallas_call(
        _kv_expand_kernel,
        grid=(n_rows // tr,),
        in_specs=[
            pl.BlockSpec((tr, KV_RANK), row),
            pl.BlockSpec((tr, LANE), row),
            _resident(wk.shape),
            _resident(wv.shape),
        ],
        out_specs=[
            pl.BlockSpec((tr, N_HEADS * HEAD_PAD), row),
            pl.BlockSpec((tr, N_HEADS * V_DIM), row),
        ],
        out_shape=[
            jax.ShapeDtypeStruct((n_rows, N_HEADS * HEAD_PAD), BF16),
            jax.ShapeDtypeStruct((n_rows, N_HEADS * V_DIM), BF16),
        ],
        compiler_params=_params("parallel"),
        name="kv_expand",
    )(ckv, kpe, wk, wv)


def _attn_kernel(q_ref, k_ref, v_ref, o_ref, *, heads):
    scale = (QK_NOPE + QK_ROPE) ** -0.5
    for hd in range(heads):
        q = q_ref[:, hd * HEAD_PAD:(hd + 1) * HEAD_PAD]
        k = k_ref[:, hd * HEAD_PAD:(hd + 1) * HEAD_PAD]
        s = lax.dot_general(q, k, (((1,), (1,)), ((), ())), preferred_element_type=F32) * scale
        p = jnp.exp(s - jnp.max(s, axis=-1, keepdims=True))
        denom = jnp.sum(p, axis=-1, keepdims=True)
        o = _dot(p.astype(BF16), v_ref[:, hd * V_DIM:(hd + 1) * V_DIM])
        o_ref[:, hd * V_DIM:(hd + 1) * V_DIM] = (o / denom).astype(BF16)


def _attention(q, k, v, n_batch, lq, lk, heads, tq=256):
    nq = lq // tq
    return pl.pallas_call(
        functools.partial(_attn_kernel, heads=heads),
        grid=(n_batch, N_HEADS // heads, nq),
        in_specs=[
            pl.BlockSpec((tq, heads * HEAD_PAD), lambda b, g, i: (b * nq + i, g)),
            pl.BlockSpec((lk, heads * HEAD_PAD), lambda b, g, i: (b, g)),
            pl.BlockSpec((lk, heads * V_DIM), lambda b, g, i: (b, g)),
        ],
        out_specs=pl.BlockSpec((tq, heads * V_DIM), lambda b, g, i: (b * nq + i, g)),
        out_shape=jax.ShapeDtypeStruct((n_batch * lq, N_HEADS * V_DIM), BF16),
        compiler_params=_params("parallel", "parallel", "arbitrary"),
        name="attention",
    )(q, k, v)


def _mlp_kernel(h_ref, x_ref, gate_ref, w1_ref, w2_ref, gf_ref, o_ref, *, tm, final):
    f = pl.program_id(1)

    @pl.when(f == 0)
    def _():
        o_ref[...] = jnp.zeros_like(o_ref)

    hid = _dot(h_ref[...], w1_ref[...])
    hid = jnp.square(jnp.maximum(hid, 0.0)).astype(BF16)
    o_ref[...] += _dot(hid, w2_ref[...])

    @pl.when(f == pl.num_programs(1) - 1)
    def _():
        gate = gate_ref[0]
        gf = gf_ref[...]
        rows = 16

        def body(c, carry):
            r = pl.multiple_of(c * rows, rows)
            y = x_ref[pl.ds(r, rows), :] + gate * o_ref[pl.ds(r, rows), :]
            if final:
                y = y * lax.rsqrt(jnp.mean(y * y, axis=-1, keepdims=True) + EPS) * gf
            o_ref[pl.ds(r, rows), :] = y
            return carry

        lax.fori_loop(0, tm // rows, body, 0, unroll=2)


def _mlp(st, h, x, mod, w1, w2, final_g, layer, final, tm=512, tf=1024):
    row = lambda i, f: (i, 0)
    return pl.pallas_call(
        functools.partial(_mlp_kernel, tm=tm, final=final),
        grid=(st.rows // tm, D_FF // tf),
        in_specs=[
            pl.BlockSpec((tm, D_MODEL), row),
            pl.BlockSpec((tm, D_MODEL), row),
            _mod_spec(st, layer, 5, tm),
            pl.BlockSpec((None, D_MODEL, tf), lambda i, f: (layer, 0, f)),
            pl.BlockSpec((None, tf, D_MODEL), lambda i, f: (layer, f, 0)),
            _resident((1, D_MODEL)),
        ],
        out_specs=pl.BlockSpec((tm, D_MODEL), row),
        out_shape=jax.ShapeDtypeStruct((st.rows, D_MODEL), F32),
        compiler_params=_params("parallel", "arbitrary"),
        name="mlp",
    )(h, x, mod, w1, w2, final_g)


def _rope_tables():
    n_rows = DEC_SEQ // GRID_W
    rows = jnp.broadcast_to(jnp.arange(n_rows)[:, None], (n_rows, GRID_W)).reshape(-1)
    cols = jnp.broadcast_to(jnp.arange(GRID_W)[None, :], (n_rows, GRID_W)).reshape(-1)
    nf = QK_ROPE // 4
    inv = 1.0 / (ROPE_THETA ** (jnp.arange(nf, dtype=F32) / nf))
    ang_r = rows.astype(F32)[:, None] * inv
    ang_c = cols.astype(F32)[:, None] * inv
    cr, sr, cc, sc = jnp.cos(ang_r), jnp.sin(ang_r), jnp.cos(ang_c), jnp.sin(ang_c)
    zeros = jnp.zeros((DEC_SEQ, LANE - QK_ROPE), F32)
    cos = jnp.concatenate([cr, cr, cc, cc, zeros], axis=1)
    sin = jnp.concatenate([-sr, sr, -sc, sc, zeros], axis=1)
    return cos, sin


def _swap_rope_partners(w):
    q = QK_ROPE // 4
    return jnp.concatenate([w[..., q:2 * q], w[..., :q], w[..., 3 * q:], w[..., 2 * q:3 * q]], axis=-1)


def _mla_weights(w_q_b, w_kv_a, w_kv_b):
    wq = w_q_b.reshape(Q_RANK, N_HEADS, QK_NOPE + QK_ROPE)
    pad = jnp.zeros((Q_RANK, N_HEADS, HEAD_PAD - QK_NOPE - QK_ROPE), w_q_b.dtype)
    wq_full = jnp.concatenate([wq, pad], axis=-1).reshape(Q_RANK, N_HEADS * HEAD_PAD)
    wq_swap = jnp.concatenate([_swap_rope_partners(wq[..., QK_NOPE:]), pad], axis=-1).reshape(Q_RANK, N_HEADS * LANE)
    kpe_w = w_kv_a[:, KV_RANK:]
    zpad = jnp.zeros((D_MODEL, LANE - QK_ROPE), w_kv_a.dtype)
    wkv = jnp.concatenate([w_kv_a[:, :KV_RANK], kpe_w, zpad, _swap_rope_partners(kpe_w), zpad], axis=1)
    wkvb = w_kv_b.reshape(KV_RANK, N_HEADS, QK_NOPE + V_DIM)
    wk = wkvb[..., :QK_NOPE].reshape(KV_RANK, N_HEADS * QK_NOPE)
    wv = wkvb[..., QK_NOPE:].reshape(KV_RANK, N_HEADS * V_DIM)
    return wq_full.astype(BF16), wq_swap.astype(BF16), wkv.astype(BF16), wk.astype(BF16), wv.astype(BF16)


def kernel(x_prompt, x_sample, cache_ckv, cache_kpe, c, c_ctx, ada_w, ada_b, norm1, norm2, conv_w_in, conv_w, conv_w_out, gmlp_w_in, gmlp_g_v, gmlp_w_s, gmlp_b_s, gmlp_w_out, mla_w_q_a, mla_g_q, mla_w_q_b, mla_w_kv_a, mla_g_kv, mla_w_kv_b, mla_w_o, mlp_w1, mlp_w2, final_norm):
    D = D_MODEL
    streams = (PROMPT, SAMPLE)
    xs = [x_prompt.reshape(PROMPT.rows, D), x_sample.reshape(SAMPLE.rows, D)]
    cond = jnp.concatenate([c_ctx[None, :], c, jnp.zeros((N_GROUPS - 1 - DEC_BATCH, D), F32)], axis=0)
    mod = _modvec(cond, ada_w, ada_b)

    n1, n2 = norm1[:, None, :], norm2[:, None, :]
    w1_all, w2_all = mlp_w1.astype(BF16), mlp_w2.astype(BF16)
    conv_in_all, conv_out_all = conv_w_in.astype(BF16), conv_w_out.astype(BF16)
    gmlp_in_all, gmlp_out_all, gmlp_ws_all = gmlp_w_in.astype(BF16), gmlp_w_out.astype(BF16), gmlp_w_s.astype(BF16)
    wqa_all, wo_all = mla_w_q_a.astype(BF16), mla_w_o.astype(BF16)
    final_g = final_norm[None, :]

    cos_t, sin_t = _rope_tables()
    new_ckv, new_kpe = [], []
    for i in range(DEPTH):
        kind, j = i % N_MIXERS, i // N_MIXERS
        hs = [None, None]
        if kind == 0:
            for n, st in enumerate(streams):
                bg, p = _conv_in(st, xs[n], mod, n1, conv_in_all, i, j)
                xs[n], hs[n] = _conv_out(st, bg, p, conv_w, xs[n], mod, conv_out_all, n2, i, j)
        elif kind == 1:
            b_full = jnp.repeat(gmlp_b_s[j].T, LANE, axis=1)
            for n, st in enumerate(streams):
                z = _gmlp_in(st, xs[n], mod, n1, gmlp_in_all, i, j)
                xs[n], hs[n] = _gmlp_out(st, z, gmlp_g_v[:, None, :], gmlp_ws_all, b_full, xs[n], mod, gmlp_out_all,
                                         n2, i, j)
        else:
            wq, wqs, wkv, wk, wv = _mla_weights(mla_w_q_b[j], mla_w_kv_a[j], mla_w_kv_b[j])
            proj = [_mla_proj(st, xs[n], mod, n1, wqa_all, mla_g_q[:, None, :], wq, wqs, wkv, mla_g_kv[:, None, :],
                              cos_t, sin_t, i, j) for n, st in enumerate(streams)]
            q, ckv, kpe, kpe_rot = proj[0]
            new_ckv.append(ckv.reshape(BATCH, SEQ, KV_RANK))
            new_kpe.append(kpe[:, :QK_ROPE].reshape(BATCH, SEQ, QK_ROPE))
            k_p, v_p = _kv_expand(ckv, kpe_rot, wk, wv)
            o_p = _attention(q, k_p, v_p, BATCH, SEQ, SEQ, heads=N_HEADS)
            q, ckv, kpe, kpe_rot = proj[1]
            lk = PAST_LEN + DEC_SEQ
            ckv_s = jnp.concatenate([cache_ckv[:, j], ckv.reshape(DEC_BATCH, DEC_SEQ, KV_RANK)], axis=1)
            cache_pe = jnp.pad(cache_kpe[:, j], ((0, 0), (0, 0), (0, LANE - QK_ROPE))).astype(BF16)
            kpe_s = jnp.concatenate([cache_pe, kpe_rot.reshape(DEC_BATCH, DEC_SEQ, LANE)], axis=1)
            k_s, v_s = _kv_expand(ckv_s.reshape(DEC_BATCH * lk, KV_RANK), kpe_s.reshape(DEC_BATCH * lk, LANE), wk, wv)
            o_s = _attention(q, k_s, v_s, DEC_BATCH, DEC_SEQ, lk, heads=4)
            for n, (st, o) in enumerate(zip(streams, (o_p, o_s))):
                xs[n], hs[n] = _attn_out(st, o, xs[n], mod, wo_all, n2, i, j)
        xs = [_mlp(st, hs[n], xs[n], mod, w1_all, w2_all, final_g, i, final=(i == DEPTH - 1))
              for n, st in enumerate(streams)]
    y_prompt = xs[0].reshape(BATCH, SEQ, D)
    y_sample = xs[1].reshape(DEC_BATCH, DEC_SEQ, D)
    return (y_prompt, y_sample, jnp.stack(new_ckv, axis=1), jnp.stack(new_kpe, axis=1))
```

```python
import functools
from typing import NamedTuple

import jax
import jax.numpy as jnp
from jax import lax
from jax.experimental import pallas as pl
from jax.experimental.pallas import tpu as pltpu

D_MODEL = 2048
BATCH = 32
SEQ = 256
DEPTH = 4
DEC_BATCH = 4
DEC_SEQ = 2048
PAST_LEN = 512
GRID_W = 64
N_MIXERS = 3
CONV_WIDTH = 3
CHUNK = 128
GMLP_GROUPS = 16
N_HEADS = 16
QK_NOPE = 128
QK_ROPE = 64
V_DIM = 128
Q_RANK = 512
KV_RANK = 512
ROPE_THETA = 10000.0
D_FF = 4 * D_MODEL
N_MOD = 6
EPS = 1e-6

N_GROUPS = 8
HEAD_PAD = 256
LANE = 128
BF16_ROWS = 16
SUB = 128
VMEM_LIMIT = 56 * 1024 * 1024
ATTN_PROMPT = dict(heads=16, tq=256)
ATTN_SAMPLE = dict(heads=4, tq=512)

BF16 = jnp.bfloat16
F32 = jnp.float32


class Stream(NamedTuple):
    rows: int
    seq: int
    group0: int
    group_rows: int
    rope: bool


PROMPT = Stream(BATCH * SEQ, SEQ, 0, BATCH * SEQ, False)
SAMPLE = Stream(DEC_BATCH * DEC_SEQ, DEC_SEQ, 1, DEC_SEQ, True)


def _params(*sem):
    return pltpu.CompilerParams(dimension_semantics=sem, vmem_limit_bytes=VMEM_LIMIT)


def _dot(a, b):
    return jnp.dot(a, b, preferred_element_type=F32)


def _resident(shape, layer=None):
    if layer is None:
        zeros = (0,) * len(shape)
        return pl.BlockSpec(shape, lambda *_: zeros, pipeline_mode=pl.Buffered(1))
    index = (layer,) + (0,) * len(shape)
    return pl.BlockSpec((None,) + tuple(shape), lambda *_: index, pipeline_mode=pl.Buffered(1))


def _mod_spec(st, layer, chunk, tm):
    def index(i, *_):
        return ((layer * N_MOD + chunk) * N_GROUPS + st.group0 + (i * tm) // st.group_rows, 0, 0)

    return pl.BlockSpec((1, 1, D_MODEL), index)


def _gain_shift(g_ref, sh_ref, sc_ref):
    return g_ref[...] * (1.0 + sc_ref[0]), sh_ref[0]


def _norm_mod(x, gsc, sh):
    ms = jnp.mean(x * x, axis=-1, keepdims=True)
    return (x * lax.rsqrt(ms + EPS) * gsc + sh).astype(BF16)


def _modvec_kernel(c_ref, w_ref, b_ref, o_ref):
    s = jax.nn.silu(c_ref[...]).astype(BF16)
    o_ref[0, 0] = _dot(s, w_ref[0].astype(BF16)) + b_ref[0]


def _modvec(cond, ada_w, ada_b, tn=1024):
    per_chunk = D_MODEL // tn
    out = pl.pallas_call(
        _modvec_kernel,
        grid=(DEPTH, N_MOD * per_chunk),
        in_specs=[
            pl.BlockSpec((N_GROUPS, D_MODEL), lambda l, j: (0, 0)),
            pl.BlockSpec((1, D_MODEL, tn), lambda l, j: (l, 0, j)),
            pl.BlockSpec((1, 1, tn), lambda l, j: (l, 0, j)),
        ],
        out_specs=pl.BlockSpec((1, 1, N_GROUPS, tn), lambda l, j: (l, j // per_chunk, 0, j % per_chunk)),
        out_shape=jax.ShapeDtypeStruct((DEPTH, N_MOD, N_GROUPS, D_MODEL), F32),
        compiler_params=_params("arbitrary", "arbitrary"),
        name="modvec",
    )(cond, ada_w, ada_b.reshape(DEPTH, 1, N_MOD * D_MODEL))
    return out.reshape(DEPTH * N_MOD * N_GROUPS, 1, D_MODEL)


def _conv_in_kernel(x_ref, g_ref, sh_ref, sc_ref, w_ref, bg_ref, p_ref, *, tm):
    gsc, sh = _gain_shift(g_ref, sh_ref, sc_ref)
    for s in range(tm // SUB):
        rs = slice(s * SUB, (s + 1) * SUB)
        z = _dot(_norm_mod(x_ref[rs, :], gsc, sh), w_ref[...])
        bg_ref[rs, :] = z[:, :D_MODEL].astype(BF16)
        p_ref[rs, :] = (z[:, D_MODEL:2 * D_MODEL] * z[:, 2 * D_MODEL:]).astype(BF16)


def _conv_in(st, x, mod, norm1, w_in, layer, j, tm=512):
    row = lambda i: (i, 0)
    return pl.pallas_call(
        functools.partial(_conv_in_kernel, tm=tm),
        grid=(st.rows // tm,),
        in_specs=[
            pl.BlockSpec((tm, D_MODEL), row),
            _resident((1, D_MODEL), layer),
            _mod_spec(st, layer, 0, tm),
            _mod_spec(st, layer, 1, tm),
            _resident((D_MODEL, 3 * D_MODEL), j),
        ],
        out_specs=[pl.BlockSpec((tm, D_MODEL), row)] * 2,
        out_shape=[jax.ShapeDtypeStruct((st.rows, D_MODEL), BF16)] * 2,
        compiler_params=_params("parallel"),
        name="conv_in",
    )(x, norm1, mod, mod, w_in)


def _gmlp_in_kernel(x_ref, g_ref, sh_ref, sc_ref, w_ref, z_ref, *, tm):
    gsc, sh = _gain_shift(g_ref, sh_ref, sc_ref)
    for s in range(tm // SUB):
        rs = slice(s * SUB, (s + 1) * SUB)
        z_ref[rs, :] = _dot(_norm_mod(x_ref[rs, :], gsc, sh), w_ref[...]).astype(BF16)


def _gmlp_in(st, x, mod, norm1, w_in, layer, j, tm=512):
    n_out = w_in.shape[-1]
    row = lambda i: (i, 0)
    return pl.pallas_call(
        functools.partial(_gmlp_in_kernel, tm=tm),
        grid=(st.rows // tm,),
        in_specs=[
            pl.BlockSpec((tm, D_MODEL), row),
            _resident((1, D_MODEL), layer),
            _mod_spec(st, layer, 0, tm),
            _mod_spec(st, layer, 1, tm),
            _resident((D_MODEL, n_out), j),
        ],
        out_specs=pl.BlockSpec((tm, n_out), row),
        out_shape=jax.ShapeDtypeStruct((st.rows, n_out), BF16),
        compiler_params=_params("parallel"),
        name="gmlp_in",
    )(x, norm1, mod, mod, w_in)


def _residual_proj(a, rs, x_ref, gate_ref, w_ref, gsc2, sh2, o_ref, h_ref):
    x1 = x_ref[rs, :] + gate_ref[0] * _dot(a, w_ref[...])
    o_ref[rs, :] = x1
    h_ref[rs, :] = _norm_mod(x1, gsc2, sh2)


def _proj_specs(st, layer, j, tm):
    row = lambda i: (i, 0)
    in_specs = [
        pl.BlockSpec((tm, D_MODEL), row),
        _mod_spec(st, layer, 2, tm),
        _resident((D_MODEL, D_MODEL), j),
        _resident((1, D_MODEL), layer),
        _mod_spec(st, layer, 3, tm),
        _mod_spec(st, layer, 4, tm),
    ]
    out_specs = [pl.BlockSpec((tm, D_MODEL), row)] * 2
    out_shape = [jax.ShapeDtypeStruct((st.rows, D_MODEL), F32), jax.ShapeDtypeStruct((st.rows, D_MODEL), BF16)]
    return in_specs, out_specs, out_shape


def _conv_out_kernel(bg_ref, p_ref, lo_ref, hi_ref, cw_ref, x_ref, gate_ref, w_ref, g2_ref, sh2_ref, sc2_ref,
                     o_ref, h_ref, pbuf_ref, *, tm, seq):
    halo = BF16_ROWS
    pbuf_ref[0:halo, :] = lo_ref[...].astype(F32)
    pbuf_ref[halo + tm:2 * halo + tm, :] = hi_ref[...].astype(F32)

    def copy(c, carry):
        r = pl.multiple_of(c * halo, halo)
        pbuf_ref[pl.ds(halo + r, halo), :] = p_ref[pl.ds(r, halo), :].astype(F32)
        return carry

    lax.fori_loop(0, tm // halo, copy, 0, unroll=4)

    gsc2, sh2 = _gain_shift(g2_ref, sh2_ref, sc2_ref)
    base = pl.program_id(0) * tm
    width = 4 * LANE
    for s in range(tm // SUB):
        r0 = s * SUB
        pos = (base + r0 + lax.broadcasted_iota(jnp.int32, (SUB, 1), 0)) % seq
        parts = []
        for lo in range(0, D_MODEL, width):
            cols = slice(lo, lo + width)
            prev = jnp.where(pos == 0, 0.0, pbuf_ref[halo - 1 + r0:halo - 1 + r0 + SUB, cols])
            nxt = jnp.where(pos == seq - 1, 0.0, pbuf_ref[halo + 1 + r0:halo + 1 + r0 + SUB, cols])
            cur = pbuf_ref[halo + r0:halo + r0 + SUB, cols]
            conv = prev * cw_ref[0:1, cols] + cur * cw_ref[1:2, cols] + nxt * cw_ref[2:3, cols]
            parts.append((bg_ref[r0:r0 + SUB, cols].astype(F32) * conv).astype(BF16))
        _residual_proj(jnp.concatenate(parts, axis=1), slice(r0, r0 + SUB), x_ref, gate_ref, w_ref, gsc2, sh2,
                       o_ref, h_ref)


def _conv_out(st, bg, p, conv_w, x, mod, w_out, norm2, layer, j, tm=512):
    nb = tm // BF16_ROWS
    last_halo = st.rows // BF16_ROWS - 1
    row = lambda i: (i, 0)
    tail, out_specs, out_shape = _proj_specs(st, layer, j, tm)
    return pl.pallas_call(
        functools.partial(_conv_out_kernel, tm=tm, seq=st.seq),
        grid=(st.rows // tm,),
        in_specs=[
            pl.BlockSpec((tm, D_MODEL), row),
            pl.BlockSpec((tm, D_MODEL), row),
            pl.BlockSpec((BF16_ROWS, D_MODEL), lambda i: (jnp.maximum(i * nb - 1, 0), 0)),
            pl.BlockSpec((BF16_ROWS, D_MODEL), lambda i: (jnp.minimum((i + 1) * nb, last_halo), 0)),
            _resident((CONV_WIDTH, D_MODEL), j),
        ] + tail,
        out_specs=out_specs,
        out_shape=out_shape,
        scratch_shapes=[pltpu.VMEM((tm + 2 * BF16_ROWS, D_MODEL), F32)],
        compiler_params=_params("parallel"),
        name="conv_out",
    )(bg, p, p, p, conv_w, x, mod, w_out, norm2, mod, mod)


def _gmlp_out_kernel(zu_ref, zv_ref, gv_ref, ws_ref, bs_ref, x_ref, gate_ref, w_ref, g2_ref, sh2_ref, sc2_ref,
                     o_ref, h_ref, a_ref, vn_ref, *, tm):
    gv = gv_ref[...]
    rows = BF16_ROWS

    def norm(c, carry):
        r = pl.multiple_of(c * rows, rows)
        v = zv_ref[pl.ds(r, rows), :].astype(F32)
        ms = jnp.mean(v * v, axis=-1, keepdims=True)
        vn_ref[pl.ds(r, rows), :] = (v * lax.rsqrt(ms + EPS) * gv).astype(BF16)
        return carry

    lax.fori_loop(0, tm // rows, norm, 0, unroll=2)

    n_chunks = tm // CHUNK
    for g in range(GMLP_GROUPS):
        cols = slice(g * LANE, (g + 1) * LANE)
        rhs = jnp.concatenate([vn_ref[c * CHUNK:(c + 1) * CHUNK, cols] for c in range(n_chunks)], axis=1)
        mixed = _dot(ws_ref[g], rhs)
        for c in range(n_chunks):
            rs = slice(c * CHUNK, (c + 1) * CHUNK)
            vv = mixed[:, c * CHUNK:(c + 1) * CHUNK] + bs_ref[:, cols]
            a_ref[rs, cols] = (zu_ref[rs, cols].astype(F32) * vv).astype(BF16)

    gsc2, sh2 = _gain_shift(g2_ref, sh2_ref, sc2_ref)
    for s in range(tm // SUB):
        rs = slice(s * SUB, (s + 1) * SUB)
        _residual_proj(a_ref[rs, :], rs, x_ref, gate_ref, w_ref, gsc2, sh2, o_ref, h_ref)


def _gmlp_out(st, z, g_v, w_s, b_full, x, mod, w_out, norm2, layer, j, tm=512):
    tail, out_specs, out_shape = _proj_specs(st, layer, j, tm)
    return pl.pallas_call(
        functools.partial(_gmlp_out_kernel, tm=tm),
        grid=(st.rows // tm,),
        in_specs=[
            pl.BlockSpec((tm, D_MODEL), lambda i: (i, 0)),
            pl.BlockSpec((tm, D_MODEL), lambda i: (i, 1)),
            _resident((1, D_MODEL), j),
            _resident((GMLP_GROUPS, CHUNK, CHUNK), j),
            _resident((CHUNK, D_MODEL)),
        ] + tail,
        out_specs=out_specs,
        out_shape=out_shape,
        scratch_shapes=[pltpu.VMEM((tm, D_MODEL), BF16), pltpu.VMEM((tm, D_MODEL), BF16)],
        compiler_params=_params("parallel"),
        name="gmlp_out",
    )(z, z, g_v, w_s, b_full, x, mod, w_out, norm2, mod, mod)


def _attn_out_kernel(a_ref, x_ref, gate_ref, w_ref, g2_ref, sh2_ref, sc2_ref, o_ref, h_ref, *, tm):
    gsc2, sh2 = _gain_shift(g2_ref, sh2_ref, sc2_ref)
    for s in range(tm // SUB):
        rs = slice(s * SUB, (s + 1) * SUB)
        _residual_proj(a_ref[rs, :], rs, x_ref, gate_ref, w_ref, gsc2, sh2, o_ref, h_ref)


def _attn_out(st, a, x, mod, w_o, norm2, layer, j, tm=512):
    tail, out_specs, out_shape = _proj_specs(st, layer, j, tm)
    return pl.pallas_call(
        functools.partial(_attn_out_kernel, tm=tm),
        grid=(st.rows // tm,),
        in_specs=[pl.BlockSpec((tm, D_MODEL), lambda i: (i, 0))] + tail,
        out_specs=out_specs,
        out_shape=out_shape,
        compiler_params=_params("parallel"),
        name="attn_out",
    )(a, x, mod, w_o, norm2, mod, mod)


def _mla_proj_kernel(x_ref, g_ref, sh_ref, sc_ref, wqa_ref, gq_ref, wq_ref, wqs_ref, wkv_ref, gkv_ref,
                     cos_ref, sin_ref, q_ref, ckv_ref, kpe_ref, kper_ref, *, tm, rope):
    gsc, sh = _gain_shift(g_ref, sh_ref, sc_ref)
    gq = gq_ref[...] * (QK_NOPE + QK_ROPE) ** -0.5
    for s in range(tm // SUB):
        rs = slice(s * SUB, (s + 1) * SUB)
        h = _norm_mod(x_ref[rs, :], gsc, sh)
        qa = _dot(h, wqa_ref[...])
        qa = qa * lax.rsqrt(jnp.mean(qa * qa, axis=-1, keepdims=True) + EPS) * gq
        qa = qa.astype(BF16)
        q = _dot(qa, wq_ref[...])
        kv = _dot(h, wkv_ref[...])
        c = kv[:, :KV_RANK]
        ckv_ref[rs, :] = c * lax.rsqrt(jnp.mean(c * c, axis=-1, keepdims=True) + EPS) * gkv_ref[...]
        kpe = kv[:, KV_RANK:KV_RANK + LANE]
        kpe_ref[rs, :] = kpe
        if rope:
            cos = cos_ref[rs, :]
            sin = sin_ref[rs, :]
            qs = _dot(qa, wqs_ref[...])
            for hd in range(N_HEADS):
                lo = hd * HEAD_PAD
                q_ref[rs, lo:lo + LANE] = q[:, lo:lo + LANE].astype(BF16)
                rot = q[:, lo + LANE:lo + HEAD_PAD] * cos + qs[:, hd * LANE:(hd + 1) * LANE] * sin
                q_ref[rs, lo + LANE:lo + HEAD_PAD] = rot.astype(BF16)
            kper_ref[rs, :] = (kpe * cos + kv[:, KV_RANK + LANE:] * sin).astype(BF16)
        else:
            q_ref[rs, :] = q.astype(BF16)
            kper_ref[rs, :] = kpe.astype(BF16)


def _mla_proj(st, x, mod, norm1, wqa, gq, wq, wqs, wkv, gkv, cos_t, sin_t, layer, j, tm=256):
    per_seq = st.seq // tm
    rope_idx = (lambda i: (i % per_seq, 0)) if st.rope else (lambda i: (0, 0))
    row = lambda i: (i, 0)
    return pl.pallas_call(
        functools.partial(_mla_proj_kernel, tm=tm, rope=st.rope),
        grid=(st.rows // tm,),
        in_specs=[
            pl.BlockSpec((tm, D_MODEL), row),
            _resident((1, D_MODEL), layer),
            _mod_spec(st, layer, 0, tm),
            _mod_spec(st, layer, 1, tm),
            _resident(wqa.shape[1:], j),
            _resident((1, Q_RANK), j),
            _resident(wq.shape),
            _resident(wqs.shape),
            _resident(wkv.shape),
            _resident((1, KV_RANK), j),
            pl.BlockSpec((tm, LANE), rope_idx),
            pl.BlockSpec((tm, LANE), rope_idx),
        ],
        out_specs=[
            pl.BlockSpec((tm, N_HEADS * HEAD_PAD), row),
            pl.BlockSpec((tm, KV_RANK), row),
            pl.BlockSpec((tm, LANE), row),
            pl.BlockSpec((tm, LANE), row),
        ],
        out_shape=[
            jax.ShapeDtypeStruct((st.rows, N_HEADS * HEAD_PAD), BF16),
            jax.ShapeDtypeStruct((st.rows, KV_RANK), F32),
            jax.ShapeDtypeStruct((st.rows, LANE), F32),
            jax.ShapeDtypeStruct((st.rows, LANE), BF16),
        ],
        compiler_params=_params("parallel"),
        name="mla_proj",
    )(x, norm1, mod, mod, wqa, gq, wq, wqs, wkv, gkv, cos_t, sin_t)


def _dot_nt(a, b):
    return lax.dot_general(a, b, (((1,), (1,)), ((), ())), preferred_element_type=F32)


def _kv_expand_kernel(c_ref, kpe_ref, wk_ref, wvt_ref, k_ref, vt_ref):
    c = c_ref[...].astype(BF16)
    kn = _dot(c, wk_ref[...])
    vt_ref[...] = _dot_nt(wvt_ref[...], c).astype(BF16)
    kpe = kpe_ref[...]
    for hd in range(N_HEADS):
        lo = hd * HEAD_PAD
        k_ref[:, lo:lo + LANE] = kn[:, hd * LANE:(hd + 1) * LANE].astype(BF16)
        k_ref[:, lo + LANE:lo + HEAD_PAD] = kpe


def _kv_expand(ckv, kpe, wk, wvt, tr=512):
    n_rows = ckv.shape[0]
    row = lambda i: (i, 0)
    return pl.pallas_call(
        _kv_expand_kernel,
        grid=(n_rows // tr,),
        in_specs=[
            pl.BlockSpec((tr, KV_RANK), row),
            pl.BlockSpec((tr, LANE), row),
            _resident(wk.shape),
            _resident(wvt.shape),
        ],
        out_specs=[
            pl.BlockSpec((tr, N_HEADS * HEAD_PAD), row),
            pl.BlockSpec((N_HEADS * V_DIM, tr), lambda i: (0, i)),
        ],
        out_shape=[
            jax.ShapeDtypeStruct((n_rows, N_HEADS * HEAD_PAD), BF16),
            jax.ShapeDtypeStruct((N_HEADS * V_DIM, n_rows), BF16),
        ],
        compiler_params=_params("parallel"),
        name="kv_expand",
    )(ckv, kpe, wk, wvt)


def _attn_kernel(q_ref, k_ref, vt_ref, o_ref, *, heads):
    def scores(hd):
        cols = slice(hd * HEAD_PAD, (hd + 1) * HEAD_PAD)
        return _dot_nt(k_ref[:, cols], q_ref[:, cols])

    nxt = scores(0)
    for hd in range(heads):
        st = nxt
        if hd + 1 < heads:
            nxt = scores(hd + 1)
        p = jnp.exp(st - jnp.max(st, axis=0, keepdims=True))
        denom = jnp.sum(p, axis=0, keepdims=True)
        ot = _dot(vt_ref[hd * V_DIM:(hd + 1) * V_DIM, :], p.astype(BF16))
        o_ref[:, hd * V_DIM:(hd + 1) * V_DIM] = (ot / denom).T.astype(BF16)


def _attention(q, k, vt, n_batch, lq, lk, heads, tq=256):
    nq = lq // tq
    return pl.pallas_call(
        functools.partial(_attn_kernel, heads=heads),
        grid=(n_batch, N_HEADS // heads, nq),
        in_specs=[
            pl.BlockSpec((tq, heads * HEAD_PAD), lambda b, g, i: (b * nq + i, g)),
            pl.BlockSpec((lk, heads * HEAD_PAD), lambda b, g, i: (b, g)),
            pl.BlockSpec((heads * V_DIM, lk), lambda b, g, i: (g, b)),
        ],
        out_specs=pl.BlockSpec((tq, heads * V_DIM), lambda b, g, i: (b * nq + i, g)),
        out_shape=jax.ShapeDtypeStruct((n_batch * lq, N_HEADS * V_DIM), BF16),
        compiler_params=_params("parallel", "parallel", "arbitrary"),
        name="attention",
    )(q, k, vt)


def _mlp_kernel(h_ref, x_ref, gate_ref, w1_ref, w2_ref, gf_ref, o_ref, *, tm, final):
    f = pl.program_id(1)

    @pl.when(f == 0)
    def _():
        o_ref[...] = jnp.zeros_like(o_ref)

    hid = _dot(h_ref[...], w1_ref[...])
    hid = jnp.square(jnp.maximum(hid, 0.0)).astype(BF16)
    o_ref[...] += _dot(hid, w2_ref[...])

    @pl.when(f == pl.num_programs(1) - 1)
    def _():
        gate = gate_ref[0]
        gf = gf_ref[...]
        rows = 16

        def body(c, carry):
            r = pl.multiple_of(c * rows, rows)
            y = x_ref[pl.ds(r, rows), :] + gate * o_ref[pl.ds(r, rows), :]
            if final:
                y = y * lax.rsqrt(jnp.mean(y * y, axis=-1, keepdims=True) + EPS) * gf
            o_ref[pl.ds(r, rows), :] = y
            return carry

        lax.fori_loop(0, tm // rows, body, 0, unroll=2)


def _mlp(st, h, x, mod, w1, w2, final_g, layer, final, tm=512, tf=1024):
    row = lambda i, f: (i, 0)
    return pl.pallas_call(
        functools.partial(_mlp_kernel, tm=tm, final=final),
        grid=(st.rows // tm, D_FF // tf),
        in_specs=[
            pl.BlockSpec((tm, D_MODEL), row),
            pl.BlockSpec((tm, D_MODEL), row),
            _mod_spec(st, layer, 5, tm),
            pl.BlockSpec((None, D_MODEL, tf), lambda i, f: (layer, 0, f)),
            pl.BlockSpec((None, tf, D_MODEL), lambda i, f: (layer, f, 0)),
            _resident((1, D_MODEL)),
        ],
        out_specs=pl.BlockSpec((tm, D_MODEL), row),
        out_shape=jax.ShapeDtypeStruct((st.rows, D_MODEL), F32),
        compiler_params=_params("parallel", "arbitrary"),
        name="mlp",
    )(h, x, mod, w1, w2, final_g)


def _rope_tables():
    n_rows = DEC_SEQ // GRID_W
    rows = jnp.broadcast_to(jnp.arange(n_rows)[:, None], (n_rows, GRID_W)).reshape(-1)
    cols = jnp.broadcast_to(jnp.arange(GRID_W)[None, :], (n_rows, GRID_W)).reshape(-1)
    nf = QK_ROPE // 4
    inv = 1.0 / (ROPE_THETA ** (jnp.arange(nf, dtype=F32) / nf))
    ang_r = rows.astype(F32)[:, None] * inv
    ang_c = cols.astype(F32)[:, None] * inv
    cr, sr, cc, sc = jnp.cos(ang_r), jnp.sin(ang_r), jnp.cos(ang_c), jnp.sin(ang_c)
    zeros = jnp.zeros((DEC_SEQ, LANE - QK_ROPE), F32)
    cos = jnp.concatenate([cr, cr, cc, cc, zeros], axis=1)
    sin = jnp.concatenate([-sr, sr, -sc, sc, zeros], axis=1)
    return cos, sin


def _swap_rope_partners(w):
    q = QK_ROPE // 4
    return jnp.concatenate([w[..., q:2 * q], w[..., :q], w[..., 3 * q:], w[..., 2 * q:3 * q]], axis=-1)


def _mla_weights(w_q_b, w_kv_a, w_kv_b):
    wq = w_q_b.reshape(Q_RANK, N_HEADS, QK_NOPE + QK_ROPE)
    pad = jnp.zeros((Q_RANK, N_HEADS, HEAD_PAD - QK_NOPE - QK_ROPE), w_q_b.dtype)
    wq_full = jnp.concatenate([wq, pad], axis=-1).reshape(Q_RANK, N_HEADS * HEAD_PAD)
    wq_swap = jnp.concatenate([_swap_rope_partners(wq[..., QK_NOPE:]), pad], axis=-1).reshape(Q_RANK, N_HEADS * LANE)
    kpe_w = w_kv_a[:, KV_RANK:]
    zpad = jnp.zeros((D_MODEL, LANE - QK_ROPE), w_kv_a.dtype)
    wkv = jnp.concatenate([w_kv_a[:, :KV_RANK], kpe_w, zpad, _swap_rope_partners(kpe_w), zpad], axis=1)
    wkvb = w_kv_b.reshape(KV_RANK, N_HEADS, QK_NOPE + V_DIM)
    wk = wkvb[..., :QK_NOPE].reshape(KV_RANK, N_HEADS * QK_NOPE)
    wvt = wkvb[..., QK_NOPE:].reshape(KV_RANK, N_HEADS * V_DIM).T
    return wq_full.astype(BF16), wq_swap.astype(BF16), wkv.astype(BF16), wk.astype(BF16), wvt.astype(BF16)


def kernel(x_prompt, x_sample, cache_ckv, cache_kpe, c, c_ctx, ada_w, ada_b, norm1, norm2, conv_w_in, conv_w, conv_w_out, gmlp_w_in, gmlp_g_v, gmlp_w_s, gmlp_b_s, gmlp_w_out, mla_w_q_a, mla_g_q, mla_w_q_b, mla_w_kv_a, mla_g_kv, mla_w_kv_b, mla_w_o, mlp_w1, mlp_w2, final_norm):
    D = D_MODEL
    streams = (PROMPT, SAMPLE)
    xs = [x_prompt.reshape(PROMPT.rows, D), x_sample.reshape(SAMPLE.rows, D)]
    cond = jnp.concatenate([c_ctx[None, :], c, jnp.zeros((N_GROUPS - 1 - DEC_BATCH, D), F32)], axis=0)
    mod = _modvec(cond, ada_w, ada_b)

    n1, n2 = norm1[:, None, :], norm2[:, None, :]
    w1_all, w2_all = mlp_w1.astype(BF16), mlp_w2.astype(BF16)
    conv_in_all, conv_out_all = conv_w_in.astype(BF16), conv_w_out.astype(BF16)
    gmlp_in_all, gmlp_out_all, gmlp_ws_all = gmlp_w_in.astype(BF16), gmlp_w_out.astype(BF16), gmlp_w_s.astype(BF16)
    wqa_all, wo_all = mla_w_q_a.astype(BF16), mla_w_o.astype(BF16)
    final_g = final_norm[None, :]

    cos_t, sin_t = _rope_tables()
    new_ckv, new_kpe = [], []
    for i in range(DEPTH):
        kind, j = i % N_MIXERS, i // N_MIXERS
        hs = [None, None]
        if kind == 0:
            for n, st in enumerate(streams):
                bg, p = _conv_in(st, xs[n], mod, n1, conv_in_all, i, j)
                xs[n], hs[n] = _conv_out(st, bg, p, conv_w, xs[n], mod, conv_out_all, n2, i, j)
        elif kind == 1:
            b_full = jnp.repeat(gmlp_b_s[j].T, LANE, axis=1)
            for n, st in enumerate(streams):
                z = _gmlp_in(st, xs[n], mod, n1, gmlp_in_all, i, j)
                xs[n], hs[n] = _gmlp_out(st, z, gmlp_g_v[:, None, :], gmlp_ws_all, b_full, xs[n], mod, gmlp_out_all,
                                         n2, i, j)
        else:
            wq, wqs, wkv, wk, wvt = _mla_weights(mla_w_q_b[j], mla_w_kv_a[j], mla_w_kv_b[j])
            proj = [_mla_proj(st, xs[n], mod, n1, wqa_all, mla_g_q[:, None, :], wq, wqs, wkv, mla_g_kv[:, None, :],
                              cos_t, sin_t, i, j) for n, st in enumerate(streams)]
            q, ckv, kpe, kpe_rot = proj[0]
            new_ckv.append(ckv.reshape(BATCH, SEQ, KV_RANK))
            new_kpe.append(kpe[:, :QK_ROPE].reshape(BATCH, SEQ, QK_ROPE))
            k_p, v_p = _kv_expand(ckv, kpe_rot, wk, wvt)
            o_p = _attention(q, k_p, v_p, BATCH, SEQ, SEQ, **ATTN_PROMPT)
            q, ckv, kpe, kpe_rot = proj[1]
            lk = PAST_LEN + DEC_SEQ
            ckv_s = jnp.concatenate([cache_ckv[:, j], ckv.reshape(DEC_BATCH, DEC_SEQ, KV_RANK)], axis=1)
            cache_pe = jnp.pad(cache_kpe[:, j], ((0, 0), (0, 0), (0, LANE - QK_ROPE))).astype(BF16)
            kpe_s = jnp.concatenate([cache_pe, kpe_rot.reshape(DEC_BATCH, DEC_SEQ, LANE)], axis=1)
            k_s, v_s = _kv_expand(ckv_s.reshape(DEC_BATCH * lk, KV_RANK), kpe_s.reshape(DEC_BATCH * lk, LANE), wk, wvt)
            o_s = _attention(q, k_s, v_s, DEC_BATCH, DEC_SEQ, lk, **ATTN_SAMPLE)
            for n, (st, o) in enumerate(zip(streams, (o_p, o_s))):
                xs[n], hs[n] = _attn_out(st, o, xs[n], mod, wo_all, n2, i, j)
        xs = [_mlp(st, hs[n], xs[n], mod, w1_all, w2_all, final_g, i, final=(i == DEPTH - 1))
              for n, st in enumerate(streams)]
    y_prompt = xs[0].reshape(BATCH, SEQ, D)
    y_sample = xs[1].reshape(DEC_BATCH, DEC_SEQ, D)
    return (y_prompt, y_sample, jnp.stack(new_ckv, axis=1), jnp.stack(new_kpe, axis=1))
```

```python
import functools
from typing import NamedTuple

import jax
import jax.numpy as jnp
from jax import lax
from jax.experimental import pallas as pl
from jax.experimental.pallas import tpu as pltpu

D_MODEL = 2048
BATCH = 32
SEQ = 256
DEPTH = 4
DEC_BATCH = 4
DEC_SEQ = 2048
PAST_LEN = 512
GRID_W = 64
N_MIXERS = 3
CONV_WIDTH = 3
CHUNK = 128
GMLP_GROUPS = 16
N_HEADS = 16
QK_NOPE = 128
QK_ROPE = 64
V_DIM = 128
Q_RANK = 512
KV_RANK = 512
ROPE_THETA = 10000.0
D_FF = 4 * D_MODEL
N_MOD = 6
EPS = 1e-6

N_GROUPS = 8
HEAD_PAD = 256
LANE = 128
BF16_ROWS = 16
SUB = 256
VMEM_LIMIT = 56 * 1024 * 1024
ATTN_PROMPT = dict(heads=16, tq=256)
ATTN_SAMPLE = dict(heads=4, tq=512)

BF16 = jnp.bfloat16
F32 = jnp.float32


class Stream(NamedTuple):
    rows: int
    seq: int
    group0: int
    group_rows: int
    rope: bool


PROMPT = Stream(BATCH * SEQ, SEQ, 0, BATCH * SEQ, False)
SAMPLE = Stream(DEC_BATCH * DEC_SEQ, DEC_SEQ, 1, DEC_SEQ, True)


def _params(*sem):
    return pltpu.CompilerParams(dimension_semantics=sem, vmem_limit_bytes=VMEM_LIMIT)


def _dot(a, b):
    return jnp.dot(a, b, preferred_element_type=F32)


def _resident(shape, layer=None):
    if layer is None:
        zeros = (0,) * len(shape)
        return pl.BlockSpec(shape, lambda *_: zeros, pipeline_mode=pl.Buffered(1))
    index = (layer,) + (0,) * len(shape)
    return pl.BlockSpec((None,) + tuple(shape), lambda *_: index, pipeline_mode=pl.Buffered(1))


def _mod_spec(st, layer, chunk, tm):
    def index(i, *_):
        return ((layer * N_MOD + chunk) * N_GROUPS + st.group0 + (i * tm) // st.group_rows, 0, 0)

    return pl.BlockSpec((1, 1, D_MODEL), index)


def _gain_shift(g_ref, sh_ref, sc_ref):
    return g_ref[...] * (1.0 + sc_ref[0]), sh_ref[0]


def _norm_mod(x, gsc, sh):
    ms = jnp.mean(x * x, axis=-1, keepdims=True)
    return (x * lax.rsqrt(ms + EPS) * gsc + sh).astype(BF16)


def _modvec_kernel(c_ref, w_ref, b_ref, o_ref):
    s = jax.nn.silu(c_ref[...]).astype(BF16)
    o_ref[0, 0] = _dot(s, w_ref[0].astype(BF16)) + b_ref[0]


def _modvec(cond, ada_w, ada_b, tn=1024):
    per_chunk = D_MODEL // tn
    out = pl.pallas_call(
        _modvec_kernel,
        grid=(DEPTH, N_MOD * per_chunk),
        in_specs=[
            pl.BlockSpec((N_GROUPS, D_MODEL), lambda l, j: (0, 0)),
            pl.BlockSpec((1, D_MODEL, tn), lambda l, j: (l, 0, j)),
            pl.BlockSpec((1, 1, tn), lambda l, j: (l, 0, j)),
        ],
        out_specs=pl.BlockSpec((1, 1, N_GROUPS, tn), lambda l, j: (l, j // per_chunk, 0, j % per_chunk)),
        out_shape=jax.ShapeDtypeStruct((DEPTH, N_MOD, N_GROUPS, D_MODEL), F32),
        compiler_params=_params("arbitrary", "arbitrary"),
        name="modvec",
    )(cond, ada_w, ada_b.reshape(DEPTH, 1, N_MOD * D_MODEL))
    return out.reshape(DEPTH * N_MOD * N_GROUPS, 1, D_MODEL)


def _conv_in_kernel(x_ref, g_ref, sh_ref, sc_ref, w_ref, bg_ref, p_ref, *, tm):
    gsc, sh = _gain_shift(g_ref, sh_ref, sc_ref)
    for s in range(tm // SUB):
        rs = slice(s * SUB, (s + 1) * SUB)
        z = _dot(_norm_mod(x_ref[rs, :], gsc, sh), w_ref[...])
        bg_ref[rs, :] = z[:, :D_MODEL].astype(BF16)
        p_ref[rs, :] = (z[:, D_MODEL:2 * D_MODEL] * z[:, 2 * D_MODEL:]).astype(BF16)


def _conv_in(st, x, mod, norm1, w_in, layer, tm=512):
    row = lambda i: (i, 0)
    return pl.pallas_call(
        functools.partial(_conv_in_kernel, tm=tm),
        grid=(st.rows // tm,),
        in_specs=[
            pl.BlockSpec((tm, D_MODEL), row),
            _resident((1, D_MODEL), layer),
            _mod_spec(st, layer, 0, tm),
            _mod_spec(st, layer, 1, tm),
            _resident((D_MODEL, 3 * D_MODEL)),
        ],
        out_specs=[pl.BlockSpec((tm, D_MODEL), row)] * 2,
        out_shape=[jax.ShapeDtypeStruct((st.rows, D_MODEL), BF16)] * 2,
        compiler_params=_params("parallel"),
        name="conv_in",
    )(x, norm1, mod, mod, w_in)


def _gmlp_in_kernel(x_ref, g_ref, sh_ref, sc_ref, w_ref, z_ref, *, tm):
    gsc, sh = _gain_shift(g_ref, sh_ref, sc_ref)
    for s in range(tm // SUB):
        rs = slice(s * SUB, (s + 1) * SUB)
        z_ref[rs, :] = _dot(_norm_mod(x_ref[rs, :], gsc, sh), w_ref[...]).astype(BF16)


def _gmlp_in(st, x, mod, norm1, w_in, layer, tm=512):
    n_out = w_in.shape[-1]
    row = lambda i: (i, 0)
    return pl.pallas_call(
        functools.partial(_gmlp_in_kernel, tm=tm),
        grid=(st.rows // tm,),
        in_specs=[
            pl.BlockSpec((tm, D_MODEL), row),
            _resident((1, D_MODEL), layer),
            _mod_spec(st, layer, 0, tm),
            _mod_spec(st, layer, 1, tm),
            _resident((D_MODEL, n_out)),
        ],
        out_specs=pl.BlockSpec((tm, n_out), row),
        out_shape=jax.ShapeDtypeStruct((st.rows, n_out), BF16),
        compiler_params=_params("parallel"),
        name="gmlp_in",
    )(x, norm1, mod, mod, w_in)


def _residual_proj(a, rs, x_ref, gate_ref, w_ref, gsc2, sh2, o_ref, h_ref):
    x1 = x_ref[rs, :] + gate_ref[0] * _dot(a, w_ref[...])
    o_ref[rs, :] = x1
    h_ref[rs, :] = _norm_mod(x1, gsc2, sh2)


def _proj_specs(st, layer, tm):
    row = lambda i: (i, 0)
    in_specs = [
        pl.BlockSpec((tm, D_MODEL), row),
        _mod_spec(st, layer, 2, tm),
        _resident((D_MODEL, D_MODEL)),
        _resident((1, D_MODEL), layer),
        _mod_spec(st, layer, 3, tm),
        _mod_spec(st, layer, 4, tm),
    ]
    out_specs = [pl.BlockSpec((tm, D_MODEL), row)] * 2
    out_shape = [jax.ShapeDtypeStruct((st.rows, D_MODEL), F32), jax.ShapeDtypeStruct((st.rows, D_MODEL), BF16)]
    return in_specs, out_specs, out_shape


def _conv_out_kernel(bg_ref, p_ref, lo_ref, hi_ref, cw_ref, x_ref, gate_ref, w_ref, g2_ref, sh2_ref, sc2_ref,
                     o_ref, h_ref, pbuf_ref, *, tm, seq):
    halo = BF16_ROWS
    pbuf_ref[0:halo, :] = lo_ref[...].astype(F32)
    pbuf_ref[halo + tm:2 * halo + tm, :] = hi_ref[...].astype(F32)

    def copy(c, carry):
        r = pl.multiple_of(c * halo, halo)
        pbuf_ref[pl.ds(halo + r, halo), :] = p_ref[pl.ds(r, halo), :].astype(F32)
        return carry

    lax.fori_loop(0, tm // halo, copy, 0, unroll=4)

    gsc2, sh2 = _gain_shift(g2_ref, sh2_ref, sc2_ref)
    base = pl.program_id(0) * tm
    width = 4 * LANE
    for s in range(tm // SUB):
        r0 = s * SUB
        pos = (base + r0 + lax.broadcasted_iota(jnp.int32, (SUB, 1), 0)) % seq
        parts = []
        for lo in range(0, D_MODEL, width):
            cols = slice(lo, lo + width)
            prev = jnp.where(pos == 0, 0.0, pbuf_ref[halo - 1 + r0:halo - 1 + r0 + SUB, cols])
            nxt = jnp.where(pos == seq - 1, 0.0, pbuf_ref[halo + 1 + r0:halo + 1 + r0 + SUB, cols])
            cur = pbuf_ref[halo + r0:halo + r0 + SUB, cols]
            conv = prev * cw_ref[0:1, cols] + cur * cw_ref[1:2, cols] + nxt * cw_ref[2:3, cols]
            parts.append((bg_ref[r0:r0 + SUB, cols].astype(F32) * conv).astype(BF16))
        _residual_proj(jnp.concatenate(parts, axis=1), slice(r0, r0 + SUB), x_ref, gate_ref, w_ref, gsc2, sh2,
                       o_ref, h_ref)


def _conv_out(st, bg, p, conv_w, x, mod, w_out, norm2, layer, j, tm=512):
    nb = tm // BF16_ROWS
    last_halo = st.rows // BF16_ROWS - 1
    row = lambda i: (i, 0)
    tail, out_specs, out_shape = _proj_specs(st, layer, tm)
    return pl.pallas_call(
        functools.partial(_conv_out_kernel, tm=tm, seq=st.seq),
        grid=(st.rows // tm,),
        in_specs=[
            pl.BlockSpec((tm, D_MODEL), row),
            pl.BlockSpec((tm, D_MODEL), row),
            pl.BlockSpec((BF16_ROWS, D_MODEL), lambda i: (jnp.maximum(i * nb - 1, 0), 0)),
            pl.BlockSpec((BF16_ROWS, D_MODEL), lambda i: (jnp.minimum((i + 1) * nb, last_halo), 0)),
            _resident((CONV_WIDTH, D_MODEL), j),
        ] + tail,
        out_specs=out_specs,
        out_shape=out_shape,
        scratch_shapes=[pltpu.VMEM((tm + 2 * BF16_ROWS, D_MODEL), F32)],
        compiler_params=_params("parallel"),
        name="conv_out",
    )(bg, p, p, p, conv_w, x, mod, w_out, norm2, mod, mod)


def _gmlp_out_kernel(zu_ref, zv_ref, gv_ref, ws_ref, bs_ref, x_ref, gate_ref, w_ref, g2_ref, sh2_ref, sc2_ref,
                     o_ref, h_ref, a_ref, vn_ref, *, tm):
    gv = gv_ref[...]
    n_chunks = tm // CHUNK
    for c in range(n_chunks):
        rs = slice(c * CHUNK, (c + 1) * CHUNK)
        v = zv_ref[rs, :].astype(F32)
        ms = jnp.mean(v * v, axis=-1, keepdims=True)
        vn_ref[rs, :] = (v * lax.rsqrt(ms + EPS) * gv).astype(BF16)

    for g in range(GMLP_GROUPS):
        cols = slice(g * LANE, (g + 1) * LANE)
        rhs = jnp.concatenate([vn_ref[c * CHUNK:(c + 1) * CHUNK, cols] for c in range(n_chunks)], axis=1)
        mixed = _dot(ws_ref[g], rhs)
        for c in range(n_chunks):
            rs = slice(c * CHUNK, (c + 1) * CHUNK)
            vv = mixed[:, c * CHUNK:(c + 1) * CHUNK] + bs_ref[:, cols]
            a_ref[rs, cols] = (zu_ref[rs, cols].astype(F32) * vv).astype(BF16)

    gsc2, sh2 = _gain_shift(g2_ref, sh2_ref, sc2_ref)
    for s in range(tm // SUB):
        rs = slice(s * SUB, (s + 1) * SUB)
        _residual_proj(a_ref[rs, :], rs, x_ref, gate_ref, w_ref, gsc2, sh2, o_ref, h_ref)


def _gmlp_out(st, z, g_v, w_s, b_full, x, mod, w_out, norm2, layer, j, tm=512):
    tail, out_specs, out_shape = _proj_specs(st, layer, tm)
    return pl.pallas_call(
        functools.partial(_gmlp_out_kernel, tm=tm),
        grid=(st.rows // tm,),
        in_specs=[
            pl.BlockSpec((tm, D_MODEL), lambda i: (i, 0)),
            pl.BlockSpec((tm, D_MODEL), lambda i: (i, 1)),
            _resident((1, D_MODEL), j),
            _resident((GMLP_GROUPS, CHUNK, CHUNK)),
            _resident((CHUNK, D_MODEL)),
        ] + tail,
        out_specs=out_specs,
        out_shape=out_shape,
        scratch_shapes=[pltpu.VMEM((tm, D_MODEL), BF16), pltpu.VMEM((tm, D_MODEL), BF16)],
        compiler_params=_params("parallel"),
        name="gmlp_out",
    )(z, z, g_v, w_s, b_full, x, mod, w_out, norm2, mod, mod)


def _attn_out_kernel(a_ref, x_ref, gate_ref, w_ref, g2_ref, sh2_ref, sc2_ref, o_ref, h_ref, *, tm):
    gsc2, sh2 = _gain_shift(g2_ref, sh2_ref, sc2_ref)
    for s in range(tm // SUB):
        rs = slice(s * SUB, (s + 1) * SUB)
        _residual_proj(a_ref[rs, :], rs, x_ref, gate_ref, w_ref, gsc2, sh2, o_ref, h_ref)


def _attn_out(st, a, x, mod, w_o, norm2, layer, tm=512):
    tail, out_specs, out_shape = _proj_specs(st, layer, tm)
    return pl.pallas_call(
        functools.partial(_attn_out_kernel, tm=tm),
        grid=(st.rows // tm,),
        in_specs=[pl.BlockSpec((tm, D_MODEL), lambda i: (i, 0))] + tail,
        out_specs=out_specs,
        out_shape=out_shape,
        compiler_params=_params("parallel"),
        name="attn_out",
    )(a, x, mod, w_o, norm2, mod, mod)


def _mla_proj_kernel(x_ref, g_ref, sh_ref, sc_ref, wqa_ref, gq_ref, wq_ref, wqs_ref, wkv_ref, gkv_ref,
                     cos_ref, sin_ref, q_ref, ckv_ref, kpe_ref, kper_ref, *, tm, rope):
    gsc, sh = _gain_shift(g_ref, sh_ref, sc_ref)
    gq = gq_ref[...] * (QK_NOPE + QK_ROPE) ** -0.5
    for s in range(tm // SUB):
        rs = slice(s * SUB, (s + 1) * SUB)
        h = _norm_mod(x_ref[rs, :], gsc, sh)
        qa = _dot(h, wqa_ref[...])
        qa = qa * lax.rsqrt(jnp.mean(qa * qa, axis=-1, keepdims=True) + EPS) * gq
        qa = qa.astype(BF16)
        q = _dot(qa, wq_ref[...])
        kv = _dot(h, wkv_ref[...])
        c = kv[:, :KV_RANK]
        ckv_ref[rs, :] = c * lax.rsqrt(jnp.mean(c * c, axis=-1, keepdims=True) + EPS) * gkv_ref[...]
        kpe = kv[:, KV_RANK:KV_RANK + LANE]
        kpe_ref[rs, :] = kpe
        if rope:
            cos = cos_ref[rs, :]
            sin = sin_ref[rs, :]
            qs = _dot(qa, wqs_ref[...])
            for hd in range(N_HEADS):
                lo = hd * HEAD_PAD
                q_ref[rs, lo:lo + LANE] = q[:, lo:lo + LANE].astype(BF16)
                rot = q[:, lo + LANE:lo + HEAD_PAD] * cos + qs[:, hd * LANE:(hd + 1) * LANE] * sin
                q_ref[rs, lo + LANE:lo + HEAD_PAD] = rot.astype(BF16)
            kper_ref[rs, :] = (kpe * cos + kv[:, KV_RANK + LANE:] * sin).astype(BF16)
        else:
            q_ref[rs, :] = q.astype(BF16)
            kper_ref[rs, :] = kpe.astype(BF16)


def _mla_proj(st, x, mod, norm1, wqa, gq, wq, wqs, wkv, gkv, cos_t, sin_t, layer, j, tm=512):
    per_seq = st.seq // tm
    rope_idx = (lambda i: (i % per_seq, 0)) if st.rope else (lambda i: (0, 0))
    row = lambda i: (i, 0)
    return pl.pallas_call(
        functools.partial(_mla_proj_kernel, tm=tm, rope=st.rope),
        grid=(st.rows // tm,),
        in_specs=[
            pl.BlockSpec((tm, D_MODEL), row),
            _resident((1, D_MODEL), layer),
            _mod_spec(st, layer, 0, tm),
            _mod_spec(st, layer, 1, tm),
            _resident(wqa.shape),
            _resident((1, Q_RANK), j),
            _resident(wq.shape),
            _resident(wqs.shape),
            _resident(wkv.shape),
            _resident((1, KV_RANK), j),
            pl.BlockSpec((tm, LANE), rope_idx),
            pl.BlockSpec((tm, LANE), rope_idx),
        ],
        out_specs=[
            pl.BlockSpec((tm, N_HEADS * HEAD_PAD), row),
            pl.BlockSpec((tm, KV_RANK), row),
            pl.BlockSpec((tm, LANE), row),
            pl.BlockSpec((tm, LANE), row),
        ],
        out_shape=[
            jax.ShapeDtypeStruct((st.rows, N_HEADS * HEAD_PAD), BF16),
            jax.ShapeDtypeStruct((st.rows, KV_RANK), F32),
            jax.ShapeDtypeStruct((st.rows, LANE), F32),
            jax.ShapeDtypeStruct((st.rows, LANE), BF16),
        ],
        compiler_params=_params("parallel"),
        name="mla_proj",
    )(x, norm1, mod, mod, wqa, gq, wq, wqs, wkv, gkv, cos_t, sin_t)


def _dot_nt(a, b):
    return lax.dot_general(a, b, (((1,), (1,)), ((), ())), preferred_element_type=F32)


def _kv_expand_kernel(c_ref, kpe_ref, wk_ref, wvt_ref, k_ref, vt_ref):
    c = c_ref[...].astype(BF16)
    kn = _dot(c, wk_ref[...])
    vt_ref[...] = _dot_nt(wvt_ref[...], c).astype(BF16)
    kpe = kpe_ref[...]
    for hd in range(N_HEADS):
        lo = hd * HEAD_PAD
        k_ref[:, lo:lo + LANE] = kn[:, hd * LANE:(hd + 1) * LANE].astype(BF16)
        k_ref[:, lo + LANE:lo + HEAD_PAD] = kpe


def _kv_expand(ckv, kpe, wk, wvt, tr=512):
    n_rows = ckv.shape[0]
    row = lambda i: (i, 0)
    return pl.pallas_call(
        _kv_expand_kernel,
        grid=(n_rows // tr,),
        in_specs=[
            pl.BlockSpec((tr, KV_RANK), row),
            pl.BlockSpec((tr, LANE), row),
            _resident(wk.shape),
            _resident(wvt.shape),
        ],
        out_specs=[
            pl.BlockSpec((tr, N_HEADS * HEAD_PAD), row),
            pl.BlockSpec((N_HEADS * V_DIM, tr), lambda i: (0, i)),
        ],
        out_shape=[
            jax.ShapeDtypeStruct((n_rows, N_HEADS * HEAD_PAD), BF16),
            jax.ShapeDtypeStruct((N_HEADS * V_DIM, n_rows), BF16),
        ],
        compiler_params=_params("parallel"),
        name="kv_expand",
    )(ckv, kpe, wk, wvt)


def _attn_kernel(q_ref, k_ref, vt_ref, o_ref, *, heads):
    def scores(hd):
        cols = slice(hd * HEAD_PAD, (hd + 1) * HEAD_PAD)
        return _dot_nt(k_ref[:, cols], q_ref[:, cols])

    nxt = scores(0)
    for hd in range(heads):
        st = nxt
        if hd + 1 < heads:
            nxt = scores(hd + 1)
        p = jnp.exp(st - jnp.max(st, axis=0, keepdims=True))
        denom = jnp.sum(p, axis=0, keepdims=True)
        ot = _dot(vt_ref[hd * V_DIM:(hd + 1) * V_DIM, :], p.astype(BF16))
        o_ref[:, hd * V_DIM:(hd + 1) * V_DIM] = (ot / denom).T.astype(BF16)


def _attention(q, k, vt, n_batch, lq, lk, heads, tq=256):
    nq = lq // tq
    return pl.pallas_call(
        functools.partial(_attn_kernel, heads=heads),
        grid=(n_batch, N_HEADS // heads, nq),
        in_specs=[
            pl.BlockSpec((tq, heads * HEAD_PAD), lambda b, g, i: (b * nq + i, g)),
            pl.BlockSpec((lk, heads * HEAD_PAD), lambda b, g, i: (b, g)),
            pl.BlockSpec((heads * V_DIM, lk), lambda b, g, i: (g, b)),
        ],
        out_specs=pl.BlockSpec((tq, heads * V_DIM), lambda b, g, i: (b * nq + i, g)),
        out_shape=jax.ShapeDtypeStruct((n_batch * lq, N_HEADS * V_DIM), BF16),
        compiler_params=_params("parallel", "parallel", "arbitrary"),
        name="attention",
    )(q, k, vt)


def _mlp_kernel(h_ref, x_ref, gate_ref, w1_ref, w2_ref, gf_ref, *rest, tm, final, n_cast):
    src_refs, o_ref, dst_refs = rest[:n_cast], rest[n_cast], rest[n_cast + 1:]
    f = pl.program_id(1)

    @pl.when(f == 0)
    def _():
        o_ref[...] = x_ref[...]

    hid = _dot(h_ref[...], w1_ref[...])
    hid = jnp.square(jnp.maximum(hid, 0.0)).astype(BF16)
    o_ref[...] += gate_ref[0] * _dot(hid, w2_ref[...])

    for src_ref, dst_ref in zip(src_refs, dst_refs):
        dst_ref[...] = src_ref[...].astype(BF16)

    if final:
        @pl.when(f == pl.num_programs(1) - 1)
        def _():
            gf = gf_ref[...]
            for s in range(tm // SUB):
                rs = slice(s * SUB, (s + 1) * SUB)
                y = o_ref[rs, :]
                o_ref[rs, :] = y * lax.rsqrt(jnp.mean(y * y, axis=-1, keepdims=True) + EPS) * gf


def _mlp(st, h, x, mod, w1, w2, final_g, layer, final, casts=(), tm=512, tf=1024):
    row = lambda i, f: (i, 0)
    nf = D_FF // tf
    steps = (st.rows // tm) * nf
    slab = lambda i, f: (i * nf + f, 0)
    cast_in, cast_out, cast_shape = [], [], []
    for w, l in casts:
        _, r, c = w.shape
        cast_in.append(pl.BlockSpec((None, r // steps, c), lambda i, f, l=l: (l, i * nf + f, 0)))
        cast_out.append(pl.BlockSpec((r // steps, c), slab))
        cast_shape.append(jax.ShapeDtypeStruct((r, c), BF16))
    out = pl.pallas_call(
        functools.partial(_mlp_kernel, tm=tm, final=final, n_cast=len(casts)),
        grid=(st.rows // tm, nf),
        in_specs=[
            pl.BlockSpec((tm, D_MODEL), row),
            pl.BlockSpec((tm, D_MODEL), row),
            _mod_spec(st, layer, 5, tm),
            pl.BlockSpec((D_MODEL, tf), lambda i, f: (0, f)),
            pl.BlockSpec((tf, D_MODEL), lambda i, f: (f, 0)),
            _resident((1, D_MODEL)),
        ] + cast_in,
        out_specs=[pl.BlockSpec((tm, D_MODEL), row)] + cast_out,
        out_shape=[jax.ShapeDtypeStruct((st.rows, D_MODEL), F32)] + cast_shape,
        compiler_params=_params("parallel", "arbitrary"),
        name="mlp",
    )(h, x, mod, w1, w2, final_g, *[w for w, _ in casts])
    return out[0], out[1:]


def _rope_tables():
    n_rows = DEC_SEQ // GRID_W
    rows = jnp.broadcast_to(jnp.arange(n_rows)[:, None], (n_rows, GRID_W)).reshape(-1)
    cols = jnp.broadcast_to(jnp.arange(GRID_W)[None, :], (n_rows, GRID_W)).reshape(-1)
    nf = QK_ROPE // 4
    inv = 1.0 / (ROPE_THETA ** (jnp.arange(nf, dtype=F32) / nf))
    ang_r = rows.astype(F32)[:, None] * inv
    ang_c = cols.astype(F32)[:, None] * inv
    cr, sr, cc, sc = jnp.cos(ang_r), jnp.sin(ang_r), jnp.cos(ang_c), jnp.sin(ang_c)
    zeros = jnp.zeros((DEC_SEQ, LANE - QK_ROPE), F32)
    cos = jnp.concatenate([cr, cr, cc, cc, zeros], axis=1)
    sin = jnp.concatenate([-sr, sr, -sc, sc, zeros], axis=1)
    return cos, sin


def _swap_rope_partners(w):
    q = QK_ROPE // 4
    return jnp.concatenate([w[..., q:2 * q], w[..., :q], w[..., 3 * q:], w[..., 2 * q:3 * q]], axis=-1)


def _mla_weights(w_q_b, w_kv_a, w_kv_b):
    wq = w_q_b.reshape(Q_RANK, N_HEADS, QK_NOPE + QK_ROPE)
    pad = jnp.zeros((Q_RANK, N_HEADS, HEAD_PAD - QK_NOPE - QK_ROPE), w_q_b.dtype)
    wq_full = jnp.concatenate([wq, pad], axis=-1).reshape(Q_RANK, N_HEADS * HEAD_PAD)
    wq_swap = jnp.concatenate([_swap_rope_partners(wq[..., QK_NOPE:]), pad], axis=-1).reshape(Q_RANK, N_HEADS * LANE)
    kpe_w = w_kv_a[:, KV_RANK:]
    zpad = jnp.zeros((D_MODEL, LANE - QK_ROPE), w_kv_a.dtype)
    wkv = jnp.concatenate([w_kv_a[:, :KV_RANK], kpe_w, zpad, _swap_rope_partners(kpe_w), zpad], axis=1)
    wkvb = w_kv_b.reshape(KV_RANK, N_HEADS, QK_NOPE + V_DIM)
    wk = wkvb[..., :QK_NOPE].reshape(KV_RANK, N_HEADS * QK_NOPE)
    wvt = wkvb[..., QK_NOPE:].reshape(KV_RANK, N_HEADS * V_DIM).T
    return wq_full.astype(BF16), wq_swap.astype(BF16), wkv.astype(BF16), wk.astype(BF16), wvt.astype(BF16)


def kernel(x_prompt, x_sample, cache_ckv, cache_kpe, c, c_ctx, ada_w, ada_b, norm1, norm2, conv_w_in, conv_w, conv_w_out, gmlp_w_in, gmlp_g_v, gmlp_w_s, gmlp_b_s, gmlp_w_out, mla_w_q_a, mla_g_q, mla_w_q_b, mla_w_kv_a, mla_g_kv, mla_w_kv_b, mla_w_o, mlp_w1, mlp_w2, final_norm):
    D = D_MODEL
    streams = (PROMPT, SAMPLE)
    xs = [x_prompt.reshape(PROMPT.rows, D), x_sample.reshape(SAMPLE.rows, D)]
    cond = jnp.concatenate([c_ctx[None, :], c, jnp.zeros((N_GROUPS - 1 - DEC_BATCH, D), F32)], axis=0)
    mod = _modvec(cond, ada_w, ada_b)

    n1, n2 = norm1[:, None, :], norm2[:, None, :]
    final_g = final_norm[None, :]

    def big_weights(i):
        mix_in, mix_out = ((conv_w_in, conv_w_out), (gmlp_w_in, gmlp_w_out), (mla_w_q_a, mla_w_o))[i % N_MIXERS]
        return {"w1": (mlp_w1, i), "in": (mix_in, i // N_MIXERS), "w2": (mlp_w2, i), "out": (mix_out, i // N_MIXERS)}

    cast_split = (("w1", "in"), ("w2", "out"))
    bf = {name: w[l].astype(BF16) for name, (w, l) in big_weights(0).items()}

    cos_t, sin_t = _rope_tables()
    new_ckv, new_kpe = [], []
    for i in range(DEPTH):
        kind, j = i % N_MIXERS, i // N_MIXERS
        hs = [None, None]
        if kind == 0:
            for n, st in enumerate(streams):
                bg, p = _conv_in(st, xs[n], mod, n1, bf["in"], i)
                xs[n], hs[n] = _conv_out(st, bg, p, conv_w, xs[n], mod, bf["out"], n2, i, j)
        elif kind == 1:
            b_full = jnp.repeat(gmlp_b_s[j].T, LANE, axis=1)
            w_s = gmlp_w_s[j].astype(BF16)
            for n, st in enumerate(streams):
                z = _gmlp_in(st, xs[n], mod, n1, bf["in"], i)
                xs[n], hs[n] = _gmlp_out(st, z, gmlp_g_v[:, None, :], w_s, b_full, xs[n], mod, bf["out"], n2, i, j)
        else:
            wq, wqs, wkv, wk, wvt = _mla_weights(mla_w_q_b[j], mla_w_kv_a[j], mla_w_kv_b[j])
            proj = [_mla_proj(st, xs[n], mod, n1, bf["in"], mla_g_q[:, None, :], wq, wqs, wkv, mla_g_kv[:, None, :],
                              cos_t, sin_t, i, j) for n, st in enumerate(streams)]
            q, ckv, kpe, kpe_rot = proj[0]
            new_ckv.append(ckv.reshape(BATCH, SEQ, KV_RANK))
            new_kpe.append(kpe[:, :QK_ROPE].reshape(BATCH, SEQ, QK_ROPE))
            k_p, v_p = _kv_expand(ckv, kpe_rot, wk, wvt)
            o_p = _attention(q, k_p, v_p, BATCH, SEQ, SEQ, **ATTN_PROMPT)
            q, ckv, kpe, kpe_rot = proj[1]
            lk = PAST_LEN + DEC_SEQ
            ckv_s = jnp.concatenate([cache_ckv[:, j], ckv.reshape(DEC_BATCH, DEC_SEQ, KV_RANK)], axis=1)
            cache_pe = jnp.pad(cache_kpe[:, j], ((0, 0), (0, 0), (0, LANE - QK_ROPE))).astype(BF16)
            kpe_s = jnp.concatenate([cache_pe, kpe_rot.reshape(DEC_BATCH, DEC_SEQ, LANE)], axis=1)
            k_s, v_s = _kv_expand(ckv_s.reshape(DEC_BATCH * lk, KV_RANK), kpe_s.reshape(DEC_BATCH * lk, LANE), wk, wvt)
            o_s = _attention(q, k_s, v_s, DEC_BATCH, DEC_SEQ, lk, **ATTN_SAMPLE)
            for n, (st, o) in enumerate(zip(streams, (o_p, o_s))):
                xs[n], hs[n] = _attn_out(st, o, xs[n], mod, bf["out"], n2, i)
        nxt = big_weights(i + 1) if i + 1 < DEPTH else None
        bf_next = {}
        for n, st in enumerate(streams):
            names = cast_split[n] if nxt else ()
            xs[n], cast = _mlp(st, hs[n], xs[n], mod, bf["w1"], bf["w2"], final_g, i, final=(i == DEPTH - 1),
                               casts=[nxt[name] for name in names])
            bf_next.update(zip(names, cast))
        bf = bf_next
    y_prompt = xs[0].reshape(BATCH, SEQ, D)
    y_sample = xs[1].reshape(DEC_BATCH, DEC_SEQ, D)
    return (y_prompt, y_sample, jnp.stack(new_ckv, axis=1), jnp.stack(new_kpe, axis=1))
```

```python
import functools
from typing import NamedTuple

import jax
import jax.numpy as jnp
from jax import lax
from jax.experimental import pallas as pl
from jax.experimental.pallas import tpu as pltpu

D_MODEL = 2048
BATCH = 32
SEQ = 256
DEPTH = 4
DEC_BATCH = 4
DEC_SEQ = 2048
PAST_LEN = 512
GRID_W = 64
N_MIXERS = 3
CONV_WIDTH = 3
CHUNK = 128
GMLP_GROUPS = 16
N_HEADS = 16
QK_NOPE = 128
QK_ROPE = 64
V_DIM = 128
Q_RANK = 512
KV_RANK = 512
ROPE_THETA = 10000.0
D_FF = 4 * D_MODEL
N_MOD = 6
EPS = 1e-6
LOG2_E = 1.4426950408889634

N_GROUPS = 8
HEAD_PAD = 256
LANE = 128
BF16_ROWS = 16
SUB = 256
VMEM_LIMIT = 56 * 1024 * 1024
ATTN_PROMPT = dict(heads=16, tq=256, ahead=16)
ATTN_SAMPLE = dict(heads=4, tq=512, ahead=2)

BF16 = jnp.bfloat16
F32 = jnp.float32


class Stream(NamedTuple):
    rows: int
    seq: int
    group0: int
    group_rows: int
    rope: bool


PROMPT = Stream(BATCH * SEQ, SEQ, 0, BATCH * SEQ, False)
SAMPLE = Stream(DEC_BATCH * DEC_SEQ, DEC_SEQ, 1, DEC_SEQ, True)


def _params(*sem):
    return pltpu.CompilerParams(dimension_semantics=sem, vmem_limit_bytes=VMEM_LIMIT)


def _dot(a, b):
    return jnp.dot(a, b, preferred_element_type=F32)


def _resident(shape, layer=None):
    if layer is None:
        zeros = (0,) * len(shape)
        return pl.BlockSpec(shape, lambda *_: zeros, pipeline_mode=pl.Buffered(1))
    index = (layer,) + (0,) * len(shape)
    return pl.BlockSpec((None,) + tuple(shape), lambda *_: index, pipeline_mode=pl.Buffered(1))


def _mod_spec(st, layer, chunk, tm):
    def index(i, *_):
        return ((layer * N_MOD + chunk) * N_GROUPS + st.group0 + (i * tm) // st.group_rows, 0, 0)

    return pl.BlockSpec((1, 1, D_MODEL), index)


def _gain_shift(g_ref, sh_ref, sc_ref):
    return g_ref[...] * (1.0 + sc_ref[0]), sh_ref[0]


def _norm_mod(x, gsc, sh):
    ms = jnp.mean(x * x, axis=-1, keepdims=True)
    return (x * lax.rsqrt(ms + EPS) * gsc + sh).astype(BF16)


def _modvec_kernel(c_ref, w_ref, b_ref, o_ref):
    s = jax.nn.silu(c_ref[...]).astype(BF16)
    o_ref[0, 0] = _dot(s, w_ref[0].astype(BF16)) + b_ref[0]


def _modvec(cond, ada_w, ada_b, tn=1024):
    per_chunk = D_MODEL // tn
    out = pl.pallas_call(
        _modvec_kernel,
        grid=(DEPTH, N_MOD * per_chunk),
        in_specs=[
            pl.BlockSpec((N_GROUPS, D_MODEL), lambda l, j: (0, 0)),
            pl.BlockSpec((1, D_MODEL, tn), lambda l, j: (l, 0, j)),
            pl.BlockSpec((1, 1, tn), lambda l, j: (l, 0, j)),
        ],
        out_specs=pl.BlockSpec((1, 1, N_GROUPS, tn), lambda l, j: (l, j // per_chunk, 0, j % per_chunk)),
        out_shape=jax.ShapeDtypeStruct((DEPTH, N_MOD, N_GROUPS, D_MODEL), F32),
        compiler_params=_params("arbitrary", "arbitrary"),
        name="modvec",
    )(cond, ada_w, ada_b.reshape(DEPTH, 1, N_MOD * D_MODEL))
    return out.reshape(DEPTH * N_MOD * N_GROUPS, 1, D_MODEL)


def _conv_in_kernel(x_ref, g_ref, sh_ref, sc_ref, w_ref, bg_ref, p_ref, *, tm):
    gsc, sh = _gain_shift(g_ref, sh_ref, sc_ref)
    for s in range(tm // SUB):
        rs = slice(s * SUB, (s + 1) * SUB)
        z = _dot(_norm_mod(x_ref[rs, :], gsc, sh), w_ref[...])
        bg_ref[rs, :] = z[:, :D_MODEL].astype(BF16)
        p_ref[rs, :] = (z[:, D_MODEL:2 * D_MODEL] * z[:, 2 * D_MODEL:]).astype(BF16)


def _conv_in(st, x, mod, norm1, w_in, layer, tm=512):
    row = lambda i: (i, 0)
    return pl.pallas_call(
        functools.partial(_conv_in_kernel, tm=tm),
        grid=(st.rows // tm,),
        in_specs=[
            pl.BlockSpec((tm, D_MODEL), row),
            _resident((1, D_MODEL), layer),
            _mod_spec(st, layer, 0, tm),
            _mod_spec(st, layer, 1, tm),
            _resident((D_MODEL, 3 * D_MODEL)),
        ],
        out_specs=[pl.BlockSpec((tm, D_MODEL), row)] * 2,
        out_shape=[jax.ShapeDtypeStruct((st.rows, D_MODEL), BF16)] * 2,
        compiler_params=_params("parallel"),
        name="conv_in",
    )(x, norm1, mod, mod, w_in)


def _gmlp_in_kernel(x_ref, g_ref, sh_ref, sc_ref, w_ref, z_ref, *, tm):
    gsc, sh = _gain_shift(g_ref, sh_ref, sc_ref)
    for s in range(tm // SUB):
        rs = slice(s * SUB, (s + 1) * SUB)
        z_ref[rs, :] = _dot(_norm_mod(x_ref[rs, :], gsc, sh), w_ref[...]).astype(BF16)


def _gmlp_in(st, x, mod, norm1, w_in, layer, tm=512):
    n_out = w_in.shape[-1]
    row = lambda i: (i, 0)
    return pl.pallas_call(
        functools.partial(_gmlp_in_kernel, tm=tm),
        grid=(st.rows // tm,),
        in_specs=[
            pl.BlockSpec((tm, D_MODEL), row),
            _resident((1, D_MODEL), layer),
            _mod_spec(st, layer, 0, tm),
            _mod_spec(st, layer, 1, tm),
            _resident((D_MODEL, n_out)),
        ],
        out_specs=pl.BlockSpec((tm, n_out), row),
        out_shape=jax.ShapeDtypeStruct((st.rows, n_out), BF16),
        compiler_params=_params("parallel"),
        name="gmlp_in",
    )(x, norm1, mod, mod, w_in)


def _residual_proj(a, rs, x_ref, gate_ref, w_ref, gsc2, sh2, o_ref, h_ref):
    x1 = x_ref[rs, :] + gate_ref[0] * _dot(a, w_ref[...])
    o_ref[rs, :] = x1
    h_ref[rs, :] = _norm_mod(x1, gsc2, sh2)


def _proj_specs(st, layer, tm):
    row = lambda i: (i, 0)
    in_specs = [
        pl.BlockSpec((tm, D_MODEL), row),
        _mod_spec(st, layer, 2, tm),
        _resident((D_MODEL, D_MODEL)),
        _resident((1, D_MODEL), layer),
        _mod_spec(st, layer, 3, tm),
        _mod_spec(st, layer, 4, tm),
    ]
    out_specs = [pl.BlockSpec((tm, D_MODEL), row)] * 2
    out_shape = [jax.ShapeDtypeStruct((st.rows, D_MODEL), F32), jax.ShapeDtypeStruct((st.rows, D_MODEL), BF16)]
    return in_specs, out_specs, out_shape


def _conv_out_kernel(bg_ref, p_ref, lo_ref, hi_ref, cw_ref, x_ref, gate_ref, w_ref, g2_ref, sh2_ref, sc2_ref,
                     o_ref, h_ref, pbuf_ref, *, tm, seq):
    halo = BF16_ROWS
    pbuf_ref[0:halo, :] = lo_ref[...].astype(F32)
    pbuf_ref[halo + tm:2 * halo + tm, :] = hi_ref[...].astype(F32)

    def copy(c, carry):
        r = pl.multiple_of(c * halo, halo)
        pbuf_ref[pl.ds(halo + r, halo), :] = p_ref[pl.ds(r, halo), :].astype(F32)
        return carry

    lax.fori_loop(0, tm // halo, copy, 0, unroll=4)

    gsc2, sh2 = _gain_shift(g2_ref, sh2_ref, sc2_ref)
    base = pl.program_id(0) * tm
    width = 4 * LANE
    for s in range(tm // SUB):
        r0 = s * SUB
        pos = (base + r0 + lax.broadcasted_iota(jnp.int32, (SUB, 1), 0)) % seq
        parts = []
        for lo in range(0, D_MODEL, width):
            cols = slice(lo, lo + width)
            prev = jnp.where(pos == 0, 0.0, pbuf_ref[halo - 1 + r0:halo - 1 + r0 + SUB, cols])
            nxt = jnp.where(pos == seq - 1, 0.0, pbuf_ref[halo + 1 + r0:halo + 1 + r0 + SUB, cols])
            cur = pbuf_ref[halo + r0:halo + r0 + SUB, cols]
            conv = prev * cw_ref[0:1, cols] + cur * cw_ref[1:2, cols] + nxt * cw_ref[2:3, cols]
            parts.append((bg_ref[r0:r0 + SUB, cols].astype(F32) * conv).astype(BF16))
        _residual_proj(jnp.concatenate(parts, axis=1), slice(r0, r0 + SUB), x_ref, gate_ref, w_ref, gsc2, sh2,
                       o_ref, h_ref)


def _conv_out(st, bg, p, conv_w, x, mod, w_out, norm2, layer, j, tm=512):
    nb = tm // BF16_ROWS
    last_halo = st.rows // BF16_ROWS - 1
    row = lambda i: (i, 0)
    tail, out_specs, out_shape = _proj_specs(st, layer, tm)
    return pl.pallas_call(
        functools.partial(_conv_out_kernel, tm=tm, seq=st.seq),
        grid=(st.rows // tm,),
        in_specs=[
            pl.BlockSpec((tm, D_MODEL), row),
            pl.BlockSpec((tm, D_MODEL), row),
            pl.BlockSpec((BF16_ROWS, D_MODEL), lambda i: (jnp.maximum(i * nb - 1, 0), 0)),
            pl.BlockSpec((BF16_ROWS, D_MODEL), lambda i: (jnp.minimum((i + 1) * nb, last_halo), 0)),
            _resident((CONV_WIDTH, D_MODEL), j),
        ] + tail,
        out_specs=out_specs,
        out_shape=out_shape,
        scratch_shapes=[pltpu.VMEM((tm + 2 * BF16_ROWS, D_MODEL), F32)],
        compiler_params=_params("parallel"),
        name="conv_out",
    )(bg, p, p, p, conv_w, x, mod, w_out, norm2, mod, mod)


def _gmlp_out_kernel(zu_ref, zv_ref, gv_ref, ws_ref, bs_ref, x_ref, gate_ref, w_ref, g2_ref, sh2_ref, sc2_ref,
                     o_ref, h_ref, a_ref, vn_ref, *, tm):
    gv = gv_ref[...]
    n_chunks = tm // CHUNK
    for c in range(n_chunks):
        rs = slice(c * CHUNK, (c + 1) * CHUNK)
        v = zv_ref[rs, :].astype(F32)
        ms = jnp.mean(v * v, axis=-1, keepdims=True)
        vn_ref[rs, :] = (v * lax.rsqrt(ms + EPS) * gv).astype(BF16)

    for g in range(GMLP_GROUPS):
        cols = slice(g * LANE, (g + 1) * LANE)
        rhs = jnp.concatenate([vn_ref[c * CHUNK:(c + 1) * CHUNK, cols] for c in range(n_chunks)], axis=1)
        mixed = _dot(ws_ref[g], rhs)
        for c in range(n_chunks):
            rs = slice(c * CHUNK, (c + 1) * CHUNK)
            vv = mixed[:, c * CHUNK:(c + 1) * CHUNK] + bs_ref[:, cols]
            a_ref[rs, cols] = (zu_ref[rs, cols].astype(F32) * vv).astype(BF16)

    gsc2, sh2 = _gain_shift(g2_ref, sh2_ref, sc2_ref)
    for s in range(tm // SUB):
        rs = slice(s * SUB, (s + 1) * SUB)
        _residual_proj(a_ref[rs, :], rs, x_ref, gate_ref, w_ref, gsc2, sh2, o_ref, h_ref)


def _gmlp_out(st, z, g_v, w_s, b_full, x, mod, w_out, norm2, layer, j, tm=512):
    tail, out_specs, out_shape = _proj_specs(st, layer, tm)
    return pl.pallas_call(
        functools.partial(_gmlp_out_kernel, tm=tm),
        grid=(st.rows // tm,),
        in_specs=[
            pl.BlockSpec((tm, D_MODEL), lambda i: (i, 0)),
            pl.BlockSpec((tm, D_MODEL), lambda i: (i, 1)),
            _resident((1, D_MODEL), j),
            _resident((GMLP_GROUPS, CHUNK, CHUNK)),
            _resident((CHUNK, D_MODEL)),
        ] + tail,
        out_specs=out_specs,
        out_shape=out_shape,
        scratch_shapes=[pltpu.VMEM((tm, D_MODEL), BF16), pltpu.VMEM((tm, D_MODEL), BF16)],
        compiler_params=_params("parallel"),
        name="gmlp_out",
    )(z, z, g_v, w_s, b_full, x, mod, w_out, norm2, mod, mod)


def _attn_out_kernel(a_ref, x_ref, gate_ref, w_ref, g2_ref, sh2_ref, sc2_ref, o_ref, h_ref, *, tm):
    gsc2, sh2 = _gain_shift(g2_ref, sh2_ref, sc2_ref)
    for s in range(tm // SUB):
        rs = slice(s * SUB, (s + 1) * SUB)
        _residual_proj(a_ref[rs, :], rs, x_ref, gate_ref, w_ref, gsc2, sh2, o_ref, h_ref)


def _attn_out(st, a, x, mod, w_o, norm2, layer, tm=512):
    tail, out_specs, out_shape = _proj_specs(st, layer, tm)
    return pl.pallas_call(
        functools.partial(_attn_out_kernel, tm=tm),
        grid=(st.rows // tm,),
        in_specs=[pl.BlockSpec((tm, D_MODEL), lambda i: (i, 0))] + tail,
        out_specs=out_specs,
        out_shape=out_shape,
        compiler_params=_params("parallel"),
        name="attn_out",
    )(a, x, mod, w_o, norm2, mod, mod)


def _mla_proj_kernel(x_ref, g_ref, sh_ref, sc_ref, wqa_ref, gq_ref, wq_ref, wqs_ref, wkv_ref, gkv_ref,
                     cos_ref, sin_ref, q_ref, ckv_ref, kpe_ref, kper_ref, *, tm, rope):
    gsc, sh = _gain_shift(g_ref, sh_ref, sc_ref)
    gq = gq_ref[...] * ((QK_NOPE + QK_ROPE) ** -0.5 * LOG2_E)
    for s in range(tm // SUB):
        rs = slice(s * SUB, (s + 1) * SUB)
        h = _norm_mod(x_ref[rs, :], gsc, sh)
        qa = _dot(h, wqa_ref[...])
        qa = qa * lax.rsqrt(jnp.mean(qa * qa, axis=-1, keepdims=True) + EPS) * gq
        qa = qa.astype(BF16)
        q = _dot(qa, wq_ref[...])
        kv = _dot(h, wkv_ref[...])
        c = kv[:, :KV_RANK]
        ckv_ref[rs, :] = c * lax.rsqrt(jnp.mean(c * c, axis=-1, keepdims=True) + EPS) * gkv_ref[...]
        kpe = kv[:, KV_RANK:KV_RANK + LANE]
        kpe_ref[rs, :] = kpe
        if rope:
            cos = cos_ref[rs, :]
            sin = sin_ref[rs, :]
            qs = _dot(qa, wqs_ref[...])
            for hd in range(N_HEADS):
                lo = hd * HEAD_PAD
                q_ref[rs, lo:lo + LANE] = q[:, lo:lo + LANE].astype(BF16)
                rot = q[:, lo + LANE:lo + HEAD_PAD] * cos + qs[:, hd * LANE:(hd + 1) * LANE] * sin
                q_ref[rs, lo + LANE:lo + HEAD_PAD] = rot.astype(BF16)
            kper_ref[rs, :] = (kpe * cos + kv[:, KV_RANK + LANE:] * sin).astype(BF16)
        else:
            q_ref[rs, :] = q.astype(BF16)
            kper_ref[rs, :] = kpe.astype(BF16)


def _mla_proj(st, x, mod, norm1, wqa, gq, wq, wqs, wkv, gkv, cos_t, sin_t, layer, j, tm=512):
    per_seq = st.seq // tm
    rope_idx = (lambda i: (i % per_seq, 0)) if st.rope else (lambda i: (0, 0))
    row = lambda i: (i, 0)
    return pl.pallas_call(
        functools.partial(_mla_proj_kernel, tm=tm, rope=st.rope),
        grid=(st.rows // tm,),
        in_specs=[
            pl.BlockSpec((tm, D_MODEL), row),
            _resident((1, D_MODEL), layer),
            _mod_spec(st, layer, 0, tm),
            _mod_spec(st, layer, 1, tm),
            _resident(wqa.shape),
            _resident((1, Q_RANK), j),
            _resident(wq.shape),
            _resident(wqs.shape),
            _resident(wkv.shape),
            _resident((1, KV_RANK), j),
            pl.BlockSpec((tm, LANE), rope_idx),
            pl.BlockSpec((tm, LANE), rope_idx),
        ],
        out_specs=[
            pl.BlockSpec((tm, N_HEADS * HEAD_PAD), row),
            pl.BlockSpec((tm, KV_RANK), row),
            pl.BlockSpec((tm, LANE), row),
            pl.BlockSpec((tm, LANE), row),
        ],
        out_shape=[
            jax.ShapeDtypeStruct((st.rows, N_HEADS * HEAD_PAD), BF16),
            jax.ShapeDtypeStruct((st.rows, KV_RANK), F32),
            jax.ShapeDtypeStruct((st.rows, LANE), F32),
            jax.ShapeDtypeStruct((st.rows, LANE), BF16),
        ],
        compiler_params=_params("parallel"),
        name="mla_proj",
    )(x, norm1, mod, mod, wqa, gq, wq, wqs, wkv, gkv, cos_t, sin_t)


def _dot_nt(a, b):
    return lax.dot_general(a, b, (((1,), (1,)), ((), ())), preferred_element_type=F32)


def _kv_expand_kernel(c_ref, kpe_ref, wk_ref, wvt_ref, k_ref, vt_ref):
    c = c_ref[...].astype(BF16)
    kn = _dot(c, wk_ref[...])
    vt_ref[...] = _dot_nt(wvt_ref[...], c).astype(BF16)
    kpe = kpe_ref[...]
    for hd in range(N_HEADS):
        lo = hd * HEAD_PAD
        k_ref[:, lo:lo + LANE] = kn[:, hd * LANE:(hd + 1) * LANE].astype(BF16)
        k_ref[:, lo + LANE:lo + HEAD_PAD] = kpe


def _kv_expand(ckv, kpe, wk, wvt, tr=512):
    n_rows = ckv.shape[0]
    row = lambda i: (i, 0)
    return pl.pallas_call(
        _kv_expand_kernel,
        grid=(n_rows // tr,),
        in_specs=[
            pl.BlockSpec((tr, KV_RANK), row),
            pl.BlockSpec((tr, LANE), row),
            _resident(wk.shape),
            _resident(wvt.shape),
        ],
        out_specs=[
            pl.BlockSpec((tr, N_HEADS * HEAD_PAD), row),
            pl.BlockSpec((N_HEADS * V_DIM, tr), lambda i: (0, i)),
        ],
        out_shape=[
            jax.ShapeDtypeStruct((n_rows, N_HEADS * HEAD_PAD), BF16),
            jax.ShapeDtypeStruct((N_HEADS * V_DIM, n_rows), BF16),
        ],
        compiler_params=_params("parallel"),
        name="kv_expand",
    )(ckv, kpe, wk, wvt)


def _attn_kernel(q_ref, k_ref, vt_ref, o_ref, *, heads, ahead):
    ones = jnp.ones((BF16_ROWS, k_ref.shape[0]), BF16)

    def scores(hd):
        cols = slice(hd * HEAD_PAD, (hd + 1) * HEAD_PAD)
        return _dot_nt(k_ref[:, cols], q_ref[:, cols])

    pending = [scores(hd) for hd in range(min(ahead, heads))]
    for hd in range(heads):
        st = pending.pop(0)
        if hd + ahead < heads:
            pending.append(scores(hd + ahead))
        p = jnp.exp2(st - jnp.max(st, axis=0, keepdims=True)).astype(BF16)
        lhs = jnp.concatenate([vt_ref[hd * V_DIM:(hd + 1) * V_DIM, :], ones], axis=0)
        ot = _dot(lhs, p)
        o_ref[:, hd * V_DIM:(hd + 1) * V_DIM] = (ot[:V_DIM] / ot[V_DIM:V_DIM + 1]).T.astype(BF16)


def _attention(q, k, vt, n_batch, lq, lk, heads, tq, ahead):
    nq = lq // tq
    return pl.pallas_call(
        functools.partial(_attn_kernel, heads=heads, ahead=ahead),
        grid=(n_batch, N_HEADS // heads, nq),
        in_specs=[
            pl.BlockSpec((tq, heads * HEAD_PAD), lambda b, g, i: (b * nq + i, g)),
            pl.BlockSpec((lk, heads * HEAD_PAD), lambda b, g, i: (b, g)),
            pl.BlockSpec((heads * V_DIM, lk), lambda b, g, i: (g, b)),
        ],
        out_specs=pl.BlockSpec((tq, heads * V_DIM), lambda b, g, i: (b * nq + i, g)),
        out_shape=jax.ShapeDtypeStruct((n_batch * lq, N_HEADS * V_DIM), BF16),
        compiler_params=_params("parallel", "parallel", "arbitrary"),
        name="attention",
    )(q, k, vt)


def _mlp_kernel(h_ref, x_ref, gate_ref, w1_ref, w2_ref, gf_ref, *rest, tm, final, n_cast):
    src_refs, o_ref, dst_refs = rest[:n_cast], rest[n_cast], rest[n_cast + 1:]
    f = pl.program_id(1)

    @pl.when(f == 0)
    def _():
        o_ref[...] = x_ref[...]

    hid = _dot(h_ref[...], w1_ref[...])
    hid = jnp.square(jnp.maximum(hid, 0.0)).astype(BF16)
    o_ref[...] += gate_ref[0] * _dot(hid, w2_ref[...])

    for src_ref, dst_ref in zip(src_refs, dst_refs):
        dst_ref[...] = src_ref[...].astype(BF16)

    if final:
        @pl.when(f == pl.num_programs(1) - 1)
        def _():
            gf = gf_ref[...]
            for s in range(tm // SUB):
                rs = slice(s * SUB, (s + 1) * SUB)
                y = o_ref[rs, :]
                o_ref[rs, :] = y * lax.rsqrt(jnp.mean(y * y, axis=-1, keepdims=True) + EPS) * gf


def _mlp(st, h, x, mod, w1, w2, final_g, layer, final, casts=(), tm=512, tf=1024):
    row = lambda i, f: (i, 0)
    nf = D_FF // tf
    steps = (st.rows // tm) * nf
    slab = lambda i, f: (i * nf + f, 0)
    cast_in, cast_out, cast_shape = [], [], []
    for w, l in casts:
        _, r, c = w.shape
        cast_in.append(pl.BlockSpec((None, r // steps, c), lambda i, f, l=l: (l, i * nf + f, 0)))
        cast_out.append(pl.BlockSpec((r // steps, c), slab))
        cast_shape.append(jax.ShapeDtypeStruct((r, c), BF16))
    out = pl.pallas_call(
        functools.partial(_mlp_kernel, tm=tm, final=final, n_cast=len(casts)),
        grid=(st.rows // tm, nf),
        in_specs=[
            pl.BlockSpec((tm, D_MODEL), row),
            pl.BlockSpec((tm, D_MODEL), row),
            _mod_spec(st, layer, 5, tm),
            pl.BlockSpec((D_MODEL, tf), lambda i, f: (0, f)),
            pl.BlockSpec((tf, D_MODEL), lambda i, f: (f, 0)),
            _resident((1, D_MODEL)),
        ] + cast_in,
        out_specs=[pl.BlockSpec((tm, D_MODEL), row)] + cast_out,
        out_shape=[jax.ShapeDtypeStruct((st.rows, D_MODEL), F32)] + cast_shape,
        compiler_params=_params("parallel", "arbitrary"),
        name="mlp",
    )(h, x, mod, w1, w2, final_g, *[w for w, _ in casts])
    return out[0], out[1:]


def _rope_tables():
    n_rows = DEC_SEQ // GRID_W
    rows = jnp.broadcast_to(jnp.arange(n_rows)[:, None], (n_rows, GRID_W)).reshape(-1)
    cols = jnp.broadcast_to(jnp.arange(GRID_W)[None, :], (n_rows, GRID_W)).reshape(-1)
    nf = QK_ROPE // 4
    inv = 1.0 / (ROPE_THETA ** (jnp.arange(nf, dtype=F32) / nf))
    ang_r = rows.astype(F32)[:, None] * inv
    ang_c = cols.astype(F32)[:, None] * inv
    cr, sr, cc, sc = jnp.cos(ang_r), jnp.sin(ang_r), jnp.cos(ang_c), jnp.sin(ang_c)
    zeros = jnp.zeros((DEC_SEQ, LANE - QK_ROPE), F32)
    cos = jnp.concatenate([cr, cr, cc, cc, zeros], axis=1)
    sin = jnp.concatenate([-sr, sr, -sc, sc, zeros], axis=1)
    return cos, sin


def _swap_rope_partners(w):
    q = QK_ROPE // 4
    return jnp.concatenate([w[..., q:2 * q], w[..., :q], w[..., 3 * q:], w[..., 2 * q:3 * q]], axis=-1)


def _mla_weights(w_q_b, w_kv_a, w_kv_b):
    wq = w_q_b.reshape(Q_RANK, N_HEADS, QK_NOPE + QK_ROPE)
    pad = jnp.zeros((Q_RANK, N_HEADS, HEAD_PAD - QK_NOPE - QK_ROPE), w_q_b.dtype)
    wq_full = jnp.concatenate([wq, pad], axis=-1).reshape(Q_RANK, N_HEADS * HEAD_PAD)
    wq_swap = jnp.concatenate([_swap_rope_partners(wq[..., QK_NOPE:]), pad], axis=-1).reshape(Q_RANK, N_HEADS * LANE)
    kpe_w = w_kv_a[:, KV_RANK:]
    zpad = jnp.zeros((D_MODEL, LANE - QK_ROPE), w_kv_a.dtype)
    wkv = jnp.concatenate([w_kv_a[:, :KV_RANK], kpe_w, zpad, _swap_rope_partners(kpe_w), zpad], axis=1)
    wkvb = w_kv_b.reshape(KV_RANK, N_HEADS, QK_NOPE + V_DIM)
    wk = wkvb[..., :QK_NOPE].reshape(KV_RANK, N_HEADS * QK_NOPE)
    wvt = wkvb[..., QK_NOPE:].reshape(KV_RANK, N_HEADS * V_DIM).T
    return wq_full.astype(BF16), wq_swap.astype(BF16), wkv.astype(BF16), wk.astype(BF16), wvt.astype(BF16)


def kernel(x_prompt, x_sample, cache_ckv, cache_kpe, c, c_ctx, ada_w, ada_b, norm1, norm2, conv_w_in, conv_w, conv_w_out, gmlp_w_in, gmlp_g_v, gmlp_w_s, gmlp_b_s, gmlp_w_out, mla_w_q_a, mla_g_q, mla_w_q_b, mla_w_kv_a, mla_g_kv, mla_w_kv_b, mla_w_o, mlp_w1, mlp_w2, final_norm):
    D = D_MODEL
    streams = (PROMPT, SAMPLE)
    xs = [x_prompt.reshape(PROMPT.rows, D), x_sample.reshape(SAMPLE.rows, D)]
    cond = jnp.concatenate([c_ctx[None, :], c, jnp.zeros((N_GROUPS - 1 - DEC_BATCH, D), F32)], axis=0)
    mod = _modvec(cond, ada_w, ada_b)

    n1, n2 = norm1[:, None, :], norm2[:, None, :]
    final_g = final_norm[None, :]

    def big_weights(i):
        mix_in, mix_out = ((conv_w_in, conv_w_out), (gmlp_w_in, gmlp_w_out), (mla_w_q_a, mla_w_o))[i % N_MIXERS]
        return {"w1": (mlp_w1, i), "in": (mix_in, i // N_MIXERS), "w2": (mlp_w2, i), "out": (mix_out, i // N_MIXERS)}

    cast_split = (("w1", "in"), ("w2", "out"))
    bf = {name: w[l].astype(BF16) for name, (w, l) in big_weights(0).items()}

    cos_t, sin_t = _rope_tables()
    new_ckv, new_kpe = [], []
    for i in range(DEPTH):
        kind, j = i % N_MIXERS, i // N_MIXERS
        hs = [None, None]
        if kind == 0:
            for n, st in enumerate(streams):
                bg, p = _conv_in(st, xs[n], mod, n1, bf["in"], i)
                xs[n], hs[n] = _conv_out(st, bg, p, conv_w, xs[n], mod, bf["out"], n2, i, j)
        elif kind == 1:
            b_full = jnp.repeat(gmlp_b_s[j].T, LANE, axis=1)
            w_s = gmlp_w_s[j].astype(BF16)
            for n, st in enumerate(streams):
                z = _gmlp_in(st, xs[n], mod, n1, bf["in"], i)
                xs[n], hs[n] = _gmlp_out(st, z, gmlp_g_v[:, None, :], w_s, b_full, xs[n], mod, bf["out"], n2, i, j)
        else:
            wq, wqs, wkv, wk, wvt = _mla_weights(mla_w_q_b[j], mla_w_kv_a[j], mla_w_kv_b[j])
            proj = [_mla_proj(st, xs[n], mod, n1, bf["in"], mla_g_q[:, None, :], wq, wqs, wkv, mla_g_kv[:, None, :],
                              cos_t, sin_t, i, j) for n, st in enumerate(streams)]
            q, ckv, kpe, kpe_rot = proj[0]
            new_ckv.append(ckv.reshape(BATCH, SEQ, KV_RANK))
            new_kpe.append(kpe[:, :QK_ROPE].reshape(BATCH, SEQ, QK_ROPE))
            k_p, v_p = _kv_expand(ckv, kpe_rot, wk, wvt)
            o_p = _attention(q, k_p, v_p, BATCH, SEQ, SEQ, **ATTN_PROMPT)
            q, ckv, kpe, kpe_rot = proj[1]
            lk = PAST_LEN + DEC_SEQ
            ckv_s = jnp.concatenate([cache_ckv[:, j], ckv.reshape(DEC_BATCH, DEC_SEQ, KV_RANK)], axis=1)
            cache_pe = jnp.pad(cache_kpe[:, j], ((0, 0), (0, 0), (0, LANE - QK_ROPE))).astype(BF16)
            kpe_s = jnp.concatenate([cache_pe, kpe_rot.reshape(DEC_BATCH, DEC_SEQ, LANE)], axis=1)
            k_s, v_s = _kv_expand(ckv_s.reshape(DEC_BATCH * lk, KV_RANK), kpe_s.reshape(DEC_BATCH * lk, LANE), wk, wvt)
            o_s = _attention(q, k_s, v_s, DEC_BATCH, DEC_SEQ, lk, **ATTN_SAMPLE)
            for n, (st, o) in enumerate(zip(streams, (o_p, o_s))):
                xs[n], hs[n] = _attn_out(st, o, xs[n], mod, bf["out"], n2, i)
        nxt = big_weights(i + 1) if i + 1 < DEPTH else None
        bf_next = {}
        for n, st in enumerate(streams):
            names = cast_split[n] if nxt else ()
            xs[n], cast = _mlp(st, hs[n], xs[n], mod, bf["w1"], bf["w2"], final_g, i, final=(i == DEPTH - 1),
                               casts=[nxt[name] for name in names])
            bf_next.update(zip(names, cast))
        bf = bf_next
    y_prompt = xs[0].reshape(BATCH, SEQ, D)
    y_sample = xs[1].reshape(DEC_BATCH, DEC_SEQ, D)
    return (y_prompt, y_sample, jnp.stack(new_ckv, axis=1), jnp.stack(new_kpe, axis=1))
```

```python
import functools
from typing import NamedTuple

import jax
import jax.numpy as jnp
from jax import lax
from jax.experimental import pallas as pl
from jax.experimental.pallas import tpu as pltpu

D_MODEL = 2048
BATCH = 32
SEQ = 256
DEPTH = 4
DEC_BATCH = 4
DEC_SEQ = 2048
PAST_LEN = 512
GRID_W = 64
N_MIXERS = 3
CONV_WIDTH = 3
CHUNK = 128
GMLP_GROUPS = 16
N_HEADS = 16
QK_NOPE = 128
QK_ROPE = 64
V_DIM = 128
Q_RANK = 512
KV_RANK = 512
ROPE_THETA = 10000.0
D_FF = 4 * D_MODEL
N_MOD = 6
EPS = 1e-6
LOG2_E = 1.4426950408889634

N_GROUPS = 8
HEAD_PAD = 256
LANE = 128
BF16_ROWS = 16
SUB = 256
VMEM_LIMIT = 56 * 1024 * 1024
ATTN_PROMPT = dict(heads=16, tq=256, ahead=16)
ATTN_SAMPLE = dict(heads=4, tq=512, ahead=1)

BF16 = jnp.bfloat16
F32 = jnp.float32


class Stream(NamedTuple):
    rows: int
    seq: int
    group0: int
    group_rows: int
    rope: bool


PROMPT = Stream(BATCH * SEQ, SEQ, 0, BATCH * SEQ, False)
SAMPLE = Stream(DEC_BATCH * DEC_SEQ, DEC_SEQ, 1, DEC_SEQ, True)


def _params(*sem):
    return pltpu.CompilerParams(dimension_semantics=sem, vmem_limit_bytes=VMEM_LIMIT)


def _dot(a, b):
    return jnp.dot(a, b, preferred_element_type=F32)


def _resident(shape, layer=None):
    if layer is None:
        zeros = (0,) * len(shape)
        return pl.BlockSpec(shape, lambda *_: zeros, pipeline_mode=pl.Buffered(1))
    index = (layer,) + (0,) * len(shape)
    return pl.BlockSpec((None,) + tuple(shape), lambda *_: index, pipeline_mode=pl.Buffered(1))


def _mod_spec(st, layer, chunk, tm):
    def index(i, *_):
        return ((layer * N_MOD + chunk) * N_GROUPS + st.group0 + (i * tm) // st.group_rows, 0, 0)

    return pl.BlockSpec((1, 1, D_MODEL), index)


def _gain_shift(g_ref, sh_ref, sc_ref):
    return g_ref[...] * (1.0 + sc_ref[0]), sh_ref[0]


def _norm_mod(x, gsc, sh):
    ms = jnp.mean(x * x, axis=-1, keepdims=True)
    return (x * lax.rsqrt(ms + EPS) * gsc + sh).astype(BF16)


def _modvec_kernel(c_ref, w_ref, b_ref, o_ref):
    s = jax.nn.silu(c_ref[...]).astype(BF16)
    o_ref[0, 0] = _dot(s, w_ref[0].astype(BF16)) + b_ref[0]


def _modvec(cond, ada_w, ada_b, tn=1024):
    per_chunk = D_MODEL // tn
    out = pl.pallas_call(
        _modvec_kernel,
        grid=(DEPTH, N_MOD * per_chunk),
        in_specs=[
            pl.BlockSpec((N_GROUPS, D_MODEL), lambda l, j: (0, 0)),
            pl.BlockSpec((1, D_MODEL, tn), lambda l, j: (l, 0, j)),
            pl.BlockSpec((1, 1, tn), lambda l, j: (l, 0, j)),
        ],
        out_specs=pl.BlockSpec((1, 1, N_GROUPS, tn), lambda l, j: (l, j // per_chunk, 0, j % per_chunk)),
        out_shape=jax.ShapeDtypeStruct((DEPTH, N_MOD, N_GROUPS, D_MODEL), F32),
        compiler_params=_params("arbitrary", "arbitrary"),
        name="modvec",
    )(cond, ada_w, ada_b.reshape(DEPTH, 1, N_MOD * D_MODEL))
    return out.reshape(DEPTH * N_MOD * N_GROUPS, 1, D_MODEL)


def _conv_in_kernel(x_ref, g_ref, sh_ref, sc_ref, w_ref, bg_ref, p_ref, *, tm):
    gsc, sh = _gain_shift(g_ref, sh_ref, sc_ref)
    for s in range(tm // SUB):
        rs = slice(s * SUB, (s + 1) * SUB)
        z = _dot(_norm_mod(x_ref[rs, :], gsc, sh), w_ref[...])
        bg_ref[rs, :] = z[:, :D_MODEL].astype(BF16)
        p_ref[rs, :] = (z[:, D_MODEL:2 * D_MODEL] * z[:, 2 * D_MODEL:]).astype(BF16)


def _conv_in(st, x, mod, norm1, w_in, layer, tm=512):
    row = lambda i: (i, 0)
    return pl.pallas_call(
        functools.partial(_conv_in_kernel, tm=tm),
        grid=(st.rows // tm,),
        in_specs=[
            pl.BlockSpec((tm, D_MODEL), row),
            _resident((1, D_MODEL), layer),
            _mod_spec(st, layer, 0, tm),
            _mod_spec(st, layer, 1, tm),
            _resident((D_MODEL, 3 * D_MODEL)),
        ],
        out_specs=[pl.BlockSpec((tm, D_MODEL), row)] * 2,
        out_shape=[jax.ShapeDtypeStruct((st.rows, D_MODEL), BF16)] * 2,
        compiler_params=_params("parallel"),
        name="conv_in",
    )(x, norm1, mod, mod, w_in)


def _gmlp_in_kernel(x_ref, g_ref, sh_ref, sc_ref, w_ref, z_ref, *, tm):
    gsc, sh = _gain_shift(g_ref, sh_ref, sc_ref)
    for s in range(tm // SUB):
        rs = slice(s * SUB, (s + 1) * SUB)
        z_ref[rs, :] = _dot(_norm_mod(x_ref[rs, :], gsc, sh), w_ref[...]).astype(BF16)


def _gmlp_in(st, x, mod, norm1, w_in, layer, tm=512):
    n_out = w_in.shape[-1]
    row = lambda i: (i, 0)
    return pl.pallas_call(
        functools.partial(_gmlp_in_kernel, tm=tm),
        grid=(st.rows // tm,),
        in_specs=[
            pl.BlockSpec((tm, D_MODEL), row),
            _resident((1, D_MODEL), layer),
            _mod_spec(st, layer, 0, tm),
            _mod_spec(st, layer, 1, tm),
            _resident((D_MODEL, n_out)),
        ],
        out_specs=pl.BlockSpec((tm, n_out), row),
        out_shape=jax.ShapeDtypeStruct((st.rows, n_out), BF16),
        compiler_params=_params("parallel"),
        name="gmlp_in",
    )(x, norm1, mod, mod, w_in)


def _residual_proj(a, rs, x_ref, gate_ref, w_ref, gsc2, sh2, o_ref, h_ref):
    x1 = x_ref[rs, :] + gate_ref[0] * _dot(a, w_ref[...])
    o_ref[rs, :] = x1
    h_ref[rs, :] = _norm_mod(x1, gsc2, sh2)


def _proj_specs(st, layer, tm):
    row = lambda i: (i, 0)
    in_specs = [
        pl.BlockSpec((tm, D_MODEL), row),
        _mod_spec(st, layer, 2, tm),
        _resident((D_MODEL, D_MODEL)),
        _resident((1, D_MODEL), layer),
        _mod_spec(st, layer, 3, tm),
        _mod_spec(st, layer, 4, tm),
    ]
    out_specs = [pl.BlockSpec((tm, D_MODEL), row)] * 2
    out_shape = [jax.ShapeDtypeStruct((st.rows, D_MODEL), F32), jax.ShapeDtypeStruct((st.rows, D_MODEL), BF16)]
    return in_specs, out_specs, out_shape


def _conv_out_kernel(bg_ref, p_ref, lo_ref, hi_ref, cw_ref, x_ref, gate_ref, w_ref, g2_ref, sh2_ref, sc2_ref,
                     o_ref, h_ref, pbuf_ref, *, tm, seq):
    halo = BF16_ROWS
    n_sub = tm // SUB
    gsc2, sh2 = _gain_shift(g2_ref, sh2_ref, sc2_ref)
    base = pl.program_id(0) * tm
    width = 4 * LANE
    for s in range(n_sub):
        r0 = s * SUB
        before = lo_ref[...] if s == 0 else p_ref[r0 - halo:r0, :]
        after = hi_ref[...] if s == n_sub - 1 else p_ref[r0 + SUB:r0 + SUB + halo, :]
        pbuf_ref[s, 0:halo, :] = jnp.where((base + r0) % seq == 0, 0.0, before.astype(F32))
        pbuf_ref[s, halo:halo + SUB, :] = p_ref[r0:r0 + SUB, :].astype(F32)
        pbuf_ref[s, halo + SUB:, :] = jnp.where((base + r0 + SUB) % seq == 0, 0.0, after.astype(F32))
        parts = []
        for lo in range(0, D_MODEL, width):
            cols = slice(lo, lo + width)
            prev = pbuf_ref[s, halo - 1:halo - 1 + SUB, cols]
            cur = pbuf_ref[s, halo:halo + SUB, cols]
            nxt = pbuf_ref[s, halo + 1:halo + 1 + SUB, cols]
            conv = prev * cw_ref[0:1, cols] + cur * cw_ref[1:2, cols] + nxt * cw_ref[2:3, cols]
            parts.append((bg_ref[r0:r0 + SUB, cols].astype(F32) * conv).astype(BF16))
        _residual_proj(jnp.concatenate(parts, axis=1), slice(r0, r0 + SUB), x_ref, gate_ref, w_ref, gsc2, sh2,
                       o_ref, h_ref)


def _conv_out(st, bg, p, conv_w, x, mod, w_out, norm2, layer, j, tm=512):
    assert st.seq % SUB == 0 and tm % SUB == 0
    nb = tm // BF16_ROWS
    last_halo = st.rows // BF16_ROWS - 1
    row = lambda i: (i, 0)
    tail, out_specs, out_shape = _proj_specs(st, layer, tm)
    return pl.pallas_call(
        functools.partial(_conv_out_kernel, tm=tm, seq=st.seq),
        grid=(st.rows // tm,),
        in_specs=[
            pl.BlockSpec((tm, D_MODEL), row),
            pl.BlockSpec((tm, D_MODEL), row),
            pl.BlockSpec((BF16_ROWS, D_MODEL), lambda i: (jnp.maximum(i * nb - 1, 0), 0)),
            pl.BlockSpec((BF16_ROWS, D_MODEL), lambda i: (jnp.minimum((i + 1) * nb, last_halo), 0)),
            _resident((CONV_WIDTH, D_MODEL), j),
        ] + tail,
        out_specs=out_specs,
        out_shape=out_shape,
        scratch_shapes=[pltpu.VMEM((tm // SUB, SUB + 2 * BF16_ROWS, D_MODEL), F32)],
        compiler_params=_params("parallel"),
        name="conv_out",
    )(bg, p, p, p, conv_w, x, mod, w_out, norm2, mod, mod)


def _gmlp_out_kernel(zu_ref, zv_ref, gv_ref, ws_ref, bs_ref, x_ref, gate_ref, w_ref, g2_ref, sh2_ref, sc2_ref,
                     o_ref, h_ref, a_ref, vn_ref, *, tm):
    gv = gv_ref[...]
    n_chunks = tm // CHUNK
    for c in range(n_chunks):
        rs = slice(c * CHUNK, (c + 1) * CHUNK)
        v = zv_ref[rs, :].astype(F32)
        ms = jnp.mean(v * v, axis=-1, keepdims=True)
        vn_ref[rs, :] = (v * lax.rsqrt(ms + EPS) * gv).astype(BF16)

    for g in range(GMLP_GROUPS):
        cols = slice(g * LANE, (g + 1) * LANE)
        rhs = jnp.concatenate([vn_ref[c * CHUNK:(c + 1) * CHUNK, cols] for c in range(n_chunks)], axis=1)
        mixed = _dot(ws_ref[g], rhs)
        for c in range(n_chunks):
            rs = slice(c * CHUNK, (c + 1) * CHUNK)
            vv = mixed[:, c * CHUNK:(c + 1) * CHUNK] + bs_ref[:, cols]
            a_ref[rs, cols] = (zu_ref[rs, cols].astype(F32) * vv).astype(BF16)

    gsc2, sh2 = _gain_shift(g2_ref, sh2_ref, sc2_ref)
    for s in range(tm // SUB):
        rs = slice(s * SUB, (s + 1) * SUB)
        _residual_proj(a_ref[rs, :], rs, x_ref, gate_ref, w_ref, gsc2, sh2, o_ref, h_ref)


def _gmlp_out(st, z, g_v, w_s, b_full, x, mod, w_out, norm2, layer, j, tm=512):
    tail, out_specs, out_shape = _proj_specs(st, layer, tm)
    return pl.pallas_call(
        functools.partial(_gmlp_out_kernel, tm=tm),
        grid=(st.rows // tm,),
        in_specs=[
            pl.BlockSpec((tm, D_MODEL), lambda i: (i, 0)),
            pl.BlockSpec((tm, D_MODEL), lambda i: (i, 1)),
            _resident((1, D_MODEL), j),
            _resident((GMLP_GROUPS, CHUNK, CHUNK)),
            _resident((CHUNK, D_MODEL)),
        ] + tail,
        out_specs=out_specs,
        out_shape=out_shape,
        scratch_shapes=[pltpu.VMEM((tm, D_MODEL), BF16), pltpu.VMEM((tm, D_MODEL), BF16)],
        compiler_params=_params("parallel"),
        name="gmlp_out",
    )(z, z, g_v, w_s, b_full, x, mod, w_out, norm2, mod, mod)


def _attn_out_kernel(a_ref, x_ref, gate_ref, w_ref, g2_ref, sh2_ref, sc2_ref, o_ref, h_ref, *, tm):
    gsc2, sh2 = _gain_shift(g2_ref, sh2_ref, sc2_ref)
    for s in range(tm // SUB):
        rs = slice(s * SUB, (s + 1) * SUB)
        _residual_proj(a_ref[rs, :], rs, x_ref, gate_ref, w_ref, gsc2, sh2, o_ref, h_ref)


def _attn_out(st, a, x, mod, w_o, norm2, layer, tm=512):
    tail, out_specs, out_shape = _proj_specs(st, layer, tm)
    return pl.pallas_call(
        functools.partial(_attn_out_kernel, tm=tm),
        grid=(st.rows // tm,),
        in_specs=[pl.BlockSpec((tm, D_MODEL), lambda i: (i, 0))] + tail,
        out_specs=out_specs,
        out_shape=out_shape,
        compiler_params=_params("parallel"),
        name="attn_out",
    )(a, x, mod, w_o, norm2, mod, mod)


def _mla_proj_kernel(x_ref, g_ref, sh_ref, sc_ref, wqa_ref, gq_ref, wq_ref, wqs_ref, wkv_ref, gkv_ref,
                     cos_ref, sin_ref, q_ref, ckv_ref, kpe_ref, kper_ref, *, tm, rope):
    gsc, sh = _gain_shift(g_ref, sh_ref, sc_ref)
    gq = gq_ref[...] * ((QK_NOPE + QK_ROPE) ** -0.5 * LOG2_E)
    for s in range(tm // SUB):
        rs = slice(s * SUB, (s + 1) * SUB)
        h = _norm_mod(x_ref[rs, :], gsc, sh)
        qa = _dot(h, wqa_ref[...])
        qa = qa * lax.rsqrt(jnp.mean(qa * qa, axis=-1, keepdims=True) + EPS) * gq
        qa = qa.astype(BF16)
        q = _dot(qa, wq_ref[...])
        kv = _dot(h, wkv_ref[...])
        c = kv[:, :KV_RANK]
        ckv_ref[rs, :] = c * lax.rsqrt(jnp.mean(c * c, axis=-1, keepdims=True) + EPS) * gkv_ref[...]
        kpe = kv[:, KV_RANK:KV_RANK + LANE]
        kpe_ref[rs, :] = kpe
        if rope:
            cos = cos_ref[rs, :]
            sin = sin_ref[rs, :]
            qs = _dot(qa, wqs_ref[...])
            for hd in range(N_HEADS):
                lo = hd * HEAD_PAD
                q_ref[rs, lo:lo + LANE] = q[:, lo:lo + LANE].astype(BF16)
                rot = q[:, lo + LANE:lo + HEAD_PAD] * cos + qs[:, hd * LANE:(hd + 1) * LANE] * sin
                q_ref[rs, lo + LANE:lo + HEAD_PAD] = rot.astype(BF16)
            kper_ref[rs, :] = (kpe * cos + kv[:, KV_RANK + LANE:] * sin).astype(BF16)
        else:
            q_ref[rs, :] = q.astype(BF16)
            kper_ref[rs, :] = kpe.astype(BF16)


def _mla_proj(st, x, mod, norm1, wqa, gq, wq, wqs, wkv, gkv, cos_t, sin_t, layer, j, tm=512):
    per_seq = st.seq // tm
    rope_idx = (lambda i: (i % per_seq, 0)) if st.rope else (lambda i: (0, 0))
    row = lambda i: (i, 0)
    return pl.pallas_call(
        functools.partial(_mla_proj_kernel, tm=tm, rope=st.rope),
        grid=(st.rows // tm,),
        in_specs=[
            pl.BlockSpec((tm, D_MODEL), row),
            _resident((1, D_MODEL), layer),
            _mod_spec(st, layer, 0, tm),
            _mod_spec(st, layer, 1, tm),
            _resident(wqa.shape),
            _resident((1, Q_RANK), j),
            _resident(wq.shape),
            _resident(wqs.shape),
            _resident(wkv.shape),
            _resident((1, KV_RANK), j),
            pl.BlockSpec((tm, LANE), rope_idx),
            pl.BlockSpec((tm, LANE), rope_idx),
        ],
        out_specs=[
            pl.BlockSpec((tm, N_HEADS * HEAD_PAD), row),
            pl.BlockSpec((tm, KV_RANK), row),
            pl.BlockSpec((tm, LANE), row),
            pl.BlockSpec((tm, LANE), row),
        ],
        out_shape=[
            jax.ShapeDtypeStruct((st.rows, N_HEADS * HEAD_PAD), BF16),
            jax.ShapeDtypeStruct((st.rows, KV_RANK), F32),
            jax.ShapeDtypeStruct((st.rows, LANE), F32),
            jax.ShapeDtypeStruct((st.rows, LANE), BF16),
        ],
        compiler_params=_params("parallel"),
        name="mla_proj",
    )(x, norm1, mod, mod, wqa, gq, wq, wqs, wkv, gkv, cos_t, sin_t)


def _dot_nt(a, b):
    return lax.dot_general(a, b, (((1,), (1,)), ((), ())), preferred_element_type=F32)


def _kv_expand_kernel(c_ref, kpe_ref, wk_ref, wvt_ref, k_ref, vt_ref):
    c = c_ref[...].astype(BF16)
    kn = _dot(c, wk_ref[...])
    vt_ref[...] = _dot_nt(wvt_ref[...], c).astype(BF16)
    kpe = kpe_ref[...]
    for hd in range(N_HEADS):
        lo = hd * HEAD_PAD
        k_ref[:, lo:lo + LANE] = kn[:, hd * LANE:(hd + 1) * LANE].astype(BF16)
        k_ref[:, lo + LANE:lo + HEAD_PAD] = kpe


def _kv_expand(ckv, kpe, wk, wvt, tr=512):
    n_rows = ckv.shape[0]
    row = lambda i: (i, 0)
    return pl.pallas_call(
        _kv_expand_kernel,
        grid=(n_rows // tr,),
        in_specs=[
            pl.BlockSpec((tr, KV_RANK), row),
            pl.BlockSpec((tr, LANE), row),
            _resident(wk.shape),
            _resident(wvt.shape),
        ],
        out_specs=[
            pl.BlockSpec((tr, N_HEADS * HEAD_PAD), row),
            pl.BlockSpec((N_HEADS * V_DIM, tr), lambda i: (0, i)),
        ],
        out_shape=[
            jax.ShapeDtypeStruct((n_rows, N_HEADS * HEAD_PAD), BF16),
            jax.ShapeDtypeStruct((N_HEADS * V_DIM, n_rows), BF16),
        ],
        compiler_params=_params("parallel"),
        name="kv_expand",
    )(ckv, kpe, wk, wvt)


def _attn_kernel(q_ref, k_ref, vt_ref, o_ref, *, heads, ahead):
    ones = jnp.ones((BF16_ROWS, k_ref.shape[0]), BF16)

    def scores(hd):
        cols = slice(hd * HEAD_PAD, (hd + 1) * HEAD_PAD)
        return _dot_nt(k_ref[:, cols], q_ref[:, cols])

    pending = [scores(hd) for hd in range(min(ahead, heads))]
    for hd in range(heads):
        st = pending.pop(0)
        if hd + ahead < heads:
            pending.append(scores(hd + ahead))
        p = jnp.exp2(st - jnp.max(st, axis=0, keepdims=True)).astype(BF16)
        lhs = jnp.concatenate([vt_ref[hd * V_DIM:(hd + 1) * V_DIM, :], ones], axis=0)
        ot = _dot(lhs, p)
        o_ref[:, hd * V_DIM:(hd + 1) * V_DIM] = (ot[:V_DIM] / ot[V_DIM:V_DIM + 1]).T.astype(BF16)


def _attention(q, k, vt, n_batch, lq, lk, heads, tq, ahead):
    nq = lq // tq
    return pl.pallas_call(
        functools.partial(_attn_kernel, heads=heads, ahead=ahead),
        grid=(n_batch, N_HEADS // heads, nq),
        in_specs=[
            pl.BlockSpec((tq, heads * HEAD_PAD), lambda b, g, i: (b * nq + i, g)),
            pl.BlockSpec((lk, heads * HEAD_PAD), lambda b, g, i: (b, g)),
            pl.BlockSpec((heads * V_DIM, lk), lambda b, g, i: (g, b)),
        ],
        out_specs=pl.BlockSpec((tq, heads * V_DIM), lambda b, g, i: (b * nq + i, g)),
        out_shape=jax.ShapeDtypeStruct((n_batch * lq, N_HEADS * V_DIM), BF16),
        compiler_params=_params("parallel", "parallel", "arbitrary"),
        name="attention",
    )(q, k, vt)


def _mlp_kernel(h_ref, x_ref, gate_ref, w1_ref, w2_ref, gf_ref, *rest, tm, final, n_cast):
    src_refs, o_ref, dst_refs = rest[:n_cast], rest[n_cast], rest[n_cast + 1:]
    f = pl.program_id(1)

    @pl.when(f == 0)
    def _():
        o_ref[...] = x_ref[...]

    hid = _dot(h_ref[...], w1_ref[...])
    hid = jnp.square(jnp.maximum(hid, 0.0)).astype(BF16)
    o_ref[...] += gate_ref[0] * _dot(hid, w2_ref[...])

    for src_ref, dst_ref in zip(src_refs, dst_refs):
        dst_ref[...] = src_ref[...].astype(BF16)

    if final:
        @pl.when(f == pl.num_programs(1) - 1)
        def _():
            gf = gf_ref[...]
            for s in range(tm // SUB):
                rs = slice(s * SUB, (s + 1) * SUB)
                y = o_ref[rs, :]
                o_ref[rs, :] = y * lax.rsqrt(jnp.mean(y * y, axis=-1, keepdims=True) + EPS) * gf


def _mlp(st, h, x, mod, w1, w2, final_g, layer, final, casts=(), tm=1024, tf=512):
    row = lambda i, f: (i, 0)
    nf = D_FF // tf
    steps = (st.rows // tm) * nf
    slab = lambda i, f: (i * nf + f, 0)
    cast_in, cast_out, cast_shape = [], [], []
    for w, l in casts:
        _, r, c = w.shape
        cast_in.append(pl.BlockSpec((None, r // steps, c), lambda i, f, l=l: (l, i * nf + f, 0)))
        cast_out.append(pl.BlockSpec((r // steps, c), slab))
        cast_shape.append(jax.ShapeDtypeStruct((r, c), BF16))
    out = pl.pallas_call(
        functools.partial(_mlp_kernel, tm=tm, final=final, n_cast=len(casts)),
        grid=(st.rows // tm, nf),
        in_specs=[
            pl.BlockSpec((tm, D_MODEL), row),
            pl.BlockSpec((tm, D_MODEL), row),
            _mod_spec(st, layer, 5, tm),
            pl.BlockSpec((D_MODEL, tf), lambda i, f: (0, f)),
            pl.BlockSpec((tf, D_MODEL), lambda i, f: (f, 0)),
            _resident((1, D_MODEL)),
        ] + cast_in,
        out_specs=[pl.BlockSpec((tm, D_MODEL), row)] + cast_out,
        out_shape=[jax.ShapeDtypeStruct((st.rows, D_MODEL), F32)] + cast_shape,
        compiler_params=_params("parallel", "arbitrary"),
        name="mlp",
    )(h, x, mod, w1, w2, final_g, *[w for w, _ in casts])
    return out[0], out[1:]


def _rope_tables():
    n_rows = DEC_SEQ // GRID_W
    rows = jnp.broadcast_to(jnp.arange(n_rows)[:, None], (n_rows, GRID_W)).reshape(-1)
    cols = jnp.broadcast_to(jnp.arange(GRID_W)[None, :], (n_rows, GRID_W)).reshape(-1)
    nf = QK_ROPE // 4
    inv = 1.0 / (ROPE_THETA ** (jnp.arange(nf, dtype=F32) / nf))
    ang_r = rows.astype(F32)[:, None] * inv
    ang_c = cols.astype(F32)[:, None] * inv
    cr, sr, cc, sc = jnp.cos(ang_r), jnp.sin(ang_r), jnp.cos(ang_c), jnp.sin(ang_c)
    zeros = jnp.zeros((DEC_SEQ, LANE - QK_ROPE), F32)
    cos = jnp.concatenate([cr, cr, cc, cc, zeros], axis=1)
    sin = jnp.concatenate([-sr, sr, -sc, sc, zeros], axis=1)
    return cos, sin


def _swap_rope_partners(w):
    q = QK_ROPE // 4
    return jnp.concatenate([w[..., q:2 * q], w[..., :q], w[..., 3 * q:], w[..., 2 * q:3 * q]], axis=-1)


def _mla_weights(w_q_b, w_kv_a, w_kv_b):
    wq = w_q_b.reshape(Q_RANK, N_HEADS, QK_NOPE + QK_ROPE)
    pad = jnp.zeros((Q_RANK, N_HEADS, HEAD_PAD - QK_NOPE - QK_ROPE), w_q_b.dtype)
    wq_full = jnp.concatenate([wq, pad], axis=-1).reshape(Q_RANK, N_HEADS * HEAD_PAD)
    wq_swap = jnp.concatenate([_swap_rope_partners(wq[..., QK_NOPE:]), pad], axis=-1).reshape(Q_RANK, N_HEADS * LANE)
    kpe_w = w_kv_a[:, KV_RANK:]
    zpad = jnp.zeros((D_MODEL, LANE - QK_ROPE), w_kv_a.dtype)
    wkv = jnp.concatenate([w_kv_a[:, :KV_RANK], kpe_w, zpad, _swap_rope_partners(kpe_w), zpad], axis=1)
    wkvb = w_kv_b.reshape(KV_RANK, N_HEADS, QK_NOPE + V_DIM)
    wk = wkvb[..., :QK_NOPE].reshape(KV_RANK, N_HEADS * QK_NOPE)
    wvt = wkvb[..., QK_NOPE:].reshape(KV_RANK, N_HEADS * V_DIM).T
    return wq_full.astype(BF16), wq_swap.astype(BF16), wkv.astype(BF16), wk.astype(BF16), wvt.astype(BF16)


def kernel(x_prompt, x_sample, cache_ckv, cache_kpe, c, c_ctx, ada_w, ada_b, norm1, norm2, conv_w_in, conv_w, conv_w_out, gmlp_w_in, gmlp_g_v, gmlp_w_s, gmlp_b_s, gmlp_w_out, mla_w_q_a, mla_g_q, mla_w_q_b, mla_w_kv_a, mla_g_kv, mla_w_kv_b, mla_w_o, mlp_w1, mlp_w2, final_norm):
    D = D_MODEL
    streams = (PROMPT, SAMPLE)
    xs = [x_prompt.reshape(PROMPT.rows, D), x_sample.reshape(SAMPLE.rows, D)]
    cond = jnp.concatenate([c_ctx[None, :], c, jnp.zeros((N_GROUPS - 1 - DEC_BATCH, D), F32)], axis=0)
    mod = _modvec(cond, ada_w, ada_b)

    n1, n2 = norm1[:, None, :], norm2[:, None, :]
    final_g = final_norm[None, :]

    def big_weights(i):
        mix_in, mix_out = ((conv_w_in, conv_w_out), (gmlp_w_in, gmlp_w_out), (mla_w_q_a, mla_w_o))[i % N_MIXERS]
        return {"w1": (mlp_w1, i), "in": (mix_in, i // N_MIXERS), "w2": (mlp_w2, i), "out": (mix_out, i // N_MIXERS)}

    cast_split = (("w1", "in"), ("w2", "out"))
    bf = {name: w[l].astype(BF16) for name, (w, l) in big_weights(0).items()}

    cos_t, sin_t = _rope_tables()
    new_ckv, new_kpe = [], []
    for i in range(DEPTH):
        kind, j = i % N_MIXERS, i // N_MIXERS
        hs = [None, None]
        if kind == 0:
            for n, st in enumerate(streams):
                bg, p = _conv_in(st, xs[n], mod, n1, bf["in"], i)
                xs[n], hs[n] = _conv_out(st, bg, p, conv_w, xs[n], mod, bf["out"], n2, i, j)
        elif kind == 1:
            b_full = jnp.repeat(gmlp_b_s[j].T, LANE, axis=1)
            w_s = gmlp_w_s[j].astype(BF16)
            for n, st in enumerate(streams):
                z = _gmlp_in(st, xs[n], mod, n1, bf["in"], i)
                xs[n], hs[n] = _gmlp_out(st, z, gmlp_g_v[:, None, :], w_s, b_full, xs[n], mod, bf["out"], n2, i, j)
        else:
            wq, wqs, wkv, wk, wvt = _mla_weights(mla_w_q_b[j], mla_w_kv_a[j], mla_w_kv_b[j])
            proj = [_mla_proj(st, xs[n], mod, n1, bf["in"], mla_g_q[:, None, :], wq, wqs, wkv, mla_g_kv[:, None, :],
                              cos_t, sin_t, i, j) for n, st in enumerate(streams)]
            q, ckv, kpe, kpe_rot = proj[0]
            new_ckv.append(ckv.reshape(BATCH, SEQ, KV_RANK))
            new_kpe.append(kpe[:, :QK_ROPE].reshape(BATCH, SEQ, QK_ROPE))
            k_p, v_p = _kv_expand(ckv, kpe_rot, wk, wvt)
            o_p = _attention(q, k_p, v_p, BATCH, SEQ, SEQ, **ATTN_PROMPT)
            q, ckv, kpe, kpe_rot = proj[1]
            lk = PAST_LEN + DEC_SEQ
            ckv_s = jnp.concatenate([cache_ckv[:, j], ckv.reshape(DEC_BATCH, DEC_SEQ, KV_RANK)], axis=1)
            cache_pe = jnp.pad(cache_kpe[:, j], ((0, 0), (0, 0), (0, LANE - QK_ROPE))).astype(BF16)
            kpe_s = jnp.concatenate([cache_pe, kpe_rot.reshape(DEC_BATCH, DEC_SEQ, LANE)], axis=1)
            k_s, v_s = _kv_expand(ckv_s.reshape(DEC_BATCH * lk, KV_RANK), kpe_s.reshape(DEC_BATCH * lk, LANE), wk, wvt)
            o_s = _attention(q, k_s, v_s, DEC_BATCH, DEC_SEQ, lk, **ATTN_SAMPLE)
            for n, (st, o) in enumerate(zip(streams, (o_p, o_s))):
                xs[n], hs[n] = _attn_out(st, o, xs[n], mod, bf["out"], n2, i)
        nxt = big_weights(i + 1) if i + 1 < DEPTH else None
        bf_next = {}
        for n, st in enumerate(streams):
            names = cast_split[n] if nxt else ()
            xs[n], cast = _mlp(st, hs[n], xs[n], mod, bf["w1"], bf["w2"], final_g, i, final=(i == DEPTH - 1),
                               casts=[nxt[name] for name in names])
            bf_next.update(zip(names, cast))
        bf = bf_next
    y_prompt = xs[0].reshape(BATCH, SEQ, D)
    y_sample = xs[1].reshape(DEC_BATCH, DEC_SEQ, D)
    return (y_prompt, y_sample, jnp.stack(new_ckv, axis=1), jnp.stack(new_kpe, axis=1))
```

```python
import functools
from typing import NamedTuple

import jax
import jax.numpy as jnp
import numpy as np
from jax import lax
from jax.experimental import pallas as pl
from jax.experimental.pallas import tpu as pltpu

D_MODEL = 2048
BATCH = 32
SEQ = 256
DEPTH = 4
DEC_BATCH = 4
DEC_SEQ = 2048
PAST_LEN = 512
GRID_W = 64
N_MIXERS = 3
CONV_WIDTH = 3
CHUNK = 128
GMLP_GROUPS = 16
N_HEADS = 16
QK_NOPE = 128
QK_ROPE = 64
V_DIM = 128
Q_RANK = 512
KV_RANK = 512
ROPE_THETA = 10000.0
D_FF = 4 * D_MODEL
N_MOD = 6
EPS = 1e-6
LOG2_E = 1.4426950408889634

N_GROUPS = 8
HEAD_PAD = 256
LANE = 128
BF16_ROWS = 16
SUB = 256
VMEM_LIMIT = 56 * 1024 * 1024
ATTN_PROMPT = dict(heads=16, tq=256, ahead=16)
ATTN_SAMPLE = dict(heads=4, tq=1024, ahead=1)

BF16 = jnp.bfloat16
F32 = jnp.float32


class Stream(NamedTuple):
    rows: int
    seq: int
    group0: int
    group_rows: int
    rope: bool


PROMPT = Stream(BATCH * SEQ, SEQ, 0, BATCH * SEQ, False)
SAMPLE = Stream(DEC_BATCH * DEC_SEQ, DEC_SEQ, 1, DEC_SEQ, True)


def _params(*sem):
    return pltpu.CompilerParams(dimension_semantics=sem, vmem_limit_bytes=VMEM_LIMIT)


def _dot(a, b):
    return jnp.dot(a, b, preferred_element_type=F32)


def _resident(shape, layer=None):
    if layer is None:
        zeros = (0,) * len(shape)
        return pl.BlockSpec(shape, lambda *_: zeros, pipeline_mode=pl.Buffered(1))
    index = (layer,) + (0,) * len(shape)
    return pl.BlockSpec((None,) + tuple(shape), lambda *_: index, pipeline_mode=pl.Buffered(1))


def _mod_spec(st, layer, chunk, tm):
    def index(i, *_):
        return ((layer * N_MOD + chunk) * N_GROUPS + st.group0 + (i * tm) // st.group_rows, 0, 0)

    return pl.BlockSpec((1, 1, D_MODEL), index)


def _gain_shift(g_ref, sh_ref, sc_ref):
    return g_ref[...] * (1.0 + sc_ref[0]), sh_ref[0]


def _norm_mod(x, gsc, sh):
    ms = jnp.mean(x * x, axis=-1, keepdims=True)
    return (x * lax.rsqrt(ms + EPS) * gsc + sh).astype(BF16)


def _modvec_kernel(c_ref, w_ref, b_ref, o_ref):
    s = jax.nn.silu(c_ref[...]).astype(BF16)
    o_ref[0, 0] = _dot(s, w_ref[0].astype(BF16)) + b_ref[0]


def _modvec(cond, ada_w, ada_b, tn=1024):
    per_chunk = D_MODEL // tn
    out = pl.pallas_call(
        _modvec_kernel,
        grid=(DEPTH, N_MOD * per_chunk),
        in_specs=[
            pl.BlockSpec((N_GROUPS, D_MODEL), lambda l, j: (0, 0)),
            pl.BlockSpec((1, D_MODEL, tn), lambda l, j: (l, 0, j)),
            pl.BlockSpec((1, 1, tn), lambda l, j: (l, 0, j)),
        ],
        out_specs=pl.BlockSpec((1, 1, N_GROUPS, tn), lambda l, j: (l, j // per_chunk, 0, j % per_chunk)),
        out_shape=jax.ShapeDtypeStruct((DEPTH, N_MOD, N_GROUPS, D_MODEL), F32),
        compiler_params=_params("arbitrary", "arbitrary"),
        name="modvec",
    )(cond, ada_w, ada_b.reshape(DEPTH, 1, N_MOD * D_MODEL))
    return out.reshape(DEPTH * N_MOD * N_GROUPS, 1, D_MODEL)


def _conv_in_kernel(x_ref, g_ref, sh_ref, sc_ref, w_ref, bg_ref, p_ref, *, tm):
    gsc, sh = _gain_shift(g_ref, sh_ref, sc_ref)
    for s in range(tm // SUB):
        rs = slice(s * SUB, (s + 1) * SUB)
        z = _dot(_norm_mod(x_ref[rs, :], gsc, sh), w_ref[...])
        bg_ref[rs, :] = z[:, :D_MODEL].astype(BF16)
        p_ref[rs, :] = (z[:, D_MODEL:2 * D_MODEL] * z[:, 2 * D_MODEL:]).astype(BF16)


def _conv_in(st, x, mod, norm1, w_in, layer, tm=512):
    row = lambda i: (i, 0)
    return pl.pallas_call(
        functools.partial(_conv_in_kernel, tm=tm),
        grid=(st.rows // tm,),
        in_specs=[
            pl.BlockSpec((tm, D_MODEL), row),
            _resident((1, D_MODEL), layer),
            _mod_spec(st, layer, 0, tm),
            _mod_spec(st, layer, 1, tm),
            _resident((D_MODEL, 3 * D_MODEL)),
        ],
        out_specs=[pl.BlockSpec((tm, D_MODEL), row)] * 2,
        out_shape=[jax.ShapeDtypeStruct((st.rows, D_MODEL), BF16)] * 2,
        compiler_params=_params("parallel"),
        name="conv_in",
    )(x, norm1, mod, mod, w_in)


def _gmlp_in_kernel(x_ref, g_ref, sh_ref, sc_ref, w_ref, z_ref, *, tm):
    gsc, sh = _gain_shift(g_ref, sh_ref, sc_ref)
    for s in range(tm // SUB):
        rs = slice(s * SUB, (s + 1) * SUB)
        z_ref[rs, :] = _dot(_norm_mod(x_ref[rs, :], gsc, sh), w_ref[...]).astype(BF16)


def _gmlp_in(st, x, mod, norm1, w_in, layer, tm=512):
    n_out = w_in.shape[-1]
    row = lambda i: (i, 0)
    return pl.pallas_call(
        functools.partial(_gmlp_in_kernel, tm=tm),
        grid=(st.rows // tm,),
        in_specs=[
            pl.BlockSpec((tm, D_MODEL), row),
            _resident((1, D_MODEL), layer),
            _mod_spec(st, layer, 0, tm),
            _mod_spec(st, layer, 1, tm),
            _resident((D_MODEL, n_out)),
        ],
        out_specs=pl.BlockSpec((tm, n_out), row),
        out_shape=jax.ShapeDtypeStruct((st.rows, n_out), BF16),
        compiler_params=_params("parallel"),
        name="gmlp_in",
    )(x, norm1, mod, mod, w_in)


def _residual_proj(a, rs, x_ref, gate_ref, w_ref, gsc2, sh2, o_ref, h_ref):
    x1 = x_ref[rs, :] + gate_ref[0] * _dot(a, w_ref[...])
    o_ref[rs, :] = x1
    h_ref[rs, :] = _norm_mod(x1, gsc2, sh2)


def _proj_specs(st, layer, tm):
    row = lambda i: (i, 0)
    in_specs = [
        pl.BlockSpec((tm, D_MODEL), row),
        _mod_spec(st, layer, 2, tm),
        _resident((D_MODEL, D_MODEL)),
        _resident((1, D_MODEL), layer),
        _mod_spec(st, layer, 3, tm),
        _mod_spec(st, layer, 4, tm),
    ]
    out_specs = [pl.BlockSpec((tm, D_MODEL), row)] * 2
    out_shape = [jax.ShapeDtypeStruct((st.rows, D_MODEL), F32), jax.ShapeDtypeStruct((st.rows, D_MODEL), BF16)]
    return in_specs, out_specs, out_shape


def _conv_out_kernel(bg_ref, p_ref, lo_ref, hi_ref, cw_ref, x_ref, gate_ref, w_ref, g2_ref, sh2_ref, sc2_ref,
                     o_ref, h_ref, pbuf_ref, *, tm, seq):
    halo = BF16_ROWS
    n_sub = tm // SUB
    gsc2, sh2 = _gain_shift(g2_ref, sh2_ref, sc2_ref)
    base = pl.program_id(0) * tm
    width = 4 * LANE
    for s in range(n_sub):
        r0 = s * SUB
        before = lo_ref[...] if s == 0 else p_ref[r0 - halo:r0, :]
        after = hi_ref[...] if s == n_sub - 1 else p_ref[r0 + SUB:r0 + SUB + halo, :]
        pbuf_ref[s, 0:halo, :] = jnp.where((base + r0) % seq == 0, 0.0, before.astype(F32))
        pbuf_ref[s, halo:halo + SUB, :] = p_ref[r0:r0 + SUB, :].astype(F32)
        pbuf_ref[s, halo + SUB:, :] = jnp.where((base + r0 + SUB) % seq == 0, 0.0, after.astype(F32))
        parts = []
        for lo in range(0, D_MODEL, width):
            cols = slice(lo, lo + width)
            prev = pbuf_ref[s, halo - 1:halo - 1 + SUB, cols]
            cur = pbuf_ref[s, halo:halo + SUB, cols]
            nxt = pbuf_ref[s, halo + 1:halo + 1 + SUB, cols]
            conv = prev * cw_ref[0:1, cols] + cur * cw_ref[1:2, cols] + nxt * cw_ref[2:3, cols]
            parts.append((bg_ref[r0:r0 + SUB, cols].astype(F32) * conv).astype(BF16))
        _residual_proj(jnp.concatenate(parts, axis=1), slice(r0, r0 + SUB), x_ref, gate_ref, w_ref, gsc2, sh2,
                       o_ref, h_ref)


def _conv_out(st, bg, p, conv_w, x, mod, w_out, norm2, layer, j, tm=512):
    assert st.seq % SUB == 0 and tm % SUB == 0
    nb = tm // BF16_ROWS
    last_halo = st.rows // BF16_ROWS - 1
    row = lambda i: (i, 0)
    tail, out_specs, out_shape = _proj_specs(st, layer, tm)
    return pl.pallas_call(
        functools.partial(_conv_out_kernel, tm=tm, seq=st.seq),
        grid=(st.rows // tm,),
        in_specs=[
            pl.BlockSpec((tm, D_MODEL), row),
            pl.BlockSpec((tm, D_MODEL), row),
            pl.BlockSpec((BF16_ROWS, D_MODEL), lambda i: (jnp.maximum(i * nb - 1, 0), 0)),
            pl.BlockSpec((BF16_ROWS, D_MODEL), lambda i: (jnp.minimum((i + 1) * nb, last_halo), 0)),
            _resident((CONV_WIDTH, D_MODEL), j),
        ] + tail,
        out_specs=out_specs,
        out_shape=out_shape,
        scratch_shapes=[pltpu.VMEM((tm // SUB, SUB + 2 * BF16_ROWS, D_MODEL), F32)],
        compiler_params=_params("parallel"),
        name="conv_out",
    )(bg, p, p, p, conv_w, x, mod, w_out, norm2, mod, mod)


def _gmlp_out_kernel(zu_ref, zv_ref, gv_ref, ws_ref, bs_ref, x_ref, gate_ref, w_ref, g2_ref, sh2_ref, sc2_ref,
                     o_ref, h_ref, a_ref, vn_ref, *, tm):
    gv = gv_ref[...]
    n_chunks = tm // CHUNK
    for c in range(n_chunks):
        rs = slice(c * CHUNK, (c + 1) * CHUNK)
        v = zv_ref[rs, :].astype(F32)
        ms = jnp.mean(v * v, axis=-1, keepdims=True)
        vn_ref[rs, :] = (v * lax.rsqrt(ms + EPS) * gv).astype(BF16)

    for g in range(GMLP_GROUPS):
        cols = slice(g * LANE, (g + 1) * LANE)
        rhs = jnp.concatenate([vn_ref[c * CHUNK:(c + 1) * CHUNK, cols] for c in range(n_chunks)], axis=1)
        mixed = _dot(ws_ref[g], rhs)
        for c in range(n_chunks):
            rs = slice(c * CHUNK, (c + 1) * CHUNK)
            vv = mixed[:, c * CHUNK:(c + 1) * CHUNK] + bs_ref[:, cols]
            a_ref[rs, cols] = (zu_ref[rs, cols].astype(F32) * vv).astype(BF16)

    gsc2, sh2 = _gain_shift(g2_ref, sh2_ref, sc2_ref)
    for s in range(tm // SUB):
        rs = slice(s * SUB, (s + 1) * SUB)
        _residual_proj(a_ref[rs, :], rs, x_ref, gate_ref, w_ref, gsc2, sh2, o_ref, h_ref)


def _gmlp_out(st, z, g_v, w_s, b_full, x, mod, w_out, norm2, layer, j, tm=512):
    tail, out_specs, out_shape = _proj_specs(st, layer, tm)
    return pl.pallas_call(
        functools.partial(_gmlp_out_kernel, tm=tm),
        grid=(st.rows // tm,),
        in_specs=[
            pl.BlockSpec((tm, D_MODEL), lambda i: (i, 0)),
            pl.BlockSpec((tm, D_MODEL), lambda i: (i, 1)),
            _resident((1, D_MODEL), j),
            _resident((GMLP_GROUPS, CHUNK, CHUNK)),
            _resident((CHUNK, D_MODEL)),
        ] + tail,
        out_specs=out_specs,
        out_shape=out_shape,
        scratch_shapes=[pltpu.VMEM((tm, D_MODEL), BF16), pltpu.VMEM((tm, D_MODEL), BF16)],
        compiler_params=_params("parallel"),
        name="gmlp_out",
    )(z, z, g_v, w_s, b_full, x, mod, w_out, norm2, mod, mod)


def _attn_out_kernel(a_ref, x_ref, gate_ref, w_ref, g2_ref, sh2_ref, sc2_ref, o_ref, h_ref, *, tm):
    gsc2, sh2 = _gain_shift(g2_ref, sh2_ref, sc2_ref)
    for s in range(tm // SUB):
        rs = slice(s * SUB, (s + 1) * SUB)
        _residual_proj(a_ref[rs, :], rs, x_ref, gate_ref, w_ref, gsc2, sh2, o_ref, h_ref)


def _attn_out(st, a, x, mod, w_o, norm2, layer, tm=512):
    tail, out_specs, out_shape = _proj_specs(st, layer, tm)
    return pl.pallas_call(
        functools.partial(_attn_out_kernel, tm=tm),
        grid=(st.rows // tm,),
        in_specs=[pl.BlockSpec((tm, D_MODEL), lambda i: (i, 0))] + tail,
        out_specs=out_specs,
        out_shape=out_shape,
        compiler_params=_params("parallel"),
        name="attn_out",
    )(a, x, mod, w_o, norm2, mod, mod)


def _mla_proj_kernel(x_ref, g_ref, sh_ref, sc_ref, wqa_ref, gq_ref, wq_ref, wqs_ref, wkv_ref, gkv_ref,
                     cos_ref, sin_ref, q_ref, ckv_ref, kpe_ref, kper_ref, *, tm, rope):
    gsc, sh = _gain_shift(g_ref, sh_ref, sc_ref)
    gq = gq_ref[...] * ((QK_NOPE + QK_ROPE) ** -0.5 * LOG2_E)
    for s in range(tm // SUB):
        rs = slice(s * SUB, (s + 1) * SUB)
        h = _norm_mod(x_ref[rs, :], gsc, sh)
        qa = _dot(h, wqa_ref[...])
        qa = qa * lax.rsqrt(jnp.mean(qa * qa, axis=-1, keepdims=True) + EPS) * gq
        qa = qa.astype(BF16)
        q = _dot(qa, wq_ref[...])
        kv = _dot(h, wkv_ref[...])
        c = kv[:, :KV_RANK]
        ckv_ref[rs, :] = c * lax.rsqrt(jnp.mean(c * c, axis=-1, keepdims=True) + EPS) * gkv_ref[...]
        kpe = kv[:, KV_RANK:KV_RANK + LANE]
        kpe_ref[rs, :] = kpe
        if rope:
            cos = cos_ref[rs, :]
            sin = sin_ref[rs, :]
            qs = _dot(qa, wqs_ref[...])
            for hd in range(N_HEADS):
                lo = hd * HEAD_PAD
                q_ref[rs, lo:lo + LANE] = q[:, lo:lo + LANE].astype(BF16)
                rot = q[:, lo + LANE:lo + HEAD_PAD] * cos + qs[:, hd * LANE:(hd + 1) * LANE] * sin
                q_ref[rs, lo + LANE:lo + HEAD_PAD] = rot.astype(BF16)
            kper_ref[rs, :] = (kpe * cos + kv[:, KV_RANK + LANE:] * sin).astype(BF16)
        else:
            q_ref[rs, :] = q.astype(BF16)
            kper_ref[rs, :] = kpe.astype(BF16)


def _mla_proj(st, x, mod, norm1, wqa, gq, wq, wqs, wkv, gkv, cos_t, sin_t, layer, j, tm=512):
    per_seq = st.seq // tm
    rope_idx = (lambda i: (i % per_seq, 0)) if st.rope else (lambda i: (0, 0))
    row = lambda i: (i, 0)
    return pl.pallas_call(
        functools.partial(_mla_proj_kernel, tm=tm, rope=st.rope),
        grid=(st.rows // tm,),
        in_specs=[
            pl.BlockSpec((tm, D_MODEL), row),
            _resident((1, D_MODEL), layer),
            _mod_spec(st, layer, 0, tm),
            _mod_spec(st, layer, 1, tm),
            _resident(wqa.shape),
            _resident((1, Q_RANK), j),
            _resident(wq.shape),
            _resident(wqs.shape),
            _resident(wkv.shape),
            _resident((1, KV_RANK), j),
            pl.BlockSpec((tm, LANE), rope_idx),
            pl.BlockSpec((tm, LANE), rope_idx),
        ],
        out_specs=[
            pl.BlockSpec((tm, N_HEADS * HEAD_PAD), row),
            pl.BlockSpec((tm, KV_RANK), row),
            pl.BlockSpec((tm, LANE), row),
            pl.BlockSpec((tm, LANE), row),
        ],
        out_shape=[
            jax.ShapeDtypeStruct((st.rows, N_HEADS * HEAD_PAD), BF16),
            jax.ShapeDtypeStruct((st.rows, KV_RANK), F32),
            jax.ShapeDtypeStruct((st.rows, LANE), F32),
            jax.ShapeDtypeStruct((st.rows, LANE), BF16),
        ],
        compiler_params=_params("parallel"),
        name="mla_proj",
    )(x, norm1, mod, mod, wqa, gq, wq, wqs, wkv, gkv, cos_t, sin_t)


def _dot_nt(a, b):
    return lax.dot_general(a, b, (((1,), (1,)), ((), ())), preferred_element_type=F32)


def _kv_expand_kernel(c_ref, kpe_ref, wk_ref, wvt_ref, k_ref, vt_ref):
    c = c_ref[...].astype(BF16)
    kn = _dot(c, wk_ref[...])
    vt_ref[...] = _dot_nt(wvt_ref[...], c).astype(BF16)
    kpe = kpe_ref[...]
    for hd in range(N_HEADS):
        lo = hd * HEAD_PAD
        k_ref[:, lo:lo + LANE] = kn[:, hd * LANE:(hd + 1) * LANE].astype(BF16)
        k_ref[:, lo + LANE:lo + HEAD_PAD] = kpe


def _kv_expand(ckv, kpe, wk, wvt, tr=512):
    n_rows = ckv.shape[0]
    row = lambda i: (i, 0)
    return pl.pallas_call(
        _kv_expand_kernel,
        grid=(n_rows // tr,),
        in_specs=[
            pl.BlockSpec((tr, KV_RANK), row),
            pl.BlockSpec((tr, LANE), row),
            _resident(wk.shape),
            _resident(wvt.shape),
        ],
        out_specs=[
            pl.BlockSpec((tr, N_HEADS * HEAD_PAD), row),
            pl.BlockSpec((N_HEADS * V_DIM, tr), lambda i: (0, i)),
        ],
        out_shape=[
            jax.ShapeDtypeStruct((n_rows, N_HEADS * HEAD_PAD), BF16),
            jax.ShapeDtypeStruct((N_HEADS * V_DIM, n_rows), BF16),
        ],
        compiler_params=_params("parallel"),
        name="kv_expand",
    )(ckv, kpe, wk, wvt)


def _attn_kernel(q_ref, k_ref, vt_ref, o_ref, *, heads, ahead):
    ones = jnp.ones((BF16_ROWS, k_ref.shape[0]), BF16)

    def scores(hd):
        cols = slice(hd * HEAD_PAD, (hd + 1) * HEAD_PAD)
        return _dot_nt(k_ref[:, cols], q_ref[:, cols])

    pending = [scores(hd) for hd in range(min(ahead, heads))]
    for hd in range(heads):
        st = pending.pop(0)
        if hd + ahead < heads:
            pending.append(scores(hd + ahead))
        p = jnp.exp2(st - jnp.max(st, axis=0, keepdims=True)).astype(BF16)
        lhs = jnp.concatenate([vt_ref[hd * V_DIM:(hd + 1) * V_DIM, :], ones], axis=0)
        ot = _dot(lhs, p)
        o_ref[:, hd * V_DIM:(hd + 1) * V_DIM] = (ot[:V_DIM] / ot[V_DIM:V_DIM + 1]).T.astype(BF16)


def _attention(q, k, vt, n_batch, lq, lk, heads, tq, ahead):
    nq = lq // tq
    return pl.pallas_call(
        functools.partial(_attn_kernel, heads=heads, ahead=ahead),
        grid=(n_batch, N_HEADS // heads, nq),
        in_specs=[
            pl.BlockSpec((tq, heads * HEAD_PAD), lambda b, g, i: (b * nq + i, g)),
            pl.BlockSpec((lk, heads * HEAD_PAD), lambda b, g, i: (b, g)),
            pl.BlockSpec((heads * V_DIM, lk), lambda b, g, i: (g, b)),
        ],
        out_specs=pl.BlockSpec((tq, heads * V_DIM), lambda b, g, i: (b * nq + i, g)),
        out_shape=jax.ShapeDtypeStruct((n_batch * lq, N_HEADS * V_DIM), BF16),
        compiler_params=_params("parallel", "parallel", "arbitrary"),
        name="attention",
    )(q, k, vt)


def _mlp_kernel(h_ref, x_ref, gate_ref, w1_ref, w2_ref, gf_ref, *rest, tm, final, n_cast):
    src_refs, o_ref, dst_refs = rest[:n_cast], rest[n_cast], rest[n_cast + 1:]
    f = pl.program_id(1)

    @pl.when(f == 0)
    def _():
        o_ref[...] = x_ref[...]

    hid = _dot(h_ref[...], w1_ref[...])
    hid = jnp.square(jnp.maximum(hid, 0.0)).astype(BF16)
    o_ref[...] += gate_ref[0] * _dot(hid, w2_ref[...])

    for src_ref, dst_ref in zip(src_refs, dst_refs):
        dst_ref[...] = src_ref[...].astype(BF16)

    if final:
        @pl.when(f == pl.num_programs(1) - 1)
        def _():
            gf = gf_ref[...]
            for s in range(tm // SUB):
                rs = slice(s * SUB, (s + 1) * SUB)
                y = o_ref[rs, :]
                o_ref[rs, :] = y * lax.rsqrt(jnp.mean(y * y, axis=-1, keepdims=True) + EPS) * gf


def _mlp(st, h, x, mod, w1, w2, final_g, layer, final, casts=(), tm=512, tf=1024):
    row = lambda i, f: (i, 0)
    nf = D_FF // tf
    steps = (st.rows // tm) * nf
    slab = lambda i, f: (i * nf + f, 0)
    cast_in, cast_out, cast_shape = [], [], []
    for w, l in casts:
        _, r, c = w.shape
        cast_in.append(pl.BlockSpec((None, r // steps, c), lambda i, f, l=l: (l, i * nf + f, 0)))
        cast_out.append(pl.BlockSpec((r // steps, c), slab))
        cast_shape.append(jax.ShapeDtypeStruct((r, c), BF16))
    out = pl.pallas_call(
        functools.partial(_mlp_kernel, tm=tm, final=final, n_cast=len(casts)),
        grid=(st.rows // tm, nf),
        in_specs=[
            pl.BlockSpec((tm, D_MODEL), row),
            pl.BlockSpec((tm, D_MODEL), row),
            _mod_spec(st, layer, 5, tm),
            pl.BlockSpec((D_MODEL, tf), lambda i, f: (0, f)),
            pl.BlockSpec((tf, D_MODEL), lambda i, f: (f, 0)),
            _resident((1, D_MODEL)),
        ] + cast_in,
        out_specs=[pl.BlockSpec((tm, D_MODEL), row)] + cast_out,
        out_shape=[jax.ShapeDtypeStruct((st.rows, D_MODEL), F32)] + cast_shape,
        compiler_params=_params("parallel", "arbitrary"),
        name="mlp",
    )(h, x, mod, w1, w2, final_g, *[w for w, _ in casts])
    return out[0], out[1:]


def _rope_tables():
    t = np.arange(DEC_SEQ)
    nf = QK_ROPE // 4
    inv = (1.0 / (np.float32(ROPE_THETA) ** (np.arange(nf, dtype=np.float32) / np.float32(nf)))).astype(np.float32)
    ang_r = (t // GRID_W).astype(np.float32)[:, None] * inv
    ang_c = (t % GRID_W).astype(np.float32)[:, None] * inv
    cr, sr, cc, sc = np.cos(ang_r), np.sin(ang_r), np.cos(ang_c), np.sin(ang_c)
    zeros = np.zeros((DEC_SEQ, LANE - QK_ROPE), np.float32)
    cos = np.concatenate([cr, cr, cc, cc, zeros], axis=1)
    sin = np.concatenate([-sr, sr, -sc, sc, zeros], axis=1)
    return jnp.asarray(cos, F32), jnp.asarray(sin, F32)


def _swap_rope_partners(w):
    q = QK_ROPE // 4
    return jnp.concatenate([w[..., q:2 * q], w[..., :q], w[..., 3 * q:], w[..., 2 * q:3 * q]], axis=-1)


def _mla_weights(w_q_b, w_kv_a, w_kv_b):
    wq = w_q_b.reshape(Q_RANK, N_HEADS, QK_NOPE + QK_ROPE)
    pad = jnp.zeros((Q_RANK, N_HEADS, HEAD_PAD - QK_NOPE - QK_ROPE), w_q_b.dtype)
    wq_full = jnp.concatenate([wq, pad], axis=-1).reshape(Q_RANK, N_HEADS * HEAD_PAD)
    wq_swap = jnp.concatenate([_swap_rope_partners(wq[..., QK_NOPE:]), pad], axis=-1).reshape(Q_RANK, N_HEADS * LANE)
    kpe_w = w_kv_a[:, KV_RANK:]
    zpad = jnp.zeros((D_MODEL, LANE - QK_ROPE), w_kv_a.dtype)
    wkv = jnp.concatenate([w_kv_a[:, :KV_RANK], kpe_w, zpad, _swap_rope_partners(kpe_w), zpad], axis=1)
    wkvb = w_kv_b.reshape(KV_RANK, N_HEADS, QK_NOPE + V_DIM)
    wk = wkvb[..., :QK_NOPE].reshape(KV_RANK, N_HEADS * QK_NOPE)
    wvt = wkvb[..., QK_NOPE:].reshape(KV_RANK, N_HEADS * V_DIM).T
    return wq_full.astype(BF16), wq_swap.astype(BF16), wkv.astype(BF16), wk.astype(BF16), wvt.astype(BF16)


def kernel(x_prompt, x_sample, cache_ckv, cache_kpe, c, c_ctx, ada_w, ada_b, norm1, norm2, conv_w_in, conv_w, conv_w_out, gmlp_w_in, gmlp_g_v, gmlp_w_s, gmlp_b_s, gmlp_w_out, mla_w_q_a, mla_g_q, mla_w_q_b, mla_w_kv_a, mla_g_kv, mla_w_kv_b, mla_w_o, mlp_w1, mlp_w2, final_norm):
    D = D_MODEL
    streams = (PROMPT, SAMPLE)
    xs = [x_prompt.reshape(PROMPT.rows, D), x_sample.reshape(SAMPLE.rows, D)]
    cond = jnp.concatenate([c_ctx[None, :], c, jnp.zeros((N_GROUPS - 1 - DEC_BATCH, D), F32)], axis=0)
    mod = _modvec(cond, ada_w, ada_b)

    n1, n2 = norm1[:, None, :], norm2[:, None, :]
    final_g = final_norm[None, :]

    def big_weights(i):
        mix_in, mix_out = ((conv_w_in, conv_w_out), (gmlp_w_in, gmlp_w_out), (mla_w_q_a, mla_w_o))[i % N_MIXERS]
        return {"w1": (mlp_w1, i), "in": (mix_in, i // N_MIXERS), "w2": (mlp_w2, i), "out": (mix_out, i // N_MIXERS)}

    cast_split = (("w1", "in"), ("w2", "out"))
    bf = {name: w[l].astype(BF16) for name, (w, l) in big_weights(0).items()}

    cos_t, sin_t = _rope_tables()
    new_ckv, new_kpe = [], []
    for i in range(DEPTH):
        kind, j = i % N_MIXERS, i // N_MIXERS
        hs = [None, None]
        if kind == 0:
            for n, st in enumerate(streams):
                bg, p = _conv_in(st, xs[n], mod, n1, bf["in"], i)
                xs[n], hs[n] = _conv_out(st, bg, p, conv_w, xs[n], mod, bf["out"], n2, i, j)
        elif kind == 1:
            b_full = jnp.repeat(gmlp_b_s[j].T, LANE, axis=1)
            w_s = gmlp_w_s[j].astype(BF16)
            for n, st in enumerate(streams):
                z = _gmlp_in(st, xs[n], mod, n1, bf["in"], i)
                xs[n], hs[n] = _gmlp_out(st, z, gmlp_g_v[:, None, :], w_s, b_full, xs[n], mod, bf["out"], n2, i, j)
        else:
            wq, wqs, wkv, wk, wvt = _mla_weights(mla_w_q_b[j], mla_w_kv_a[j], mla_w_kv_b[j])
            proj = [_mla_proj(st, xs[n], mod, n1, bf["in"], mla_g_q[:, None, :], wq, wqs, wkv, mla_g_kv[:, None, :],
                              cos_t, sin_t, i, j) for n, st in enumerate(streams)]
            q, ckv, kpe, kpe_rot = proj[0]
            new_ckv.append(ckv.reshape(BATCH, SEQ, KV_RANK))
            new_kpe.append(kpe[:, :QK_ROPE].reshape(BATCH, SEQ, QK_ROPE))
            k_p, v_p = _kv_expand(ckv, kpe_rot, wk, wvt)
            o_p = _attention(q, k_p, v_p, BATCH, SEQ, SEQ, **ATTN_PROMPT)
            q, ckv, kpe, kpe_rot = proj[1]
            lk = PAST_LEN + DEC_SEQ
            ckv_s = jnp.concatenate([cache_ckv[:, j], ckv.reshape(DEC_BATCH, DEC_SEQ, KV_RANK)], axis=1)
            cache_pe = jnp.pad(cache_kpe[:, j], ((0, 0), (0, 0), (0, LANE - QK_ROPE))).astype(BF16)
            kpe_s = jnp.concatenate([cache_pe, kpe_rot.reshape(DEC_BATCH, DEC_SEQ, LANE)], axis=1)
            k_s, v_s = _kv_expand(ckv_s.reshape(DEC_BATCH * lk, KV_RANK), kpe_s.reshape(DEC_BATCH * lk, LANE), wk, wvt)
            o_s = _attention(q, k_s, v_s, DEC_BATCH, DEC_SEQ, lk, **ATTN_SAMPLE)
            for n, (st, o) in enumerate(zip(streams, (o_p, o_s))):
                xs[n], hs[n] = _attn_out(st, o, xs[n], mod, bf["out"], n2, i)
        nxt = big_weights(i + 1) if i + 1 < DEPTH else None
        bf_next = {}
        for n, st in enumerate(streams):
            names = cast_split[n] if nxt else ()
            xs[n], cast = _mlp(st, hs[n], xs[n], mod, bf["w1"], bf["w2"], final_g, i, final=(i == DEPTH - 1),
                               casts=[nxt[name] for name in names])
            bf_next.update(zip(names, cast))
        bf = bf_next
    y_prompt = xs[0].reshape(BATCH, SEQ, D)
    y_sample = xs[1].reshape(DEC_BATCH, DEC_SEQ, D)
    return (y_prompt, y_sample, jnp.stack(new_ckv, axis=1), jnp.stack(new_kpe, axis=1))
```

```python
import functools
from typing import NamedTuple

import jax
import jax.numpy as jnp
import numpy as np
from jax import lax
from jax.experimental import pallas as pl
from jax.experimental.pallas import tpu as pltpu

D_MODEL = 2048
BATCH = 32
SEQ = 256
DEPTH = 4
DEC_BATCH = 4
DEC_SEQ = 2048
PAST_LEN = 512
GRID_W = 64
N_MIXERS = 3
CONV_WIDTH = 3
CHUNK = 128
GMLP_GROUPS = 16
N_HEADS = 16
QK_NOPE = 128
QK_ROPE = 64
V_DIM = 128
Q_RANK = 512
KV_RANK = 512
ROPE_THETA = 10000.0
D_FF = 4 * D_MODEL
N_MOD = 6
EPS = 1e-6
LOG2_E = 1.4426950408889634

N_GROUPS = 8
HEAD_PAD = 256
LANE = 128
BF16_ROWS = 16
SUB = 256
VMEM_LIMIT = 56 * 1024 * 1024
ATTN_PROMPT = dict(heads=16, tq=256, ahead=16)
ATTN_SAMPLE = dict(heads=4, tq=1024, ahead=1)

BF16 = jnp.bfloat16
F32 = jnp.float32


class Stream(NamedTuple):
    rows: int
    seq: int
    group0: int
    group_rows: int
    rope: bool


PROMPT = Stream(BATCH * SEQ, SEQ, 0, BATCH * SEQ, False)
SAMPLE = Stream(DEC_BATCH * DEC_SEQ, DEC_SEQ, 1, DEC_SEQ, True)


def _params(*sem):
    return pltpu.CompilerParams(dimension_semantics=sem, vmem_limit_bytes=VMEM_LIMIT)


def _dot(a, b):
    return jnp.dot(a, b, preferred_element_type=F32)


def _resident(shape, layer=None):
    if layer is None:
        zeros = (0,) * len(shape)
        return pl.BlockSpec(shape, lambda *_: zeros, pipeline_mode=pl.Buffered(1))
    index = (layer,) + (0,) * len(shape)
    return pl.BlockSpec((None,) + tuple(shape), lambda *_: index, pipeline_mode=pl.Buffered(1))


def _mod_spec(st, layer, chunk, tm):
    def index(i, *_):
        return ((layer * N_MOD + chunk) * N_GROUPS + st.group0 + (i * tm) // st.group_rows, 0, 0)

    return pl.BlockSpec((1, 1, D_MODEL), index)


def _gain_shift(g_ref, sh_ref, sc_ref):
    return g_ref[...] * (1.0 + sc_ref[0]), sh_ref[0]


def _norm_mod(x, gsc, sh):
    ms = jnp.mean(x * x, axis=-1, keepdims=True)
    return (x * lax.rsqrt(ms + EPS) * gsc + sh).astype(BF16)


def _modvec_kernel(c_ref, w_ref, b_ref, o_ref):
    s = jax.nn.silu(c_ref[...]).astype(BF16)
    o_ref[0, 0] = _dot(s, w_ref[0].astype(BF16)) + b_ref[0]


def _modvec(cond, ada_w, ada_b, tn=1024):
    per_chunk = D_MODEL // tn
    out = pl.pallas_call(
        _modvec_kernel,
        grid=(DEPTH, N_MOD * per_chunk),
        in_specs=[
            pl.BlockSpec((N_GROUPS, D_MODEL), lambda l, j: (0, 0)),
            pl.BlockSpec((1, D_MODEL, tn), lambda l, j: (l, 0, j)),
            pl.BlockSpec((1, 1, tn), lambda l, j: (l, 0, j)),
        ],
        out_specs=pl.BlockSpec((1, 1, N_GROUPS, tn), lambda l, j: (l, j // per_chunk, 0, j % per_chunk)),
        out_shape=jax.ShapeDtypeStruct((DEPTH, N_MOD, N_GROUPS, D_MODEL), F32),
        compiler_params=_params("arbitrary", "arbitrary"),
        name="modvec",
    )(cond, ada_w, ada_b.reshape(DEPTH, 1, N_MOD * D_MODEL))
    return out.reshape(DEPTH * N_MOD * N_GROUPS, 1, D_MODEL)


class HalfCast(NamedTuple):
    w: jax.Array
    layer: int
    part: int
    prev: jax.Array | None


def _half_cast_specs(cast, steps):
    _, r, c = cast.w.shape
    rows, off = r // (2 * steps), cast.part * steps
    in_specs = [pl.BlockSpec((None, rows, c), lambda i: (cast.layer, off + i, 0))]
    args = [cast.w]
    if cast.prev is not None:
        in_specs.append(pl.BlockSpec(memory_space=pl.ANY))
        args.append(cast.prev)
    return in_specs, args, pl.BlockSpec((rows, c), lambda i: (off + i, 0)), jax.ShapeDtypeStruct((r, c), BF16)


def _conv_in_kernel(x_ref, g_ref, sh_ref, sc_ref, w_ref, *rest, tm, cast):
    bg_ref, p_ref = rest[-3:-1] if cast else rest
    gsc, sh = _gain_shift(g_ref, sh_ref, sc_ref)
    for s in range(tm // SUB):
        rs = slice(s * SUB, (s + 1) * SUB)
        z = _dot(_norm_mod(x_ref[rs, :], gsc, sh), w_ref[...])
        bg_ref[rs, :] = z[:, :D_MODEL].astype(BF16)
        p_ref[rs, :] = (z[:, D_MODEL:2 * D_MODEL] * z[:, 2 * D_MODEL:]).astype(BF16)
    if cast:
        rest[-1][...] = rest[0][...].astype(BF16)


def _conv_in(st, x, mod, norm1, w_in, layer, cast=None, tm=512):
    row = lambda i: (i, 0)
    in_specs = [
        pl.BlockSpec((tm, D_MODEL), row),
        _resident((1, D_MODEL), layer),
        _mod_spec(st, layer, 0, tm),
        _mod_spec(st, layer, 1, tm),
        _resident((D_MODEL, 3 * D_MODEL)),
    ]
    args = [x, norm1, mod, mod, w_in]
    out_specs = [pl.BlockSpec((tm, D_MODEL), row)] * 2
    out_shape = [jax.ShapeDtypeStruct((st.rows, D_MODEL), BF16)] * 2
    aliases = {}
    if cast is not None:
        c_in, c_args, c_out, c_shape = _half_cast_specs(cast, st.rows // tm)
        if cast.prev is not None:
            aliases = {len(in_specs) + 1: len(out_specs)}
        in_specs, args, out_specs, out_shape = in_specs + c_in, args + c_args, out_specs + [c_out], out_shape + [c_shape]
    return pl.pallas_call(
        functools.partial(_conv_in_kernel, tm=tm, cast=cast is not None),
        grid=(st.rows // tm,),
        in_specs=in_specs,
        out_specs=out_specs,
        out_shape=out_shape,
        input_output_aliases=aliases,
        compiler_params=_params("parallel"),
        name="conv_in",
    )(*args)


def _gmlp_in_kernel(x_ref, g_ref, sh_ref, sc_ref, w_ref, z_ref, *, tm):
    gsc, sh = _gain_shift(g_ref, sh_ref, sc_ref)
    for s in range(tm // SUB):
        rs = slice(s * SUB, (s + 1) * SUB)
        z_ref[rs, :] = _dot(_norm_mod(x_ref[rs, :], gsc, sh), w_ref[...]).astype(BF16)


def _gmlp_in(st, x, mod, norm1, w_in, layer, tm=512):
    n_out = w_in.shape[-1]
    row = lambda i: (i, 0)
    return pl.pallas_call(
        functools.partial(_gmlp_in_kernel, tm=tm),
        grid=(st.rows // tm,),
        in_specs=[
            pl.BlockSpec((tm, D_MODEL), row),
            _resident((1, D_MODEL), layer),
            _mod_spec(st, layer, 0, tm),
            _mod_spec(st, layer, 1, tm),
            _resident((D_MODEL, n_out)),
        ],
        out_specs=pl.BlockSpec((tm, n_out), row),
        out_shape=jax.ShapeDtypeStruct((st.rows, n_out), BF16),
        compiler_params=_params("parallel"),
        name="gmlp_in",
    )(x, norm1, mod, mod, w_in)


def _residual_proj(a, rs, x_ref, gate_ref, w_ref, gsc2, sh2, o_ref, h_ref):
    x1 = x_ref[rs, :] + gate_ref[0] * _dot(a, w_ref[...])
    o_ref[rs, :] = x1
    h_ref[rs, :] = _norm_mod(x1, gsc2, sh2)


def _proj_specs(st, layer, tm):
    row = lambda i: (i, 0)
    in_specs = [
        pl.BlockSpec((tm, D_MODEL), row),
        _mod_spec(st, layer, 2, tm),
        _resident((D_MODEL, D_MODEL)),
        _resident((1, D_MODEL), layer),
        _mod_spec(st, layer, 3, tm),
        _mod_spec(st, layer, 4, tm),
    ]
    out_specs = [pl.BlockSpec((tm, D_MODEL), row)] * 2
    out_shape = [jax.ShapeDtypeStruct((st.rows, D_MODEL), F32), jax.ShapeDtypeStruct((st.rows, D_MODEL), BF16)]
    return in_specs, out_specs, out_shape


def _conv_out_kernel(bg_ref, p_ref, lo_ref, hi_ref, cw_ref, x_ref, gate_ref, w_ref, g2_ref, sh2_ref, sc2_ref,
                     *rest, tm, seq, cast):
    if cast:
        o_ref, h_ref, dst_ref, pbuf_ref = rest[-4:]
        dst_ref[...] = rest[0][...].astype(BF16)
    else:
        o_ref, h_ref, pbuf_ref = rest
    halo = BF16_ROWS
    n_sub = tm // SUB
    gsc2, sh2 = _gain_shift(g2_ref, sh2_ref, sc2_ref)
    base = pl.program_id(0) * tm
    width = 4 * LANE
    for s in range(n_sub):
        r0 = s * SUB
        before = lo_ref[...] if s == 0 else p_ref[r0 - halo:r0, :]
        after = hi_ref[...] if s == n_sub - 1 else p_ref[r0 + SUB:r0 + SUB + halo, :]
        pbuf_ref[s, 0:halo, :] = jnp.where((base + r0) % seq == 0, 0.0, before.astype(F32))
        pbuf_ref[s, halo:halo + SUB, :] = p_ref[r0:r0 + SUB, :].astype(F32)
        pbuf_ref[s, halo + SUB:, :] = jnp.where((base + r0 + SUB) % seq == 0, 0.0, after.astype(F32))
        parts = []
        for lo in range(0, D_MODEL, width):
            cols = slice(lo, lo + width)
            prev = pbuf_ref[s, halo - 1:halo - 1 + SUB, cols]
            cur = pbuf_ref[s, halo:halo + SUB, cols]
            nxt = pbuf_ref[s, halo + 1:halo + 1 + SUB, cols]
            conv = prev * cw_ref[0:1, cols] + cur * cw_ref[1:2, cols] + nxt * cw_ref[2:3, cols]
            parts.append((bg_ref[r0:r0 + SUB, cols].astype(F32) * conv).astype(BF16))
        _residual_proj(jnp.concatenate(parts, axis=1), slice(r0, r0 + SUB), x_ref, gate_ref, w_ref, gsc2, sh2,
                       o_ref, h_ref)


def _conv_out(st, bg, p, conv_w, x, mod, w_out, norm2, layer, j, cast=None, tm=512):
    assert st.seq % SUB == 0 and tm % SUB == 0
    nb = tm // BF16_ROWS
    last_halo = st.rows // BF16_ROWS - 1
    row = lambda i: (i, 0)
    tail, out_specs, out_shape = _proj_specs(st, layer, tm)
    in_specs = [
        pl.BlockSpec((tm, D_MODEL), row),
        pl.BlockSpec((tm, D_MODEL), row),
        pl.BlockSpec((BF16_ROWS, D_MODEL), lambda i: (jnp.maximum(i * nb - 1, 0), 0)),
        pl.BlockSpec((BF16_ROWS, D_MODEL), lambda i: (jnp.minimum((i + 1) * nb, last_halo), 0)),
        _resident((CONV_WIDTH, D_MODEL), j),
    ] + tail
    args = [bg, p, p, p, conv_w, x, mod, w_out, norm2, mod, mod]
    aliases = {}
    if cast is not None:
        c_in, c_args, c_out, c_shape = _half_cast_specs(cast, st.rows // tm)
        if cast.prev is not None:
            aliases = {len(in_specs) + 1: len(out_specs)}
        in_specs, args, out_specs, out_shape = in_specs + c_in, args + c_args, out_specs + [c_out], out_shape + [c_shape]
    return pl.pallas_call(
        functools.partial(_conv_out_kernel, tm=tm, seq=st.seq, cast=cast is not None),
        grid=(st.rows // tm,),
        in_specs=in_specs,
        out_specs=out_specs,
        out_shape=out_shape,
        scratch_shapes=[pltpu.VMEM((tm // SUB, SUB + 2 * BF16_ROWS, D_MODEL), F32)],
        input_output_aliases=aliases,
        compiler_params=_params("parallel"),
        name="conv_out",
    )(*args)


def _gmlp_out_kernel(zu_ref, zv_ref, gv_ref, ws_ref, bs_ref, x_ref, gate_ref, w_ref, g2_ref, sh2_ref, sc2_ref,
                     o_ref, h_ref, a_ref, vn_ref, *, tm):
    gv = gv_ref[...]
    n_chunks = tm // CHUNK
    for c in range(n_chunks):
        rs = slice(c * CHUNK, (c + 1) * CHUNK)
        v = zv_ref[rs, :].astype(F32)
        ms = jnp.mean(v * v, axis=-1, keepdims=True)
        vn_ref[rs, :] = (v * lax.rsqrt(ms + EPS) * gv).astype(BF16)

    for g in range(GMLP_GROUPS):
        cols = slice(g * LANE, (g + 1) * LANE)
        rhs = jnp.concatenate([vn_ref[c * CHUNK:(c + 1) * CHUNK, cols] for c in range(n_chunks)], axis=1)
        mixed = _dot(ws_ref[g], rhs)
        for c in range(n_chunks):
            rs = slice(c * CHUNK, (c + 1) * CHUNK)
            vv = mixed[:, c * CHUNK:(c + 1) * CHUNK] + bs_ref[:, cols]
            a_ref[rs, cols] = (zu_ref[rs, cols].astype(F32) * vv).astype(BF16)

    gsc2, sh2 = _gain_shift(g2_ref, sh2_ref, sc2_ref)
    for s in range(tm // SUB):
        rs = slice(s * SUB, (s + 1) * SUB)
        _residual_proj(a_ref[rs, :], rs, x_ref, gate_ref, w_ref, gsc2, sh2, o_ref, h_ref)


def _gmlp_out(st, z, g_v, w_s, b_full, x, mod, w_out, norm2, layer, j, tm=512):
    tail, out_specs, out_shape = _proj_specs(st, layer, tm)
    return pl.pallas_call(
        functools.partial(_gmlp_out_kernel, tm=tm),
        grid=(st.rows // tm,),
        in_specs=[
            pl.BlockSpec((tm, D_MODEL), lambda i: (i, 0)),
            pl.BlockSpec((tm, D_MODEL), lambda i: (i, 1)),
            _resident((1, D_MODEL), j),
            _resident((GMLP_GROUPS, CHUNK, CHUNK)),
            _resident((CHUNK, D_MODEL)),
        ] + tail,
        out_specs=out_specs,
        out_shape=out_shape,
        scratch_shapes=[pltpu.VMEM((tm, D_MODEL), BF16), pltpu.VMEM((tm, D_MODEL), BF16)],
        compiler_params=_params("parallel"),
        name="gmlp_out",
    )(z, z, g_v, w_s, b_full, x, mod, w_out, norm2, mod, mod)


def _attn_out_kernel(a_ref, x_ref, gate_ref, w_ref, g2_ref, sh2_ref, sc2_ref, o_ref, h_ref, *, tm):
    gsc2, sh2 = _gain_shift(g2_ref, sh2_ref, sc2_ref)
    for s in range(tm // SUB):
        rs = slice(s * SUB, (s + 1) * SUB)
        _residual_proj(a_ref[rs, :], rs, x_ref, gate_ref, w_ref, gsc2, sh2, o_ref, h_ref)


def _attn_out(st, a, x, mod, w_o, norm2, layer, tm=512):
    tail, out_specs, out_shape = _proj_specs(st, layer, tm)
    return pl.pallas_call(
        functools.partial(_attn_out_kernel, tm=tm),
        grid=(st.rows // tm,),
        in_specs=[pl.BlockSpec((tm, D_MODEL), lambda i: (i, 0))] + tail,
        out_specs=out_specs,
        out_shape=out_shape,
        compiler_params=_params("parallel"),
        name="attn_out",
    )(a, x, mod, w_o, norm2, mod, mod)


def _mla_proj_kernel(x_ref, g_ref, sh_ref, sc_ref, wqa_ref, gq_ref, wq_ref, wqs_ref, wkv_ref, gkv_ref,
                     cos_ref, sin_ref, q_ref, ckv_ref, kpe_ref, kper_ref, *, tm, rope):
    gsc, sh = _gain_shift(g_ref, sh_ref, sc_ref)
    gq = gq_ref[...] * ((QK_NOPE + QK_ROPE) ** -0.5 * LOG2_E)
    for s in range(tm // SUB):
        rs = slice(s * SUB, (s + 1) * SUB)
        h = _norm_mod(x_ref[rs, :], gsc, sh)
        qa = _dot(h, wqa_ref[...])
        qa = qa * lax.rsqrt(jnp.mean(qa * qa, axis=-1, keepdims=True) + EPS) * gq
        qa = qa.astype(BF16)
        q = _dot(qa, wq_ref[...])
        kv = _dot(h, wkv_ref[...])
        c = kv[:, :KV_RANK]
        ckv_ref[rs, :] = c * lax.rsqrt(jnp.mean(c * c, axis=-1, keepdims=True) + EPS) * gkv_ref[...]
        kpe = kv[:, KV_RANK:KV_RANK + LANE]
        kpe_ref[rs, :] = kpe
        if rope:
            cos = cos_ref[rs, :]
            sin = sin_ref[rs, :]
            qs = _dot(qa, wqs_ref[...])
            for hd in range(N_HEADS):
                lo = hd * HEAD_PAD
                q_ref[rs, lo:lo + LANE] = q[:, lo:lo + LANE].astype(BF16)
                rot = q[:, lo + LANE:lo + HEAD_PAD] * cos + qs[:, hd * LANE:(hd + 1) * LANE] * sin
                q_ref[rs, lo + LANE:lo + HEAD_PAD] = rot.astype(BF16)
            kper_ref[rs, :] = (kpe * cos + kv[:, KV_RANK + LANE:] * sin).astype(BF16)
        else:
            q_ref[rs, :] = q.astype(BF16)
            kper_ref[rs, :] = kpe.astype(BF16)


def _mla_proj(st, x, mod, norm1, wqa, gq, wq, wqs, wkv, gkv, cos_t, sin_t, layer, j, tm=512):
    per_seq = st.seq // tm
    rope_idx = (lambda i: (i % per_seq, 0)) if st.rope else (lambda i: (0, 0))
    row = lambda i: (i, 0)
    return pl.pallas_call(
        functools.partial(_mla_proj_kernel, tm=tm, rope=st.rope),
        grid=(st.rows // tm,),
        in_specs=[
            pl.BlockSpec((tm, D_MODEL), row),
            _resident((1, D_MODEL), layer),
            _mod_spec(st, layer, 0, tm),
            _mod_spec(st, layer, 1, tm),
            _resident(wqa.shape),
            _resident((1, Q_RANK), j),
            _resident(wq.shape),
            _resident(wqs.shape),
            _resident(wkv.shape),
            _resident((1, KV_RANK), j),
            pl.BlockSpec((tm, LANE), rope_idx),
            pl.BlockSpec((tm, LANE), rope_idx),
        ],
        out_specs=[
            pl.BlockSpec((tm, N_HEADS * HEAD_PAD), row),
            pl.BlockSpec((tm, KV_RANK), row),
            pl.BlockSpec((tm, LANE), row),
            pl.BlockSpec((tm, LANE), row),
        ],
        out_shape=[
            jax.ShapeDtypeStruct((st.rows, N_HEADS * HEAD_PAD), BF16),
            jax.ShapeDtypeStruct((st.rows, KV_RANK), F32),
            jax.ShapeDtypeStruct((st.rows, LANE), F32),
            jax.ShapeDtypeStruct((st.rows, LANE), BF16),
        ],
        compiler_params=_params("parallel"),
        name="mla_proj",
    )(x, norm1, mod, mod, wqa, gq, wq, wqs, wkv, gkv, cos_t, sin_t)


def _dot_nt(a, b):
    return lax.dot_general(a, b, (((1,), (1,)), ((), ())), preferred_element_type=F32)


def _kv_expand_kernel(c_ref, kpe_ref, wk_ref, wvt_ref, k_ref, vt_ref):
    c = c_ref[...].astype(BF16)
    kn = _dot(c, wk_ref[...])
    vt_ref[...] = _dot_nt(wvt_ref[...], c).astype(BF16)
    kpe = kpe_ref[...]
    for hd in range(N_HEADS):
        lo = hd * HEAD_PAD
        k_ref[:, lo:lo + LANE] = kn[:, hd * LANE:(hd + 1) * LANE].astype(BF16)
        k_ref[:, lo + LANE:lo + HEAD_PAD] = kpe


def _kv_expand(ckv, kpe, wk, wvt, tr=512):
    n_rows = ckv.shape[0]
    row = lambda i: (i, 0)
    return pl.pallas_call(
        _kv_expand_kernel,
        grid=(n_rows // tr,),
        in_specs=[
            pl.BlockSpec((tr, KV_RANK), row),
            pl.BlockSpec((tr, LANE), row),
            _resident(wk.shape),
            _resident(wvt.shape),
        ],
        out_specs=[
            pl.BlockSpec((tr, N_HEADS * HEAD_PAD), row),
            pl.BlockSpec((N_HEADS * V_DIM, tr), lambda i: (0, i)),
        ],
        out_shape=[
            jax.ShapeDtypeStruct((n_rows, N_HEADS * HEAD_PAD), BF16),
            jax.ShapeDtypeStruct((N_HEADS * V_DIM, n_rows), BF16),
        ],
        compiler_params=_params("parallel"),
        name="kv_expand",
    )(ckv, kpe, wk, wvt)


def _attn_kernel(q_ref, k_ref, vt_ref, o_ref, *, heads, ahead):
    ones = jnp.ones((BF16_ROWS, k_ref.shape[0]), BF16)

    def scores(hd):
        cols = slice(hd * HEAD_PAD, (hd + 1) * HEAD_PAD)
        return _dot_nt(k_ref[:, cols], q_ref[:, cols])

    pending = [scores(hd) for hd in range(min(ahead, heads))]
    for hd in range(heads):
        st = pending.pop(0)
        if hd + ahead < heads:
            pending.append(scores(hd + ahead))
        p = jnp.exp2(st - jnp.max(st, axis=0, keepdims=True)).astype(BF16)
        lhs = jnp.concatenate([vt_ref[hd * V_DIM:(hd + 1) * V_DIM, :], ones], axis=0)
        ot = _dot(lhs, p)
        o_ref[:, hd * V_DIM:(hd + 1) * V_DIM] = (ot[:V_DIM] / ot[V_DIM:V_DIM + 1]).T.astype(BF16)


def _attention(q, k, vt, n_batch, lq, lk, heads, tq, ahead):
    nq = lq // tq
    return pl.pallas_call(
        functools.partial(_attn_kernel, heads=heads, ahead=ahead),
        grid=(n_batch, N_HEADS // heads, nq),
        in_specs=[
            pl.BlockSpec((tq, heads * HEAD_PAD), lambda b, g, i: (b * nq + i, g)),
            pl.BlockSpec((lk, heads * HEAD_PAD), lambda b, g, i: (b, g)),
            pl.BlockSpec((heads * V_DIM, lk), lambda b, g, i: (g, b)),
        ],
        out_specs=pl.BlockSpec((tq, heads * V_DIM), lambda b, g, i: (b * nq + i, g)),
        out_shape=jax.ShapeDtypeStruct((n_batch * lq, N_HEADS * V_DIM), BF16),
        compiler_params=_params("parallel", "parallel", "arbitrary"),
        name="attention",
    )(q, k, vt)


def _mlp_kernel(h_ref, x_ref, gate_ref, w1_ref, w2_ref, gf_ref, *rest, tm, final, n_cast):
    src_refs, o_ref, dst_refs = rest[:n_cast], rest[n_cast], rest[n_cast + 1:]
    f = pl.program_id(1)

    @pl.when(f == 0)
    def _():
        o_ref[...] = x_ref[...]

    hid = _dot(h_ref[...], w1_ref[...])
    hid = jnp.square(jnp.maximum(hid, 0.0)).astype(BF16)
    o_ref[...] += gate_ref[0] * _dot(hid, w2_ref[...])

    for src_ref, dst_ref in zip(src_refs, dst_refs):
        dst_ref[...] = src_ref[...].astype(BF16)

    if final:
        @pl.when(f == pl.num_programs(1) - 1)
        def _():
            gf = gf_ref[...]
            for s in range(tm // SUB):
                rs = slice(s * SUB, (s + 1) * SUB)
                y = o_ref[rs, :]
                o_ref[rs, :] = y * lax.rsqrt(jnp.mean(y * y, axis=-1, keepdims=True) + EPS) * gf


def _mlp(st, h, x, mod, w1, w2, final_g, layer, final, casts=(), tm=512, tf=1024):
    row = lambda i, f: (i, 0)
    nf = D_FF // tf
    steps = (st.rows // tm) * nf
    slab = lambda i, f: (i * nf + f, 0)
    cast_in, cast_out, cast_shape = [], [], []
    for w, l in casts:
        _, r, c = w.shape
        cast_in.append(pl.BlockSpec((None, r // steps, c), lambda i, f, l=l: (l, i * nf + f, 0)))
        cast_out.append(pl.BlockSpec((r // steps, c), slab))
        cast_shape.append(jax.ShapeDtypeStruct((r, c), BF16))
    out = pl.pallas_call(
        functools.partial(_mlp_kernel, tm=tm, final=final, n_cast=len(casts)),
        grid=(st.rows // tm, nf),
        in_specs=[
            pl.BlockSpec((tm, D_MODEL), row),
            pl.BlockSpec((tm, D_MODEL), row),
            _mod_spec(st, layer, 5, tm),
            pl.BlockSpec((D_MODEL, tf), lambda i, f: (0, f)),
            pl.BlockSpec((tf, D_MODEL), lambda i, f: (f, 0)),
            _resident((1, D_MODEL)),
        ] + cast_in,
        out_specs=[pl.BlockSpec((tm, D_MODEL), row)] + cast_out,
        out_shape=[jax.ShapeDtypeStruct((st.rows, D_MODEL), F32)] + cast_shape,
        compiler_params=_params("parallel", "arbitrary"),
        name="mlp",
    )(h, x, mod, w1, w2, final_g, *[w for w, _ in casts])
    return out[0], out[1:]


def _rope_tables():
    t = np.arange(DEC_SEQ)
    nf = QK_ROPE // 4
    inv = (1.0 / (np.float32(ROPE_THETA) ** (np.arange(nf, dtype=np.float32) / np.float32(nf)))).astype(np.float32)
    ang_r = (t // GRID_W).astype(np.float32)[:, None] * inv
    ang_c = (t % GRID_W).astype(np.float32)[:, None] * inv
    cr, sr, cc, sc = np.cos(ang_r), np.sin(ang_r), np.cos(ang_c), np.sin(ang_c)
    zeros = np.zeros((DEC_SEQ, LANE - QK_ROPE), np.float32)
    cos = np.concatenate([cr, cr, cc, cc, zeros], axis=1)
    sin = np.concatenate([-sr, sr, -sc, sc, zeros], axis=1)
    return jnp.asarray(cos, F32), jnp.asarray(sin, F32)


def _swap_rope_partners(w):
    q = QK_ROPE // 4
    return jnp.concatenate([w[..., q:2 * q], w[..., :q], w[..., 3 * q:], w[..., 2 * q:3 * q]], axis=-1)


def _mla_weights(w_q_b, w_kv_a, w_kv_b):
    wq = w_q_b.reshape(Q_RANK, N_HEADS, QK_NOPE + QK_ROPE)
    pad = jnp.zeros((Q_RANK, N_HEADS, HEAD_PAD - QK_NOPE - QK_ROPE), w_q_b.dtype)
    wq_full = jnp.concatenate([wq, pad], axis=-1).reshape(Q_RANK, N_HEADS * HEAD_PAD)
    wq_swap = jnp.concatenate([_swap_rope_partners(wq[..., QK_NOPE:]), pad], axis=-1).reshape(Q_RANK, N_HEADS * LANE)
    kpe_w = w_kv_a[:, KV_RANK:]
    zpad = jnp.zeros((D_MODEL, LANE - QK_ROPE), w_kv_a.dtype)
    wkv = jnp.concatenate([w_kv_a[:, :KV_RANK], kpe_w, zpad, _swap_rope_partners(kpe_w), zpad], axis=1)
    wkvb = w_kv_b.reshape(KV_RANK, N_HEADS, QK_NOPE + V_DIM)
    wk = wkvb[..., :QK_NOPE].reshape(KV_RANK, N_HEADS * QK_NOPE)
    wvt = wkvb[..., QK_NOPE:].reshape(KV_RANK, N_HEADS * V_DIM).T
    return wq_full.astype(BF16), wq_swap.astype(BF16), wkv.astype(BF16), wk.astype(BF16), wvt.astype(BF16)


def kernel(x_prompt, x_sample, cache_ckv, cache_kpe, c, c_ctx, ada_w, ada_b, norm1, norm2, conv_w_in, conv_w, conv_w_out, gmlp_w_in, gmlp_g_v, gmlp_w_s, gmlp_b_s, gmlp_w_out, mla_w_q_a, mla_g_q, mla_w_q_b, mla_w_kv_a, mla_g_kv, mla_w_kv_b, mla_w_o, mlp_w1, mlp_w2, final_norm):
    D = D_MODEL
    streams = (PROMPT, SAMPLE)
    xs = [x_prompt.reshape(PROMPT.rows, D), x_sample.reshape(SAMPLE.rows, D)]
    cond = jnp.concatenate([c_ctx[None, :], c, jnp.zeros((N_GROUPS - 1 - DEC_BATCH, D), F32)], axis=0)
    mod = _modvec(cond, ada_w, ada_b)

    n1, n2 = norm1[:, None, :], norm2[:, None, :]
    final_g = final_norm[None, :]

    def big_weights(i):
        mix_in, mix_out = ((conv_w_in, conv_w_out), (gmlp_w_in, gmlp_w_out), (mla_w_q_a, mla_w_o))[i % N_MIXERS]
        return {"w1": (mlp_w1, i), "in": (mix_in, i // N_MIXERS), "w2": (mlp_w2, i), "out": (mix_out, i // N_MIXERS)}

    cast_split = (("w1", "in"), ("w2", "out"))
    bf = {name: w[l].astype(BF16) for name, (w, l) in big_weights(0).items() if name in ("in", "out")}

    cos_t, sin_t = _rope_tables()
    new_ckv, new_kpe = [], []
    for i in range(DEPTH):
        kind, j = i % N_MIXERS, i // N_MIXERS
        hs = [None, None]
        if kind == 0:
            host = "w1" not in bf
            for n, st in enumerate(streams):
                cast = HalfCast(mlp_w1, i, n, bf.get("w1")) if host else None
                bg, p, *half = _conv_in(st, xs[n], mod, n1, bf["in"], i, cast=cast)
                bf.update(zip(("w1",), half))
                cast = HalfCast(mlp_w2, i, n, bf.get("w2")) if host else None
                xs[n], hs[n], *half = _conv_out(st, bg, p, conv_w, xs[n], mod, bf["out"], n2, i, j, cast=cast)
                bf.update(zip(("w2",), half))
        elif kind == 1:
            b_full = jnp.repeat(gmlp_b_s[j].T, LANE, axis=1)
            w_s = gmlp_w_s[j].astype(BF16)
            for n, st in enumerate(streams):
                z = _gmlp_in(st, xs[n], mod, n1, bf["in"], i)
                xs[n], hs[n] = _gmlp_out(st, z, gmlp_g_v[:, None, :], w_s, b_full, xs[n], mod, bf["out"], n2, i, j)
        else:
            wq, wqs, wkv, wk, wvt = _mla_weights(mla_w_q_b[j], mla_w_kv_a[j], mla_w_kv_b[j])
            proj = [_mla_proj(st, xs[n], mod, n1, bf["in"], mla_g_q[:, None, :], wq, wqs, wkv, mla_g_kv[:, None, :],
                              cos_t, sin_t, i, j) for n, st in enumerate(streams)]
            q, ckv, kpe, kpe_rot = proj[0]
            new_ckv.append(ckv.reshape(BATCH, SEQ, KV_RANK))
            new_kpe.append(kpe[:, :QK_ROPE].reshape(BATCH, SEQ, QK_ROPE))
            k_p, v_p = _kv_expand(ckv, kpe_rot, wk, wvt)
            o_p = _attention(q, k_p, v_p, BATCH, SEQ, SEQ, **ATTN_PROMPT)
            q, ckv, kpe, kpe_rot = proj[1]
            lk = PAST_LEN + DEC_SEQ
            ckv_s = jnp.concatenate([cache_ckv[:, j], ckv.reshape(DEC_BATCH, DEC_SEQ, KV_RANK)], axis=1)
            cache_pe = jnp.pad(cache_kpe[:, j], ((0, 0), (0, 0), (0, LANE - QK_ROPE))).astype(BF16)
            kpe_s = jnp.concatenate([cache_pe, kpe_rot.reshape(DEC_BATCH, DEC_SEQ, LANE)], axis=1)
            k_s, v_s = _kv_expand(ckv_s.reshape(DEC_BATCH * lk, KV_RANK), kpe_s.reshape(DEC_BATCH * lk, LANE), wk, wvt)
            o_s = _attention(q, k_s, v_s, DEC_BATCH, DEC_SEQ, lk, **ATTN_SAMPLE)
            for n, (st, o) in enumerate(zip(streams, (o_p, o_s))):
                xs[n], hs[n] = _attn_out(st, o, xs[n], mod, bf["out"], n2, i)
        nxt = big_weights(i + 1) if i + 1 < DEPTH else None
        bf_next = {}
        for n, st in enumerate(streams):
            names = cast_split[n] if nxt else ()
            xs[n], cast = _mlp(st, hs[n], xs[n], mod, bf["w1"], bf["w2"], final_g, i, final=(i == DEPTH - 1),
                               casts=[nxt[name] for name in names])
            bf_next.update(zip(names, cast))
        bf = bf_next
    y_prompt = xs[0].reshape(BATCH, SEQ, D)
    y_sample = xs[1].reshape(DEC_BATCH, DEC_SEQ, D)
    return (y_prompt, y_sample, jnp.stack(new_ckv, axis=1), jnp.stack(new_kpe, axis=1))
```

```python
import functools
from typing import NamedTuple

import jax
import jax.numpy as jnp
import numpy as np
from jax import lax
from jax.experimental import pallas as pl
from jax.experimental.pallas import tpu as pltpu

D_MODEL = 2048
BATCH = 32
SEQ = 256
DEPTH = 4
DEC_BATCH = 4
DEC_SEQ = 2048
PAST_LEN = 512
GRID_W = 64
N_MIXERS = 3
CONV_WIDTH = 3
CHUNK = 128
GMLP_GROUPS = 16
N_HEADS = 16
QK_NOPE = 128
QK_ROPE = 64
V_DIM = 128
Q_RANK = 512
KV_RANK = 512
ROPE_THETA = 10000.0
D_FF = 4 * D_MODEL
N_MOD = 6
EPS = 1e-6
LOG2_E = 1.4426950408889634

N_GROUPS = 8
HEAD_PAD = 256
LANE = 128
BF16_ROWS = 16
SUB = 256
VMEM_LIMIT = 56 * 1024 * 1024
ATTN_PROMPT = dict(heads=16, tq=256, ahead=16)
ATTN_SAMPLE = dict(heads=4, tq=1024, ahead=1)

BF16 = jnp.bfloat16
F32 = jnp.float32


class Stream(NamedTuple):
    rows: int
    seq: int
    group0: int
    group_rows: int
    rope: bool


PROMPT = Stream(BATCH * SEQ, SEQ, 0, BATCH * SEQ, False)
SAMPLE = Stream(DEC_BATCH * DEC_SEQ, DEC_SEQ, 1, DEC_SEQ, True)


def _params(*sem):
    return pltpu.CompilerParams(dimension_semantics=sem, vmem_limit_bytes=VMEM_LIMIT)


def _dot(a, b):
    return jnp.dot(a, b, preferred_element_type=F32)


def _resident(shape, layer=None):
    if layer is None:
        zeros = (0,) * len(shape)
        return pl.BlockSpec(shape, lambda *_: zeros, pipeline_mode=pl.Buffered(1))
    index = (layer,) + (0,) * len(shape)
    return pl.BlockSpec((None,) + tuple(shape), lambda *_: index, pipeline_mode=pl.Buffered(1))


def _mod_spec(st, layer, chunk, tm):
    def index(i, *_):
        return ((layer * N_MOD + chunk) * N_GROUPS + st.group0 + (i * tm) // st.group_rows, 0, 0)

    return pl.BlockSpec((1, 1, D_MODEL), index)


def _gain_shift(g_ref, sh_ref, sc_ref):
    return g_ref[...] * (1.0 + sc_ref[0]), sh_ref[0]


def _norm_mod(x, gsc, sh):
    ms = jnp.mean(x * x, axis=-1, keepdims=True)
    return (x * lax.rsqrt(ms + EPS) * gsc + sh).astype(BF16)


def _modvec_kernel(c_ref, w_ref, b_ref, o_ref):
    s = jax.nn.silu(c_ref[...]).astype(BF16)
    o_ref[0, 0] = _dot(s, w_ref[0].astype(BF16)) + b_ref[0]


def _modvec(cond, ada_w, ada_b, tn=1024):
    per_chunk = D_MODEL // tn
    out = pl.pallas_call(
        _modvec_kernel,
        grid=(DEPTH, N_MOD * per_chunk),
        in_specs=[
            pl.BlockSpec((N_GROUPS, D_MODEL), lambda l, j: (0, 0)),
            pl.BlockSpec((1, D_MODEL, tn), lambda l, j: (l, 0, j)),
            pl.BlockSpec((1, 1, tn), lambda l, j: (l, 0, j)),
        ],
        out_specs=pl.BlockSpec((1, 1, N_GROUPS, tn), lambda l, j: (l, j // per_chunk, 0, j % per_chunk)),
        out_shape=jax.ShapeDtypeStruct((DEPTH, N_MOD, N_GROUPS, D_MODEL), F32),
        compiler_params=_params("arbitrary", "arbitrary"),
        name="modvec",
    )(cond, ada_w, ada_b.reshape(DEPTH, 1, N_MOD * D_MODEL))
    return out.reshape(DEPTH * N_MOD * N_GROUPS, 1, D_MODEL)


class HalfCast(NamedTuple):
    w: jax.Array
    layer: int
    part: int
    prev: jax.Array | None


def _half_cast_specs(cast, steps):
    _, r, c = cast.w.shape
    rows, off = r // (2 * steps), cast.part * steps
    in_specs = [pl.BlockSpec((None, rows, c), lambda i: (cast.layer, off + i, 0))]
    args = [cast.w]
    if cast.prev is not None:
        in_specs.append(pl.BlockSpec(memory_space=pl.ANY))
        args.append(cast.prev)
    return in_specs, args, pl.BlockSpec((rows, c), lambda i: (off + i, 0)), jax.ShapeDtypeStruct((r, c), BF16)


def _conv_in_kernel(x_ref, g_ref, sh_ref, sc_ref, w_ref, *rest, tm, cast):
    bg_ref, p_ref = rest[-3:-1] if cast else rest
    gsc, sh = _gain_shift(g_ref, sh_ref, sc_ref)
    for s in range(tm // SUB):
        rs = slice(s * SUB, (s + 1) * SUB)
        z = _dot(_norm_mod(x_ref[rs, :], gsc, sh), w_ref[...])
        bg_ref[rs, :] = z[:, :D_MODEL].astype(BF16)
        p_ref[rs, :] = (z[:, D_MODEL:2 * D_MODEL] * z[:, 2 * D_MODEL:]).astype(BF16)
    if cast:
        rest[-1][...] = rest[0][...].astype(BF16)


def _conv_in(st, x, mod, norm1, w_in, layer, cast=None, tm=512):
    row = lambda i: (i, 0)
    in_specs = [
        pl.BlockSpec((tm, D_MODEL), row),
        _resident((1, D_MODEL), layer),
        _mod_spec(st, layer, 0, tm),
        _mod_spec(st, layer, 1, tm),
        _resident((D_MODEL, 3 * D_MODEL)),
    ]
    args = [x, norm1, mod, mod, w_in]
    out_specs = [pl.BlockSpec((tm, D_MODEL), row)] * 2
    out_shape = [jax.ShapeDtypeStruct((st.rows, D_MODEL), BF16)] * 2
    aliases = {}
    if cast is not None:
        c_in, c_args, c_out, c_shape = _half_cast_specs(cast, st.rows // tm)
        if cast.prev is not None:
            aliases = {len(in_specs) + 1: len(out_specs)}
        in_specs, args, out_specs, out_shape = in_specs + c_in, args + c_args, out_specs + [c_out], out_shape + [c_shape]
    return pl.pallas_call(
        functools.partial(_conv_in_kernel, tm=tm, cast=cast is not None),
        grid=(st.rows // tm,),
        in_specs=in_specs,
        out_specs=out_specs,
        out_shape=out_shape,
        input_output_aliases=aliases,
        compiler_params=_params("parallel"),
        name="conv_in",
    )(*args)


def _gmlp_in_kernel(x_ref, g_ref, sh_ref, sc_ref, w_ref, z_ref, *, tm):
    gsc, sh = _gain_shift(g_ref, sh_ref, sc_ref)
    for s in range(tm // SUB):
        rs = slice(s * SUB, (s + 1) * SUB)
        z_ref[rs, :] = _dot(_norm_mod(x_ref[rs, :], gsc, sh), w_ref[...]).astype(BF16)


def _gmlp_in(st, x, mod, norm1, w_in, layer, tm=512):
    n_out = w_in.shape[-1]
    row = lambda i: (i, 0)
    return pl.pallas_call(
        functools.partial(_gmlp_in_kernel, tm=tm),
        grid=(st.rows // tm,),
        in_specs=[
            pl.BlockSpec((tm, D_MODEL), row),
            _resident((1, D_MODEL), layer),
            _mod_spec(st, layer, 0, tm),
            _mod_spec(st, layer, 1, tm),
            _resident((D_MODEL, n_out)),
        ],
        out_specs=pl.BlockSpec((tm, n_out), row),
        out_shape=jax.ShapeDtypeStruct((st.rows, n_out), BF16),
        compiler_params=_params("parallel"),
        name="gmlp_in",
    )(x, norm1, mod, mod, w_in)


def _residual_proj(a, rs, x_ref, gate_ref, w_ref, gsc2, sh2, o_ref, h_ref):
    x1 = x_ref[rs, :] + gate_ref[0] * _dot(a, w_ref[...])
    o_ref[rs, :] = x1
    h_ref[rs, :] = _norm_mod(x1, gsc2, sh2)


def _proj_specs(st, layer, tm):
    row = lambda i: (i, 0)
    in_specs = [
        pl.BlockSpec((tm, D_MODEL), row),
        _mod_spec(st, layer, 2, tm),
        _resident((D_MODEL, D_MODEL)),
        _resident((1, D_MODEL), layer),
        _mod_spec(st, layer, 3, tm),
        _mod_spec(st, layer, 4, tm),
    ]
    out_specs = [pl.BlockSpec((tm, D_MODEL), row)] * 2
    out_shape = [jax.ShapeDtypeStruct((st.rows, D_MODEL), F32), jax.ShapeDtypeStruct((st.rows, D_MODEL), BF16)]
    return in_specs, out_specs, out_shape


def _conv_out_kernel(bg_ref, p_ref, lo_ref, hi_ref, cw_ref, x_ref, gate_ref, w_ref, g2_ref, sh2_ref, sc2_ref,
                     *rest, tm, seq, cast):
    if cast:
        o_ref, h_ref, dst_ref, pbuf_ref = rest[-4:]
        dst_ref[...] = rest[0][...].astype(BF16)
    else:
        o_ref, h_ref, pbuf_ref = rest
    halo = BF16_ROWS
    n_sub = tm // SUB
    gsc2, sh2 = _gain_shift(g2_ref, sh2_ref, sc2_ref)
    base = pl.program_id(0) * tm
    width = 4 * LANE
    for s in range(n_sub):
        r0 = s * SUB
        before = lo_ref[...] if s == 0 else p_ref[r0 - halo:r0, :]
        after = hi_ref[...] if s == n_sub - 1 else p_ref[r0 + SUB:r0 + SUB + halo, :]
        pbuf_ref[s, 0:halo, :] = jnp.where((base + r0) % seq == 0, 0.0, before.astype(F32))
        pbuf_ref[s, halo:halo + SUB, :] = p_ref[r0:r0 + SUB, :].astype(F32)
        pbuf_ref[s, halo + SUB:, :] = jnp.where((base + r0 + SUB) % seq == 0, 0.0, after.astype(F32))
        parts = []
        for lo in range(0, D_MODEL, width):
            cols = slice(lo, lo + width)
            prev = pbuf_ref[s, halo - 1:halo - 1 + SUB, cols]
            cur = pbuf_ref[s, halo:halo + SUB, cols]
            nxt = pbuf_ref[s, halo + 1:halo + 1 + SUB, cols]
            conv = prev * cw_ref[0:1, cols] + cur * cw_ref[1:2, cols] + nxt * cw_ref[2:3, cols]
            parts.append((bg_ref[r0:r0 + SUB, cols].astype(F32) * conv).astype(BF16))
        _residual_proj(jnp.concatenate(parts, axis=1), slice(r0, r0 + SUB), x_ref, gate_ref, w_ref, gsc2, sh2,
                       o_ref, h_ref)


def _conv_out(st, bg, p, conv_w, x, mod, w_out, norm2, layer, j, cast=None, tm=512):
    assert st.seq % SUB == 0 and tm % SUB == 0
    nb = tm // BF16_ROWS
    last_halo = st.rows // BF16_ROWS - 1
    row = lambda i: (i, 0)
    tail, out_specs, out_shape = _proj_specs(st, layer, tm)
    in_specs = [
        pl.BlockSpec((tm, D_MODEL), row),
        pl.BlockSpec((tm, D_MODEL), row),
        pl.BlockSpec((BF16_ROWS, D_MODEL), lambda i: (jnp.maximum(i * nb - 1, 0), 0)),
        pl.BlockSpec((BF16_ROWS, D_MODEL), lambda i: (jnp.minimum((i + 1) * nb, last_halo), 0)),
        _resident((CONV_WIDTH, D_MODEL), j),
    ] + tail
    args = [bg, p, p, p, conv_w, x, mod, w_out, norm2, mod, mod]
    aliases = {}
    if cast is not None:
        c_in, c_args, c_out, c_shape = _half_cast_specs(cast, st.rows // tm)
        if cast.prev is not None:
            aliases = {len(in_specs) + 1: len(out_specs)}
        in_specs, args, out_specs, out_shape = in_specs + c_in, args + c_args, out_specs + [c_out], out_shape + [c_shape]
    return pl.pallas_call(
        functools.partial(_conv_out_kernel, tm=tm, seq=st.seq, cast=cast is not None),
        grid=(st.rows // tm,),
        in_specs=in_specs,
        out_specs=out_specs,
        out_shape=out_shape,
        scratch_shapes=[pltpu.VMEM((tm // SUB, SUB + 2 * BF16_ROWS, D_MODEL), F32)],
        input_output_aliases=aliases,
        compiler_params=_params("parallel"),
        name="conv_out",
    )(*args)


def _gmlp_out_kernel(zu_ref, zv_ref, gv_ref, ws_ref, bs_ref, x_ref, gate_ref, w_ref, g2_ref, sh2_ref, sc2_ref,
                     o_ref, h_ref, a_ref, vn_ref, *, tm):
    gv = gv_ref[...]
    gsc2, sh2 = _gain_shift(g2_ref, sh2_ref, sc2_ref)
    for s in range(tm // SUB):
        chunks = range(s * SUB // CHUNK, (s + 1) * SUB // CHUNK)
        for c in chunks:
            rs = slice(c * CHUNK, (c + 1) * CHUNK)
            v = zv_ref[rs, :].astype(F32)
            ms = jnp.mean(v * v, axis=-1, keepdims=True)
            vn_ref[rs, :] = (v * lax.rsqrt(ms + EPS) * gv).astype(BF16)
        for g in range(GMLP_GROUPS):
            cols = slice(g * LANE, (g + 1) * LANE)
            rhs = jnp.concatenate([vn_ref[c * CHUNK:(c + 1) * CHUNK, cols] for c in chunks], axis=1)
            mixed = _dot(ws_ref[g], rhs)
            for n, c in enumerate(chunks):
                rs = slice(c * CHUNK, (c + 1) * CHUNK)
                vv = mixed[:, n * CHUNK:(n + 1) * CHUNK] + bs_ref[:, cols]
                a_ref[rs, cols] = (zu_ref[rs, cols].astype(F32) * vv).astype(BF16)
        rs = slice(s * SUB, (s + 1) * SUB)
        _residual_proj(a_ref[rs, :], rs, x_ref, gate_ref, w_ref, gsc2, sh2, o_ref, h_ref)


def _gmlp_out(st, z, g_v, w_s, b_full, x, mod, w_out, norm2, layer, j, tm=512):
    tail, out_specs, out_shape = _proj_specs(st, layer, tm)
    return pl.pallas_call(
        functools.partial(_gmlp_out_kernel, tm=tm),
        grid=(st.rows // tm,),
        in_specs=[
            pl.BlockSpec((tm, D_MODEL), lambda i: (i, 0)),
            pl.BlockSpec((tm, D_MODEL), lambda i: (i, 1)),
            _resident((1, D_MODEL), j),
            _resident((GMLP_GROUPS, CHUNK, CHUNK)),
            _resident((CHUNK, D_MODEL)),
        ] + tail,
        out_specs=out_specs,
        out_shape=out_shape,
        scratch_shapes=[pltpu.VMEM((tm, D_MODEL), BF16), pltpu.VMEM((tm, D_MODEL), BF16)],
        compiler_params=_params("parallel"),
        name="gmlp_out",
    )(z, z, g_v, w_s, b_full, x, mod, w_out, norm2, mod, mod)


def _attn_out_kernel(a_ref, x_ref, gate_ref, w_ref, g2_ref, sh2_ref, sc2_ref, o_ref, h_ref, *, tm):
    gsc2, sh2 = _gain_shift(g2_ref, sh2_ref, sc2_ref)
    for s in range(tm // SUB):
        rs = slice(s * SUB, (s + 1) * SUB)
        _residual_proj(a_ref[rs, :], rs, x_ref, gate_ref, w_ref, gsc2, sh2, o_ref, h_ref)


def _attn_out(st, a, x, mod, w_o, norm2, layer, tm=512):
    tail, out_specs, out_shape = _proj_specs(st, layer, tm)
    return pl.pallas_call(
        functools.partial(_attn_out_kernel, tm=tm),
        grid=(st.rows // tm,),
        in_specs=[pl.BlockSpec((tm, D_MODEL), lambda i: (i, 0))] + tail,
        out_specs=out_specs,
        out_shape=out_shape,
        compiler_params=_params("parallel"),
        name="attn_out",
    )(a, x, mod, w_o, norm2, mod, mod)


def _mla_proj_kernel(x_ref, g_ref, sh_ref, sc_ref, wqa_ref, gq_ref, wqn_ref, wqr_ref, wqrs_ref, wkv_ref, gkv_ref,
                     cos_ref, sin_ref, q_ref, ckv_ref, kpe_ref, kper_ref, *, tm, rope):
    gsc, sh = _gain_shift(g_ref, sh_ref, sc_ref)
    gq = gq_ref[...] * ((QK_NOPE + QK_ROPE) ** -0.5 * LOG2_E)
    first_half = lax.broadcasted_iota(jnp.int32, (1, LANE), 1) < QK_ROPE
    for s in range(tm // SUB):
        rs = slice(s * SUB, (s + 1) * SUB)
        h = _norm_mod(x_ref[rs, :], gsc, sh)
        qa = _dot(h, wqa_ref[...])
        qa = qa * lax.rsqrt(jnp.mean(qa * qa, axis=-1, keepdims=True) + EPS) * gq
        qa = qa.astype(BF16)
        qn = _dot(qa, wqn_ref[...])
        qr = _dot(qa, wqr_ref[...])
        kv = _dot(h, wkv_ref[...])
        c = kv[:, :KV_RANK]
        ckv_ref[rs, :] = c * lax.rsqrt(jnp.mean(c * c, axis=-1, keepdims=True) + EPS) * gkv_ref[...]
        kpe = kv[:, KV_RANK:KV_RANK + LANE]
        kpe_ref[rs, :] = kpe
        if rope:
            cos = cos_ref[rs, :]
            sin = sin_ref[rs, :]
            qrs = _dot(qa, wqrs_ref[...])
            kper_ref[rs, :] = (kpe * cos + kv[:, KV_RANK + LANE:] * sin).astype(BF16)
        else:
            kper_ref[rs, :] = kpe.astype(BF16)
        for pair in range(N_HEADS // 2):
            tile = slice(pair * LANE, (pair + 1) * LANE)
            rot = qr[:, tile] * cos + qrs[:, tile] * sin if rope else qr[:, tile]
            halves = (rot, pltpu.roll(rot, QK_ROPE, axis=1))
            for hd, val in zip((2 * pair, 2 * pair + 1), halves):
                lo = hd * HEAD_PAD
                q_ref[rs, lo:lo + LANE] = qn[:, hd * LANE:(hd + 1) * LANE].astype(BF16)
                q_ref[rs, lo + LANE:lo + HEAD_PAD] = jnp.where(first_half, val, 0.0).astype(BF16)


def _mla_proj(st, x, mod, norm1, wqa, gq, wqn, wqr, wqrs, wkv, gkv, cos_t, sin_t, layer, j, tm=512):
    per_seq = st.seq // tm
    rope_idx = (lambda i: (i % per_seq, 0)) if st.rope else (lambda i: (0, 0))
    row = lambda i: (i, 0)
    return pl.pallas_call(
        functools.partial(_mla_proj_kernel, tm=tm, rope=st.rope),
        grid=(st.rows // tm,),
        in_specs=[
            pl.BlockSpec((tm, D_MODEL), row),
            _resident((1, D_MODEL), layer),
            _mod_spec(st, layer, 0, tm),
            _mod_spec(st, layer, 1, tm),
            _resident(wqa.shape),
            _resident((1, Q_RANK), j),
            _resident(wqn.shape),
            _resident(wqr.shape),
            _resident(wqrs.shape),
            _resident(wkv.shape),
            _resident((1, KV_RANK), j),
            pl.BlockSpec((tm, LANE), rope_idx),
            pl.BlockSpec((tm, LANE), rope_idx),
        ],
        out_specs=[
            pl.BlockSpec((tm, N_HEADS * HEAD_PAD), row),
            pl.BlockSpec((tm, KV_RANK), row),
            pl.BlockSpec((tm, LANE), row),
            pl.BlockSpec((tm, LANE), row),
        ],
        out_shape=[
            jax.ShapeDtypeStruct((st.rows, N_HEADS * HEAD_PAD), BF16),
            jax.ShapeDtypeStruct((st.rows, KV_RANK), F32),
            jax.ShapeDtypeStruct((st.rows, LANE), F32),
            jax.ShapeDtypeStruct((st.rows, LANE), BF16),
        ],
        compiler_params=_params("parallel"),
        name="mla_proj",
    )(x, norm1, mod, mod, wqa, gq, wqn, wqr, wqrs, wkv, gkv, cos_t, sin_t)


def _dot_nt(a, b):
    return lax.dot_general(a, b, (((1,), (1,)), ((), ())), preferred_element_type=F32)


def _kv_expand_kernel(c_ref, kpe_ref, wk_ref, wvt_ref, k_ref, vt_ref):
    c = c_ref[...].astype(BF16)
    kn = _dot(c, wk_ref[...])
    vt_ref[...] = _dot_nt(wvt_ref[...], c).astype(BF16)
    kpe = kpe_ref[...]
    for hd in range(N_HEADS):
        lo = hd * HEAD_PAD
        k_ref[:, lo:lo + LANE] = kn[:, hd * LANE:(hd + 1) * LANE].astype(BF16)
        k_ref[:, lo + LANE:lo + HEAD_PAD] = kpe


def _kv_expand(ckv, kpe, wk, wvt, tr=512):
    n_rows = ckv.shape[0]
    row = lambda i: (i, 0)
    return pl.pallas_call(
        _kv_expand_kernel,
        grid=(n_rows // tr,),
        in_specs=[
            pl.BlockSpec((tr, KV_RANK), row),
            pl.BlockSpec((tr, LANE), row),
            _resident(wk.shape),
            _resident(wvt.shape),
        ],
        out_specs=[
            pl.BlockSpec((tr, N_HEADS * HEAD_PAD), row),
            pl.BlockSpec((N_HEADS * V_DIM, tr), lambda i: (0, i)),
        ],
        out_shape=[
            jax.ShapeDtypeStruct((n_rows, N_HEADS * HEAD_PAD), BF16),
            jax.ShapeDtypeStruct((N_HEADS * V_DIM, n_rows), BF16),
        ],
        compiler_params=_params("parallel"),
        name="kv_expand",
    )(ckv, kpe, wk, wvt)


def _attn_kernel(q_ref, k_ref, vt_ref, o_ref, *, heads, ahead):
    ones = jnp.ones((BF16_ROWS, k_ref.shape[0]), BF16)

    def scores(hd):
        cols = slice(hd * HEAD_PAD, (hd + 1) * HEAD_PAD)
        return _dot_nt(k_ref[:, cols], q_ref[:, cols])

    pending = [scores(hd) for hd in range(min(ahead, heads))]
    for hd in range(heads):
        st = pending.pop(0)
        if hd + ahead < heads:
            pending.append(scores(hd + ahead))
        p = jnp.exp2(st - jnp.max(st, axis=0, keepdims=True)).astype(BF16)
        lhs = jnp.concatenate([vt_ref[hd * V_DIM:(hd + 1) * V_DIM, :], ones], axis=0)
        ot = _dot(lhs, p)
        o_ref[:, hd * V_DIM:(hd + 1) * V_DIM] = (ot[:V_DIM] / ot[V_DIM:V_DIM + 1]).T.astype(BF16)


def _attention(q, k, vt, n_batch, lq, lk, heads, tq, ahead):
    nq = lq // tq
    return pl.pallas_call(
        functools.partial(_attn_kernel, heads=heads, ahead=ahead),
        grid=(n_batch, N_HEADS // heads, nq),
        in_specs=[
            pl.BlockSpec((tq, heads * HEAD_PAD), lambda b, g, i: (b * nq + i, g)),
            pl.BlockSpec((lk, heads * HEAD_PAD), lambda b, g, i: (b, g)),
            pl.BlockSpec((heads * V_DIM, lk), lambda b, g, i: (g, b)),
        ],
        out_specs=pl.BlockSpec((tq, heads * V_DIM), lambda b, g, i: (b * nq + i, g)),
        out_shape=jax.ShapeDtypeStruct((n_batch * lq, N_HEADS * V_DIM), BF16),
        compiler_params=_params("parallel", "parallel", "arbitrary"),
        name="attention",
    )(q, k, vt)


def _mlp_kernel(h_ref, x_ref, gate_ref, w1_ref, w2_ref, gf_ref, *rest, tm, final, n_cast):
    src_refs, o_ref, dst_refs = rest[:n_cast], rest[n_cast], rest[n_cast + 1:]
    f = pl.program_id(1)

    @pl.when(f == 0)
    def _():
        o_ref[...] = x_ref[...]

    hid = _dot(h_ref[...], w1_ref[...])
    hid = jnp.square(jnp.maximum(hid, 0.0)).astype(BF16)
    o_ref[...] += gate_ref[0] * _dot(hid, w2_ref[...])

    for src_ref, dst_ref in zip(src_refs, dst_refs):
        dst_ref[...] = src_ref[...].astype(BF16)

    if final:
        @pl.when(f == pl.num_programs(1) - 1)
        def _():
            gf = gf_ref[...]
            for s in range(tm // SUB):
                rs = slice(s * SUB, (s + 1) * SUB)
                y = o_ref[rs, :]
                o_ref[rs, :] = y * lax.rsqrt(jnp.mean(y * y, axis=-1, keepdims=True) + EPS) * gf


def _mlp(st, h, x, mod, w1, w2, final_g, layer, final, casts=(), tm=512, tf=1024):
    row = lambda i, f: (i, 0)
    nf = D_FF // tf
    steps = (st.rows // tm) * nf
    slab = lambda i, f: (i * nf + f, 0)
    cast_in, cast_out, cast_shape = [], [], []
    for w, l in casts:
        _, r, c = w.shape
        cast_in.append(pl.BlockSpec((None, r // steps, c), lambda i, f, l=l: (l, i * nf + f, 0)))
        cast_out.append(pl.BlockSpec((r // steps, c), slab))
        cast_shape.append(jax.ShapeDtypeStruct((r, c), BF16))
    out = pl.pallas_call(
        functools.partial(_mlp_kernel, tm=tm, final=final, n_cast=len(casts)),
        grid=(st.rows // tm, nf),
        in_specs=[
            pl.BlockSpec((tm, D_MODEL), row),
            pl.BlockSpec((tm, D_MODEL), row),
            _mod_spec(st, layer, 5, tm),
            pl.BlockSpec((D_MODEL, tf), lambda i, f: (0, f)),
            pl.BlockSpec((tf, D_MODEL), lambda i, f: (f, 0)),
            _resident((1, D_MODEL)),
        ] + cast_in,
        out_specs=[pl.BlockSpec((tm, D_MODEL), row)] + cast_out,
        out_shape=[jax.ShapeDtypeStruct((st.rows, D_MODEL), F32)] + cast_shape,
        compiler_params=_params("parallel", "arbitrary"),
        name="mlp",
    )(h, x, mod, w1, w2, final_g, *[w for w, _ in casts])
    return out[0], out[1:]


def _rope_tables():
    t = np.arange(DEC_SEQ)
    nf = QK_ROPE // 4
    inv = (1.0 / (np.float32(ROPE_THETA) ** (np.arange(nf, dtype=np.float32) / np.float32(nf)))).astype(np.float32)
    ang_r = (t // GRID_W).astype(np.float32)[:, None] * inv
    ang_c = (t % GRID_W).astype(np.float32)[:, None] * inv
    cr, sr, cc, sc = np.cos(ang_r), np.sin(ang_r), np.cos(ang_c), np.sin(ang_c)
    cos = np.tile(np.concatenate([cr, cr, cc, cc], axis=1), (1, LANE // QK_ROPE))
    sin = np.tile(np.concatenate([-sr, sr, -sc, sc], axis=1), (1, LANE // QK_ROPE))
    return jnp.asarray(cos, F32), jnp.asarray(sin, F32)


def _swap_rope_partners(w):
    q = QK_ROPE // 4
    return jnp.concatenate([w[..., q:2 * q], w[..., :q], w[..., 3 * q:], w[..., 2 * q:3 * q]], axis=-1)


def _mla_weights(w_q_b, w_kv_a, w_kv_b):
    wq = w_q_b.reshape(Q_RANK, N_HEADS, QK_NOPE + QK_ROPE)
    wq_nope = wq[..., :QK_NOPE].reshape(Q_RANK, N_HEADS * QK_NOPE)
    wq_rope = wq[..., QK_NOPE:].reshape(Q_RANK, N_HEADS * QK_ROPE)
    wq_rope_swap = _swap_rope_partners(wq[..., QK_NOPE:]).reshape(Q_RANK, N_HEADS * QK_ROPE)
    kpe_w = w_kv_a[:, KV_RANK:]
    zpad = jnp.zeros((D_MODEL, LANE - QK_ROPE), w_kv_a.dtype)
    wkv = jnp.concatenate([w_kv_a[:, :KV_RANK], kpe_w, zpad, _swap_rope_partners(kpe_w), zpad], axis=1)
    wkvb = w_kv_b.reshape(KV_RANK, N_HEADS, QK_NOPE + V_DIM)
    wk = wkvb[..., :QK_NOPE].reshape(KV_RANK, N_HEADS * QK_NOPE)
    wvt = wkvb[..., QK_NOPE:].reshape(KV_RANK, N_HEADS * V_DIM).T
    casts = (wq_nope, wq_rope, wq_rope_swap, wkv, wk, wvt)
    return tuple(w.astype(BF16) for w in casts)


def kernel(x_prompt, x_sample, cache_ckv, cache_kpe, c, c_ctx, ada_w, ada_b, norm1, norm2, conv_w_in, conv_w, conv_w_out, gmlp_w_in, gmlp_g_v, gmlp_w_s, gmlp_b_s, gmlp_w_out, mla_w_q_a, mla_g_q, mla_w_q_b, mla_w_kv_a, mla_g_kv, mla_w_kv_b, mla_w_o, mlp_w1, mlp_w2, final_norm):
    D = D_MODEL
    streams = (PROMPT, SAMPLE)
    xs = [x_prompt.reshape(PROMPT.rows, D), x_sample.reshape(SAMPLE.rows, D)]
    cond = jnp.concatenate([c_ctx[None, :], c, jnp.zeros((N_GROUPS - 1 - DEC_BATCH, D), F32)], axis=0)
    mod = _modvec(cond, ada_w, ada_b)

    n1, n2 = norm1[:, None, :], norm2[:, None, :]
    final_g = final_norm[None, :]

    def big_weights(i):
        mix_in, mix_out = ((conv_w_in, conv_w_out), (gmlp_w_in, gmlp_w_out), (mla_w_q_a, mla_w_o))[i % N_MIXERS]
        return {"w1": (mlp_w1, i), "in": (mix_in, i // N_MIXERS), "w2": (mlp_w2, i), "out": (mix_out, i // N_MIXERS)}

    cast_split = (("w1", "in"), ("w2", "out"))
    bf = {name: w[l].astype(BF16) for name, (w, l) in big_weights(0).items() if name in ("in", "out")}

    cos_t, sin_t = _rope_tables()
    new_ckv, new_kpe = [], []
    for i in range(DEPTH):
        kind, j = i % N_MIXERS, i // N_MIXERS
        hs = [None, None]
        if kind == 0:
            host = "w1" not in bf
            for n, st in enumerate(streams):
                cast = HalfCast(mlp_w1, i, n, bf.get("w1")) if host else None
                bg, p, *half = _conv_in(st, xs[n], mod, n1, bf["in"], i, cast=cast)
                bf.update(zip(("w1",), half))
                cast = HalfCast(mlp_w2, i, n, bf.get("w2")) if host else None
                xs[n], hs[n], *half = _conv_out(st, bg, p, conv_w, xs[n], mod, bf["out"], n2, i, j, cast=cast)
                bf.update(zip(("w2",), half))
        elif kind == 1:
            b_full = jnp.repeat(gmlp_b_s[j].T, LANE, axis=1)
            w_s = gmlp_w_s[j].astype(BF16)
            for n, st in enumerate(streams):
                z = _gmlp_in(st, xs[n], mod, n1, bf["in"], i)
                xs[n], hs[n] = _gmlp_out(st, z, gmlp_g_v[:, None, :], w_s, b_full, xs[n], mod, bf["out"], n2, i, j)
        else:
            wqn, wqr, wqrs, wkv, wk, wvt = _mla_weights(mla_w_q_b[j], mla_w_kv_a[j], mla_w_kv_b[j])
            proj = [_mla_proj(st, xs[n], mod, n1, bf["in"], mla_g_q[:, None, :], wqn, wqr, wqrs, wkv, mla_g_kv[:, None, :],
                              cos_t, sin_t, i, j) for n, st in enumerate(streams)]
            q, ckv, kpe, kpe_rot = proj[0]
            new_ckv.append(ckv.reshape(BATCH, SEQ, KV_RANK))
            new_kpe.append(kpe[:, :QK_ROPE].reshape(BATCH, SEQ, QK_ROPE))
            k_p, v_p = _kv_expand(ckv, kpe_rot, wk, wvt)
            o_p = _attention(q, k_p, v_p, BATCH, SEQ, SEQ, **ATTN_PROMPT)
            q, ckv, kpe, kpe_rot = proj[1]
            lk = PAST_LEN + DEC_SEQ
            ckv_s = jnp.concatenate([cache_ckv[:, j], ckv.reshape(DEC_BATCH, DEC_SEQ, KV_RANK)], axis=1)
            cache_pe = jnp.pad(cache_kpe[:, j], ((0, 0), (0, 0), (0, LANE - QK_ROPE))).astype(BF16)
            kpe_s = jnp.concatenate([cache_pe, kpe_rot.reshape(DEC_BATCH, DEC_SEQ, LANE)], axis=1)
            k_s, v_s = _kv_expand(ckv_s.reshape(DEC_BATCH * lk, KV_RANK), kpe_s.reshape(DEC_BATCH * lk, LANE), wk, wvt)
            o_s = _attention(q, k_s, v_s, DEC_BATCH, DEC_SEQ, lk, **ATTN_SAMPLE)
            for n, (st, o) in enumerate(zip(streams, (o_p, o_s))):
                xs[n], hs[n] = _attn_out(st, o, xs[n], mod, bf["out"], n2, i)
        nxt = big_weights(i + 1) if i + 1 < DEPTH else None
        bf_next = {}
        for n, st in enumerate(streams):
            names = cast_split[n] if nxt else ()
            xs[n], cast = _mlp(st, hs[n], xs[n], mod, bf["w1"], bf["w2"], final_g, i, final=(i == DEPTH - 1),
                               casts=[nxt[name] for name in names])
            bf_next.update(zip(names, cast))
        bf = bf_next
    y_prompt = xs[0].reshape(BATCH, SEQ, D)
    y_sample = xs[1].reshape(DEC_BATCH, DEC_SEQ, D)
    return (y_prompt, y_sample, jnp.stack(new_ckv, axis=1), jnp.stack(new_kpe, axis=1))
```

```python
import functools
from typing import NamedTuple

import jax
import jax.numpy as jnp
import numpy as np
from jax import lax
from jax.experimental import pallas as pl
from jax.experimental.pallas import tpu as pltpu

D_MODEL = 2048
BATCH = 32
SEQ = 256
DEPTH = 4
DEC_BATCH = 4
DEC_SEQ = 2048
PAST_LEN = 512
GRID_W = 64
N_MIXERS = 3
CONV_WIDTH = 3
CHUNK = 128
GMLP_GROUPS = 16
N_HEADS = 16
QK_NOPE = 128
QK_ROPE = 64
V_DIM = 128
Q_RANK = 512
KV_RANK = 512
ROPE_THETA = 10000.0
D_FF = 4 * D_MODEL
N_MOD = 6
EPS = 1e-6
LOG2_E = 1.4426950408889634

N_GROUPS = 8
HEAD_PAD = 256
LANE = 128
BF16_ROWS = 16
SUB = 256
VMEM_LIMIT = 56 * 1024 * 1024
ATTN_PROMPT = dict(heads=16, tq=256, ahead=16)
ATTN_SAMPLE = dict(heads=4, tq=1024, ahead=1)

BF16 = jnp.bfloat16
F32 = jnp.float32


class Stream(NamedTuple):
    rows: int
    seq: int
    group0: int
    group_rows: int
    rope: bool


PROMPT = Stream(BATCH * SEQ, SEQ, 0, BATCH * SEQ, False)
SAMPLE = Stream(DEC_BATCH * DEC_SEQ, DEC_SEQ, 1, DEC_SEQ, True)


def _params(*sem):
    return pltpu.CompilerParams(dimension_semantics=sem, vmem_limit_bytes=VMEM_LIMIT)


def _dot(a, b):
    return jnp.dot(a, b, preferred_element_type=F32)


def _resident(shape, layer=None):
    if layer is None:
        zeros = (0,) * len(shape)
        return pl.BlockSpec(shape, lambda *_: zeros, pipeline_mode=pl.Buffered(1))
    index = (layer,) + (0,) * len(shape)
    return pl.BlockSpec((None,) + tuple(shape), lambda *_: index, pipeline_mode=pl.Buffered(1))


def _mod_spec(st, layer, chunk, tm):
    def index(i, *_):
        return ((layer * N_MOD + chunk) * N_GROUPS + st.group0 + (i * tm) // st.group_rows, 0, 0)

    return pl.BlockSpec((1, 1, D_MODEL), index)


def _gain_shift(g_ref, sh_ref, sc_ref):
    return g_ref[...] * (1.0 + sc_ref[0]), sh_ref[0]


def _norm_mod(x, gsc, sh):
    ms = jnp.mean(x * x, axis=-1, keepdims=True)
    return (x * lax.rsqrt(ms + EPS) * gsc + sh).astype(BF16)


def _modvec_kernel(c_ref, w_ref, b_ref, o_ref):
    s = jax.nn.silu(c_ref[...]).astype(BF16)
    o_ref[0, 0] = _dot(s, w_ref[0].astype(BF16)) + b_ref[0]


def _modvec(cond, ada_w, ada_b, tn=1024):
    per_chunk = D_MODEL // tn
    out = pl.pallas_call(
        _modvec_kernel,
        grid=(DEPTH, N_MOD * per_chunk),
        in_specs=[
            pl.BlockSpec((N_GROUPS, D_MODEL), lambda l, j: (0, 0)),
            pl.BlockSpec((1, D_MODEL, tn), lambda l, j: (l, 0, j)),
            pl.BlockSpec((1, 1, tn), lambda l, j: (l, 0, j)),
        ],
        out_specs=pl.BlockSpec((1, 1, N_GROUPS, tn), lambda l, j: (l, j // per_chunk, 0, j % per_chunk)),
        out_shape=jax.ShapeDtypeStruct((DEPTH, N_MOD, N_GROUPS, D_MODEL), F32),
        compiler_params=_params("arbitrary", "arbitrary"),
        name="modvec",
    )(cond, ada_w, ada_b.reshape(DEPTH, 1, N_MOD * D_MODEL))
    return out.reshape(DEPTH * N_MOD * N_GROUPS, 1, D_MODEL)


class HalfCast(NamedTuple):
    w: jax.Array
    layer: int
    part: int
    prev: jax.Array | None


def _half_cast_specs(cast, steps):
    _, r, c = cast.w.shape
    rows, off = r // (2 * steps), cast.part * steps
    in_specs = [pl.BlockSpec((None, rows, c), lambda i: (cast.layer, off + i, 0))]
    args = [cast.w]
    if cast.prev is not None:
        in_specs.append(pl.BlockSpec(memory_space=pl.ANY))
        args.append(cast.prev)
    return in_specs, args, pl.BlockSpec((rows, c), lambda i: (off + i, 0)), jax.ShapeDtypeStruct((r, c), BF16)


def _conv_in_kernel(x_ref, g_ref, sh_ref, sc_ref, w_ref, *rest, tm, cast):
    bg_ref, p_ref = rest[-3:-1] if cast else rest
    gsc, sh = _gain_shift(g_ref, sh_ref, sc_ref)
    for s in range(tm // SUB):
        rs = slice(s * SUB, (s + 1) * SUB)
        z = _dot(_norm_mod(x_ref[rs, :], gsc, sh), w_ref[...])
        bg_ref[rs, :] = z[:, :D_MODEL].astype(BF16)
        p_ref[rs, :] = (z[:, D_MODEL:2 * D_MODEL] * z[:, 2 * D_MODEL:]).astype(BF16)
    if cast:
        rest[-1][...] = rest[0][...].astype(BF16)


def _conv_in(st, x, mod, norm1, w_in, layer, cast=None, tm=512):
    row = lambda i: (i, 0)
    in_specs = [
        pl.BlockSpec((tm, D_MODEL), row),
        _resident((1, D_MODEL), layer),
        _mod_spec(st, layer, 0, tm),
        _mod_spec(st, layer, 1, tm),
        _resident((D_MODEL, 3 * D_MODEL)),
    ]
    args = [x, norm1, mod, mod, w_in]
    out_specs = [pl.BlockSpec((tm, D_MODEL), row)] * 2
    out_shape = [jax.ShapeDtypeStruct((st.rows, D_MODEL), BF16)] * 2
    aliases = {}
    if cast is not None:
        c_in, c_args, c_out, c_shape = _half_cast_specs(cast, st.rows // tm)
        if cast.prev is not None:
            aliases = {len(in_specs) + 1: len(out_specs)}
        in_specs, args, out_specs, out_shape = in_specs + c_in, args + c_args, out_specs + [c_out], out_shape + [c_shape]
    return pl.pallas_call(
        functools.partial(_conv_in_kernel, tm=tm, cast=cast is not None),
        grid=(st.rows // tm,),
        in_specs=in_specs,
        out_specs=out_specs,
        out_shape=out_shape,
        input_output_aliases=aliases,
        compiler_params=_params("parallel"),
        name="conv_in",
    )(*args)


def _gmlp_in_kernel(x_ref, g_ref, sh_ref, sc_ref, w_ref, z_ref, *, tm):
    gsc, sh = _gain_shift(g_ref, sh_ref, sc_ref)
    for s in range(tm // SUB):
        rs = slice(s * SUB, (s + 1) * SUB)
        z_ref[rs, :] = _dot(_norm_mod(x_ref[rs, :], gsc, sh), w_ref[...]).astype(BF16)


def _gmlp_in(st, x, mod, norm1, w_in, layer, tm=512):
    n_out = w_in.shape[-1]
    row = lambda i: (i, 0)
    return pl.pallas_call(
        functools.partial(_gmlp_in_kernel, tm=tm),
        grid=(st.rows // tm,),
        in_specs=[
            pl.BlockSpec((tm, D_MODEL), row),
            _resident((1, D_MODEL), layer),
            _mod_spec(st, layer, 0, tm),
            _mod_spec(st, layer, 1, tm),
            _resident((D_MODEL, n_out)),
        ],
        out_specs=pl.BlockSpec((tm, n_out), row),
        out_shape=jax.ShapeDtypeStruct((st.rows, n_out), BF16),
        compiler_params=_params("parallel"),
        name="gmlp_in",
    )(x, norm1, mod, mod, w_in)


def _residual_proj(a, rs, x_ref, gate_ref, w_ref, gsc2, sh2, o_ref, h_ref):
    x1 = x_ref[rs, :] + gate_ref[0] * _dot(a, w_ref[...])
    o_ref[rs, :] = x1
    h_ref[rs, :] = _norm_mod(x1, gsc2, sh2)


def _proj_specs(st, layer, tm):
    row = lambda i: (i, 0)
    in_specs = [
        pl.BlockSpec((tm, D_MODEL), row),
        _mod_spec(st, layer, 2, tm),
        _resident((D_MODEL, D_MODEL)),
        _resident((1, D_MODEL), layer),
        _mod_spec(st, layer, 3, tm),
        _mod_spec(st, layer, 4, tm),
    ]
    out_specs = [pl.BlockSpec((tm, D_MODEL), row)] * 2
    out_shape = [jax.ShapeDtypeStruct((st.rows, D_MODEL), F32), jax.ShapeDtypeStruct((st.rows, D_MODEL), BF16)]
    return in_specs, out_specs, out_shape


def _conv_out_kernel(bg_ref, p_ref, lo_ref, hi_ref, cw_ref, x_ref, gate_ref, w_ref, g2_ref, sh2_ref, sc2_ref,
                     *rest, tm, seq, cast):
    if cast:
        o_ref, h_ref, dst_ref, pbuf_ref = rest[-4:]
        dst_ref[...] = rest[0][...].astype(BF16)
    else:
        o_ref, h_ref, pbuf_ref = rest
    halo = BF16_ROWS
    n_sub = tm // SUB
    gsc2, sh2 = _gain_shift(g2_ref, sh2_ref, sc2_ref)
    base = pl.program_id(0) * tm
    width = 4 * LANE
    for s in range(n_sub):
        r0 = s * SUB
        before = lo_ref[...] if s == 0 else p_ref[r0 - halo:r0, :]
        after = hi_ref[...] if s == n_sub - 1 else p_ref[r0 + SUB:r0 + SUB + halo, :]
        pbuf_ref[s, 0:halo, :] = jnp.where((base + r0) % seq == 0, 0.0, before.astype(F32))
        pbuf_ref[s, halo:halo + SUB, :] = p_ref[r0:r0 + SUB, :].astype(F32)
        pbuf_ref[s, halo + SUB:, :] = jnp.where((base + r0 + SUB) % seq == 0, 0.0, after.astype(F32))
        parts = []
        for lo in range(0, D_MODEL, width):
            cols = slice(lo, lo + width)
            prev = pbuf_ref[s, halo - 1:halo - 1 + SUB, cols]
            cur = pbuf_ref[s, halo:halo + SUB, cols]
            nxt = pbuf_ref[s, halo + 1:halo + 1 + SUB, cols]
            conv = prev * cw_ref[0:1, cols] + cur * cw_ref[1:2, cols] + nxt * cw_ref[2:3, cols]
            parts.append((bg_ref[r0:r0 + SUB, cols].astype(F32) * conv).astype(BF16))
        _residual_proj(jnp.concatenate(parts, axis=1), slice(r0, r0 + SUB), x_ref, gate_ref, w_ref, gsc2, sh2,
                       o_ref, h_ref)


def _conv_out(st, bg, p, conv_w, x, mod, w_out, norm2, layer, j, cast=None, tm=512):
    assert st.seq % SUB == 0 and tm % SUB == 0
    nb = tm // BF16_ROWS
    last_halo = st.rows // BF16_ROWS - 1
    row = lambda i: (i, 0)
    tail, out_specs, out_shape = _proj_specs(st, layer, tm)
    in_specs = [
        pl.BlockSpec((tm, D_MODEL), row),
        pl.BlockSpec((tm, D_MODEL), row),
        pl.BlockSpec((BF16_ROWS, D_MODEL), lambda i: (jnp.maximum(i * nb - 1, 0), 0)),
        pl.BlockSpec((BF16_ROWS, D_MODEL), lambda i: (jnp.minimum((i + 1) * nb, last_halo), 0)),
        _resident((CONV_WIDTH, D_MODEL), j),
    ] + tail
    args = [bg, p, p, p, conv_w, x, mod, w_out, norm2, mod, mod]
    aliases = {}
    if cast is not None:
        c_in, c_args, c_out, c_shape = _half_cast_specs(cast, st.rows // tm)
        if cast.prev is not None:
            aliases = {len(in_specs) + 1: len(out_specs)}
        in_specs, args, out_specs, out_shape = in_specs + c_in, args + c_args, out_specs + [c_out], out_shape + [c_shape]
    return pl.pallas_call(
        functools.partial(_conv_out_kernel, tm=tm, seq=st.seq, cast=cast is not None),
        grid=(st.rows // tm,),
        in_specs=in_specs,
        out_specs=out_specs,
        out_shape=out_shape,
        scratch_shapes=[pltpu.VMEM((tm // SUB, SUB + 2 * BF16_ROWS, D_MODEL), F32)],
        input_output_aliases=aliases,
        compiler_params=_params("parallel"),
        name="conv_out",
    )(*args)


def _gmlp_out_kernel(zu_ref, zv_ref, gv_ref, ws_ref, bs_ref, x_ref, gate_ref, w_ref, g2_ref, sh2_ref, sc2_ref,
                     o_ref, h_ref, a_ref, vn_ref, *, tm):
    gv = gv_ref[...]
    gsc2, sh2 = _gain_shift(g2_ref, sh2_ref, sc2_ref)
    for s in range(tm // SUB):
        chunks = range(s * SUB // CHUNK, (s + 1) * SUB // CHUNK)
        for c in chunks:
            rs = slice(c * CHUNK, (c + 1) * CHUNK)
            v = zv_ref[rs, :].astype(F32)
            ms = jnp.mean(v * v, axis=-1, keepdims=True)
            vn_ref[rs, :] = (v * lax.rsqrt(ms + EPS) * gv).astype(BF16)
        for g in range(GMLP_GROUPS):
            cols = slice(g * LANE, (g + 1) * LANE)
            rhs = jnp.concatenate([vn_ref[c * CHUNK:(c + 1) * CHUNK, cols] for c in chunks], axis=1)
            mixed = _dot(ws_ref[g], rhs)
            for n, c in enumerate(chunks):
                rs = slice(c * CHUNK, (c + 1) * CHUNK)
                vv = mixed[:, n * CHUNK:(n + 1) * CHUNK] + bs_ref[:, cols]
                a_ref[rs, cols] = (zu_ref[rs, cols].astype(F32) * vv).astype(BF16)
        rs = slice(s * SUB, (s + 1) * SUB)
        _residual_proj(a_ref[rs, :], rs, x_ref, gate_ref, w_ref, gsc2, sh2, o_ref, h_ref)


def _gmlp_out(st, z, g_v, w_s, b_full, x, mod, w_out, norm2, layer, j, tm=512):
    tail, out_specs, out_shape = _proj_specs(st, layer, tm)
    return pl.pallas_call(
        functools.partial(_gmlp_out_kernel, tm=tm),
        grid=(st.rows // tm,),
        in_specs=[
            pl.BlockSpec((tm, D_MODEL), lambda i: (i, 0)),
            pl.BlockSpec((tm, D_MODEL), lambda i: (i, 1)),
            _resident((1, D_MODEL), j),
            _resident((GMLP_GROUPS, CHUNK, CHUNK)),
            _resident((CHUNK, D_MODEL)),
        ] + tail,
        out_specs=out_specs,
        out_shape=out_shape,
        scratch_shapes=[pltpu.VMEM((tm, D_MODEL), BF16), pltpu.VMEM((tm, D_MODEL), BF16)],
        compiler_params=_params("parallel"),
        name="gmlp_out",
    )(z, z, g_v, w_s, b_full, x, mod, w_out, norm2, mod, mod)


def _attn_out_kernel(a_ref, x_ref, gate_ref, w_ref, g2_ref, sh2_ref, sc2_ref, o_ref, h_ref, *, tm):
    gsc2, sh2 = _gain_shift(g2_ref, sh2_ref, sc2_ref)
    for s in range(tm // SUB):
        rs = slice(s * SUB, (s + 1) * SUB)
        _residual_proj(a_ref[rs, :], rs, x_ref, gate_ref, w_ref, gsc2, sh2, o_ref, h_ref)


def _attn_out(st, a, x, mod, w_o, norm2, layer, tm=512):
    tail, out_specs, out_shape = _proj_specs(st, layer, tm)
    return pl.pallas_call(
        functools.partial(_attn_out_kernel, tm=tm),
        grid=(st.rows // tm,),
        in_specs=[pl.BlockSpec((tm, D_MODEL), lambda i: (i, 0))] + tail,
        out_specs=out_specs,
        out_shape=out_shape,
        compiler_params=_params("parallel"),
        name="attn_out",
    )(a, x, mod, w_o, norm2, mod, mod)


def _mla_proj_kernel(x_ref, g_ref, sh_ref, sc_ref, wqa_ref, gq_ref, wqn_ref, wqr_ref, wqrs_ref, wkv_ref, gkv_ref,
                     cos_ref, sin_ref, q_ref, ckv_ref, kpe_ref, kper_ref, *, tm, rope):
    gsc, sh = _gain_shift(g_ref, sh_ref, sc_ref)
    gq = gq_ref[...] * ((QK_NOPE + QK_ROPE) ** -0.5 * LOG2_E)
    first_half = lax.broadcasted_iota(jnp.int32, (1, LANE), 1) < QK_ROPE
    for s in range(tm // SUB):
        rs = slice(s * SUB, (s + 1) * SUB)
        h = _norm_mod(x_ref[rs, :], gsc, sh)
        qa = _dot(h, wqa_ref[...])
        qa = qa * lax.rsqrt(jnp.mean(qa * qa, axis=-1, keepdims=True) + EPS) * gq
        qa = qa.astype(BF16)
        qn = _dot(qa, wqn_ref[...])
        qr = _dot(qa, wqr_ref[...])
        kv = _dot(h, wkv_ref[...])
        c = kv[:, :KV_RANK]
        ckv_ref[rs, :] = c * lax.rsqrt(jnp.mean(c * c, axis=-1, keepdims=True) + EPS) * gkv_ref[...]
        kpe2 = kv[:, KV_RANK:]
        kpe = jnp.where(first_half, kpe2, 0.0)
        kpe_ref[rs, :] = kpe
        if rope:
            cos = cos_ref[rs, :]
            sin = sin_ref[rs, :]
            qrs = _dot(qa, wqrs_ref[...])
            rot = kpe2 * cos + pltpu.roll(kpe2, QK_ROPE, axis=1) * sin
            kper_ref[rs, :] = jnp.where(first_half, rot, 0.0).astype(BF16)
        else:
            kper_ref[rs, :] = kpe.astype(BF16)
        for pair in range(N_HEADS // 2):
            tile = slice(pair * LANE, (pair + 1) * LANE)
            rot = qr[:, tile] * cos + qrs[:, tile] * sin if rope else qr[:, tile]
            halves = (rot, pltpu.roll(rot, QK_ROPE, axis=1))
            for hd, val in zip((2 * pair, 2 * pair + 1), halves):
                lo = hd * HEAD_PAD
                q_ref[rs, lo:lo + LANE] = qn[:, hd * LANE:(hd + 1) * LANE].astype(BF16)
                q_ref[rs, lo + LANE:lo + HEAD_PAD] = jnp.where(first_half, val, 0.0).astype(BF16)


def _mla_proj(st, x, mod, norm1, wqa, gq, wqn, wqr, wqrs, wkv, gkv, cos_t, sin_t, layer, j, tm=512):
    per_seq = st.seq // tm
    rope_idx = (lambda i: (i % per_seq, 0)) if st.rope else (lambda i: (0, 0))
    row = lambda i: (i, 0)
    return pl.pallas_call(
        functools.partial(_mla_proj_kernel, tm=tm, rope=st.rope),
        grid=(st.rows // tm,),
        in_specs=[
            pl.BlockSpec((tm, D_MODEL), row),
            _resident((1, D_MODEL), layer),
            _mod_spec(st, layer, 0, tm),
            _mod_spec(st, layer, 1, tm),
            _resident(wqa.shape),
            _resident((1, Q_RANK), j),
            _resident(wqn.shape),
            _resident(wqr.shape),
            _resident(wqrs.shape),
            _resident(wkv.shape),
            _resident((1, KV_RANK), j),
            pl.BlockSpec((tm, LANE), rope_idx),
            pl.BlockSpec((tm, LANE), rope_idx),
        ],
        out_specs=[
            pl.BlockSpec((tm, N_HEADS * HEAD_PAD), row),
            pl.BlockSpec((tm, KV_RANK), row),
            pl.BlockSpec((tm, LANE), row),
            pl.BlockSpec((tm, LANE), row),
        ],
        out_shape=[
            jax.ShapeDtypeStruct((st.rows, N_HEADS * HEAD_PAD), BF16),
            jax.ShapeDtypeStruct((st.rows, KV_RANK), F32),
            jax.ShapeDtypeStruct((st.rows, LANE), F32),
            jax.ShapeDtypeStruct((st.rows, LANE), BF16),
        ],
        compiler_params=_params("parallel"),
        name="mla_proj",
    )(x, norm1, mod, mod, wqa, gq, wqn, wqr, wqrs, wkv, gkv, cos_t, sin_t)


def _dot_nt(a, b):
    return lax.dot_general(a, b, (((1,), (1,)), ((), ())), preferred_element_type=F32)


def _kv_expand_kernel(*refs, past_blocks, seq_blocks):
    c_ref, kpe_ref, wk_ref, wvt_ref, k_ref, vt_ref = refs[-6:]
    c, kpe = c_ref[...], kpe_ref[...]
    if past_blocks:
        cached = pl.program_id(0) % (past_blocks + seq_blocks) < past_blocks
        c = jnp.where(cached, refs[0][...], c)
        kpe = jnp.where(cached, refs[1][...], kpe)
    c = c.astype(BF16)
    kn = _dot(c, wk_ref[...])
    vt_ref[...] = _dot_nt(wvt_ref[...], c).astype(BF16)
    for hd in range(N_HEADS):
        lo = hd * HEAD_PAD
        k_ref[:, lo:lo + LANE] = kn[:, hd * LANE:(hd + 1) * LANE].astype(BF16)
        k_ref[:, lo + LANE:lo + HEAD_PAD] = kpe


def _kv_expand(ckv, kpe, wk, wvt, seq=None, past=None, tr=512):
    seq = ckv.shape[0] if seq is None else seq
    n_req = ckv.shape[0] // seq
    seq_blocks = seq // tr
    past_blocks = 0 if past is None else past[0].shape[0] // n_req // tr
    per_req = past_blocks + seq_blocks
    n_rows = n_req * per_req * tr
    row = lambda i: (i, 0)
    new_row = lambda i: ((i // per_req) * seq_blocks + jnp.maximum(i % per_req - past_blocks, 0), 0)
    past_row = lambda i: ((i // per_req) * past_blocks + jnp.minimum(i % per_req, past_blocks - 1), 0)
    past_specs = [] if past is None else [pl.BlockSpec((tr, KV_RANK), past_row), pl.BlockSpec((tr, LANE), past_row)]
    return pl.pallas_call(
        functools.partial(_kv_expand_kernel, past_blocks=past_blocks, seq_blocks=seq_blocks),
        grid=(n_rows // tr,),
        in_specs=past_specs + [
            pl.BlockSpec((tr, KV_RANK), new_row),
            pl.BlockSpec((tr, LANE), new_row),
            _resident(wk.shape),
            _resident(wvt.shape),
        ],
        out_specs=[
            pl.BlockSpec((tr, N_HEADS * HEAD_PAD), row),
            pl.BlockSpec((N_HEADS * V_DIM, tr), lambda i: (0, i)),
        ],
        out_shape=[
            jax.ShapeDtypeStruct((n_rows, N_HEADS * HEAD_PAD), BF16),
            jax.ShapeDtypeStruct((N_HEADS * V_DIM, n_rows), BF16),
        ],
        compiler_params=_params("parallel"),
        name="kv_expand",
    )(*(past or ()), ckv, kpe, wk, wvt)


def _attn_kernel(q_ref, k_ref, vt_ref, o_ref, *, heads, ahead):
    ones = jnp.ones((BF16_ROWS, k_ref.shape[0]), BF16)

    def scores(hd):
        cols = slice(hd * HEAD_PAD, (hd + 1) * HEAD_PAD)
        return _dot_nt(k_ref[:, cols], q_ref[:, cols])

    pending = [scores(hd) for hd in range(min(ahead, heads))]
    for hd in range(heads):
        st = pending.pop(0)
        if hd + ahead < heads:
            pending.append(scores(hd + ahead))
        p = jnp.exp2(st - jnp.max(st, axis=0, keepdims=True)).astype(BF16)
        lhs = jnp.concatenate([vt_ref[hd * V_DIM:(hd + 1) * V_DIM, :], ones], axis=0)
        ot = _dot(lhs, p)
        o_ref[:, hd * V_DIM:(hd + 1) * V_DIM] = (ot[:V_DIM] / ot[V_DIM:V_DIM + 1]).T.astype(BF16)


def _attention(q, k, vt, n_batch, lq, lk, heads, tq, ahead):
    nq = lq // tq
    return pl.pallas_call(
        functools.partial(_attn_kernel, heads=heads, ahead=ahead),
        grid=(n_batch, N_HEADS // heads, nq),
        in_specs=[
            pl.BlockSpec((tq, heads * HEAD_PAD), lambda b, g, i: (b * nq + i, g)),
            pl.BlockSpec((lk, heads * HEAD_PAD), lambda b, g, i: (b, g)),
            pl.BlockSpec((heads * V_DIM, lk), lambda b, g, i: (g, b)),
        ],
        out_specs=pl.BlockSpec((tq, heads * V_DIM), lambda b, g, i: (b * nq + i, g)),
        out_shape=jax.ShapeDtypeStruct((n_batch * lq, N_HEADS * V_DIM), BF16),
        compiler_params=_params("parallel", "parallel", "arbitrary"),
        name="attention",
    )(q, k, vt)


def _mlp_kernel(h_ref, x_ref, gate_ref, w1_ref, w2_ref, gf_ref, *rest, tm, final, n_cast):
    src_refs, o_ref, dst_refs = rest[:n_cast], rest[n_cast], rest[n_cast + 1:]
    f = pl.program_id(1)

    @pl.when(f == 0)
    def _():
        o_ref[...] = x_ref[...]

    hid = _dot(h_ref[...], w1_ref[...])
    hid = jnp.square(jnp.maximum(hid, 0.0)).astype(BF16)
    o_ref[...] += gate_ref[0] * _dot(hid, w2_ref[...])

    for src_ref, dst_ref in zip(src_refs, dst_refs):
        dst_ref[...] = src_ref[...].astype(BF16)

    if final:
        @pl.when(f == pl.num_programs(1) - 1)
        def _():
            gf = gf_ref[...]
            for s in range(tm // SUB):
                rs = slice(s * SUB, (s + 1) * SUB)
                y = o_ref[rs, :]
                o_ref[rs, :] = y * lax.rsqrt(jnp.mean(y * y, axis=-1, keepdims=True) + EPS) * gf


def _mlp(st, h, x, mod, w1, w2, final_g, layer, final, casts=(), tm=512, tf=1024):
    row = lambda i, f: (i, 0)
    nf = D_FF // tf
    steps = (st.rows // tm) * nf
    slab = lambda i, f: (i * nf + f, 0)
    cast_in, cast_out, cast_shape = [], [], []
    for w, l in casts:
        _, r, c = w.shape
        cast_in.append(pl.BlockSpec((None, r // steps, c), lambda i, f, l=l: (l, i * nf + f, 0)))
        cast_out.append(pl.BlockSpec((r // steps, c), slab))
        cast_shape.append(jax.ShapeDtypeStruct((r, c), BF16))
    out = pl.pallas_call(
        functools.partial(_mlp_kernel, tm=tm, final=final, n_cast=len(casts)),
        grid=(st.rows // tm, nf),
        in_specs=[
            pl.BlockSpec((tm, D_MODEL), row),
            pl.BlockSpec((tm, D_MODEL), row),
            _mod_spec(st, layer, 5, tm),
            pl.BlockSpec((D_MODEL, tf), lambda i, f: (0, f)),
            pl.BlockSpec((tf, D_MODEL), lambda i, f: (f, 0)),
            _resident((1, D_MODEL)),
        ] + cast_in,
        out_specs=[pl.BlockSpec((tm, D_MODEL), row)] + cast_out,
        out_shape=[jax.ShapeDtypeStruct((st.rows, D_MODEL), F32)] + cast_shape,
        compiler_params=_params("parallel", "arbitrary"),
        name="mlp",
    )(h, x, mod, w1, w2, final_g, *[w for w, _ in casts])
    return out[0], out[1:]


def _rope_tables():
    t = np.arange(DEC_SEQ)
    nf = QK_ROPE // 4
    inv = (1.0 / (np.float32(ROPE_THETA) ** (np.arange(nf, dtype=np.float32) / np.float32(nf)))).astype(np.float32)
    ang_r = (t // GRID_W).astype(np.float32)[:, None] * inv
    ang_c = (t % GRID_W).astype(np.float32)[:, None] * inv
    cr, sr, cc, sc = np.cos(ang_r), np.sin(ang_r), np.cos(ang_c), np.sin(ang_c)
    cos = np.tile(np.concatenate([cr, cr, cc, cc], axis=1), (1, LANE // QK_ROPE))
    sin = np.tile(np.concatenate([-sr, sr, -sc, sc], axis=1), (1, LANE // QK_ROPE))
    return jnp.asarray(cos, F32), jnp.asarray(sin, F32)


def _swap_rope_partners(w):
    q = QK_ROPE // 4
    return jnp.concatenate([w[..., q:2 * q], w[..., :q], w[..., 3 * q:], w[..., 2 * q:3 * q]], axis=-1)


def _mla_weights(w_q_b, w_kv_a, w_kv_b):
    wq = w_q_b.reshape(Q_RANK, N_HEADS, QK_NOPE + QK_ROPE)
    wq_nope = wq[..., :QK_NOPE].reshape(Q_RANK, N_HEADS * QK_NOPE)
    wq_rope = wq[..., QK_NOPE:].reshape(Q_RANK, N_HEADS * QK_ROPE)
    wq_rope_swap = _swap_rope_partners(wq[..., QK_NOPE:]).reshape(Q_RANK, N_HEADS * QK_ROPE)
    kpe_w = w_kv_a[:, KV_RANK:]
    wkv = jnp.concatenate([w_kv_a[:, :KV_RANK], kpe_w, _swap_rope_partners(kpe_w)], axis=1)
    wkvb = w_kv_b.reshape(KV_RANK, N_HEADS, QK_NOPE + V_DIM)
    wk = wkvb[..., :QK_NOPE].reshape(KV_RANK, N_HEADS * QK_NOPE)
    wvt = wkvb[..., QK_NOPE:].reshape(KV_RANK, N_HEADS * V_DIM).T
    casts = (wq_nope, wq_rope, wq_rope_swap, wkv, wk, wvt)
    return tuple(w.astype(BF16) for w in casts)


def kernel(x_prompt, x_sample, cache_ckv, cache_kpe, c, c_ctx, ada_w, ada_b, norm1, norm2, conv_w_in, conv_w, conv_w_out, gmlp_w_in, gmlp_g_v, gmlp_w_s, gmlp_b_s, gmlp_w_out, mla_w_q_a, mla_g_q, mla_w_q_b, mla_w_kv_a, mla_g_kv, mla_w_kv_b, mla_w_o, mlp_w1, mlp_w2, final_norm):
    D = D_MODEL
    streams = (PROMPT, SAMPLE)
    xs = [x_prompt.reshape(PROMPT.rows, D), x_sample.reshape(SAMPLE.rows, D)]
    cond = jnp.concatenate([c_ctx[None, :], c, jnp.zeros((N_GROUPS - 1 - DEC_BATCH, D), F32)], axis=0)
    mod = _modvec(cond, ada_w, ada_b)

    n1, n2 = norm1[:, None, :], norm2[:, None, :]
    final_g = final_norm[None, :]

    def big_weights(i):
        mix_in, mix_out = ((conv_w_in, conv_w_out), (gmlp_w_in, gmlp_w_out), (mla_w_q_a, mla_w_o))[i % N_MIXERS]
        return {"w1": (mlp_w1, i), "in": (mix_in, i // N_MIXERS), "w2": (mlp_w2, i), "out": (mix_out, i // N_MIXERS)}

    cast_split = (("w1", "in"), ("w2", "out"))
    bf = {name: w[l].astype(BF16) for name, (w, l) in big_weights(0).items() if name in ("in", "out")}

    cos_t, sin_t = _rope_tables()
    new_ckv, new_kpe = [], []
    for i in range(DEPTH):
        kind, j = i % N_MIXERS, i // N_MIXERS
        hs = [None, None]
        if kind == 0:
            host = "w1" not in bf
            for n, st in enumerate(streams):
                cast = HalfCast(mlp_w1, i, n, bf.get("w1")) if host else None
                bg, p, *half = _conv_in(st, xs[n], mod, n1, bf["in"], i, cast=cast)
                bf.update(zip(("w1",), half))
                cast = HalfCast(mlp_w2, i, n, bf.get("w2")) if host else None
                xs[n], hs[n], *half = _conv_out(st, bg, p, conv_w, xs[n], mod, bf["out"], n2, i, j, cast=cast)
                bf.update(zip(("w2",), half))
        elif kind == 1:
            b_full = jnp.repeat(gmlp_b_s[j].T, LANE, axis=1)
            w_s = gmlp_w_s[j].astype(BF16)
            for n, st in enumerate(streams):
                z = _gmlp_in(st, xs[n], mod, n1, bf["in"], i)
                xs[n], hs[n] = _gmlp_out(st, z, gmlp_g_v[:, None, :], w_s, b_full, xs[n], mod, bf["out"], n2, i, j)
        else:
            wqn, wqr, wqrs, wkv, wk, wvt = _mla_weights(mla_w_q_b[j], mla_w_kv_a[j], mla_w_kv_b[j])
            proj = [_mla_proj(st, xs[n], mod, n1, bf["in"], mla_g_q[:, None, :], wqn, wqr, wqrs, wkv, mla_g_kv[:, None, :],
                              cos_t, sin_t, i, j) for n, st in enumerate(streams)]
            q, ckv, kpe, kpe_rot = proj[0]
            new_ckv.append(ckv.reshape(BATCH, SEQ, KV_RANK))
            new_kpe.append(kpe[:, :QK_ROPE].reshape(BATCH, SEQ, QK_ROPE))
            k_p, v_p = _kv_expand(ckv, kpe_rot, wk, wvt)
            o_p = _attention(q, k_p, v_p, BATCH, SEQ, SEQ, **ATTN_PROMPT)
            q, ckv, kpe, kpe_rot = proj[1]
            lk = PAST_LEN + DEC_SEQ
            cache_pe = jnp.pad(cache_kpe[:, j], ((0, 0), (0, 0), (0, LANE - QK_ROPE))).astype(BF16)
            past = (cache_ckv[:, j].reshape(DEC_BATCH * PAST_LEN, KV_RANK), cache_pe.reshape(DEC_BATCH * PAST_LEN, LANE))
            k_s, v_s = _kv_expand(ckv, kpe_rot, wk, wvt, seq=DEC_SEQ, past=past)
            o_s = _attention(q, k_s, v_s, DEC_BATCH, DEC_SEQ, lk, **ATTN_SAMPLE)
            for n, (st, o) in enumerate(zip(streams, (o_p, o_s))):
                xs[n], hs[n] = _attn_out(st, o, xs[n], mod, bf["out"], n2, i)
        nxt = big_weights(i + 1) if i + 1 < DEPTH else None
        bf_next = {}
        for n, st in enumerate(streams):
            names = cast_split[n] if nxt else ()
            xs[n], cast = _mlp(st, hs[n], xs[n], mod, bf["w1"], bf["w2"], final_g, i, final=(i == DEPTH - 1),
                               casts=[nxt[name] for name in names])
            bf_next.update(zip(names, cast))
        bf = bf_next
    y_prompt = xs[0].reshape(BATCH, SEQ, D)
    y_sample = xs[1].reshape(DEC_BATCH, DEC_SEQ, D)
    return (y_prompt, y_sample, jnp.stack(new_ckv, axis=1), jnp.stack(new_kpe, axis=1))
```

```python
import functools
from typing import NamedTuple

import jax
import jax.numpy as jnp
import numpy as np
from jax import lax
from jax.experimental import pallas as pl
from jax.experimental.pallas import tpu as pltpu

D_MODEL = 2048
BATCH = 32
SEQ = 256
DEPTH = 4
DEC_BATCH = 4
DEC_SEQ = 2048
PAST_LEN = 512
GRID_W = 64
N_MIXERS = 3
CONV_WIDTH = 3
CHUNK = 128
GMLP_GROUPS = 16
N_HEADS = 16
QK_NOPE = 128
QK_ROPE = 64
V_DIM = 128
Q_RANK = 512
KV_RANK = 512
ROPE_THETA = 10000.0
D_FF = 4 * D_MODEL
N_MOD = 6
EPS = 1e-6
LOG2_E = 1.4426950408889634

N_GROUPS = 8
HEAD_PAD = 256
LANE = 128
BF16_ROWS = 16
SUB = 256
VMEM_LIMIT = 56 * 1024 * 1024
ATTN_SAMPLE = dict(heads=4, tq=1024, ahead=1)
CTX_AHEAD = 16

BF16 = jnp.bfloat16
F32 = jnp.float32


class Stream(NamedTuple):
    rows: int
    seq: int
    group0: int
    group_rows: int
    rope: bool


PROMPT = Stream(BATCH * SEQ, SEQ, 0, BATCH * SEQ, False)
SAMPLE = Stream(DEC_BATCH * DEC_SEQ, DEC_SEQ, 1, DEC_SEQ, True)


def _params(*sem):
    return pltpu.CompilerParams(dimension_semantics=sem, vmem_limit_bytes=VMEM_LIMIT)


def _dot(a, b):
    return jnp.dot(a, b, preferred_element_type=F32)


def _resident(shape, layer=None):
    if layer is None:
        zeros = (0,) * len(shape)
        return pl.BlockSpec(shape, lambda *_: zeros, pipeline_mode=pl.Buffered(1))
    index = (layer,) + (0,) * len(shape)
    return pl.BlockSpec((None,) + tuple(shape), lambda *_: index, pipeline_mode=pl.Buffered(1))


def _mod_spec(st, layer, chunk, tm):
    def index(i, *_):
        return ((layer * N_MOD + chunk) * N_GROUPS + st.group0 + (i * tm) // st.group_rows, 0, 0)

    return pl.BlockSpec((1, 1, D_MODEL), index)


def _gain_shift(g_ref, sh_ref, sc_ref):
    return g_ref[...] * (1.0 + sc_ref[0]), sh_ref[0]


def _norm_mod(x, gsc, sh):
    ms = jnp.mean(x * x, axis=-1, keepdims=True)
    return (x * lax.rsqrt(ms + EPS) * gsc + sh).astype(BF16)


def _modvec_kernel(c_ref, w_ref, b_ref, o_ref):
    s = jax.nn.silu(c_ref[...]).astype(BF16)
    o_ref[0, 0] = _dot(s, w_ref[0].astype(BF16)) + b_ref[0]


def _modvec(cond, ada_w, ada_b, tn=1024):
    per_chunk = D_MODEL // tn
    out = pl.pallas_call(
        _modvec_kernel,
        grid=(DEPTH, N_MOD * per_chunk),
        in_specs=[
            pl.BlockSpec((N_GROUPS, D_MODEL), lambda l, j: (0, 0)),
            pl.BlockSpec((1, D_MODEL, tn), lambda l, j: (l, 0, j)),
            pl.BlockSpec((1, 1, tn), lambda l, j: (l, 0, j)),
        ],
        out_specs=pl.BlockSpec((1, 1, N_GROUPS, tn), lambda l, j: (l, j // per_chunk, 0, j % per_chunk)),
        out_shape=jax.ShapeDtypeStruct((DEPTH, N_MOD, N_GROUPS, D_MODEL), F32),
        compiler_params=_params("arbitrary", "arbitrary"),
        name="modvec",
    )(cond, ada_w, ada_b.reshape(DEPTH, 1, N_MOD * D_MODEL))
    return out.reshape(DEPTH * N_MOD * N_GROUPS, 1, D_MODEL)


class HalfCast(NamedTuple):
    w: jax.Array
    layer: int
    part: int
    prev: jax.Array | None


def _half_cast_specs(cast, steps):
    _, r, c = cast.w.shape
    rows, off = r // (2 * steps), cast.part * steps
    in_specs = [pl.BlockSpec((None, rows, c), lambda i: (cast.layer, off + i, 0))]
    args = [cast.w]
    if cast.prev is not None:
        in_specs.append(pl.BlockSpec(memory_space=pl.ANY))
        args.append(cast.prev)
    return in_specs, args, pl.BlockSpec((rows, c), lambda i: (off + i, 0)), jax.ShapeDtypeStruct((r, c), BF16)


def _conv_in_kernel(x_ref, g_ref, sh_ref, sc_ref, w_ref, *rest, tm, cast):
    bg_ref, p_ref = rest[-3:-1] if cast else rest
    gsc, sh = _gain_shift(g_ref, sh_ref, sc_ref)
    for s in range(tm // SUB):
        rs = slice(s * SUB, (s + 1) * SUB)
        z = _dot(_norm_mod(x_ref[rs, :], gsc, sh), w_ref[...])
        bg_ref[rs, :] = z[:, :D_MODEL].astype(BF16)
        p_ref[rs, :] = (z[:, D_MODEL:2 * D_MODEL] * z[:, 2 * D_MODEL:]).astype(BF16)
    if cast:
        rest[-1][...] = rest[0][...].astype(BF16)


def _conv_in(st, x, mod, norm1, w_in, layer, cast=None, tm=512):
    row = lambda i: (i, 0)
    in_specs = [
        pl.BlockSpec((tm, D_MODEL), row),
        _resident((1, D_MODEL), layer),
        _mod_spec(st, layer, 0, tm),
        _mod_spec(st, layer, 1, tm),
        _resident((D_MODEL, 3 * D_MODEL)),
    ]
    args = [x, norm1, mod, mod, w_in]
    out_specs = [pl.BlockSpec((tm, D_MODEL), row)] * 2
    out_shape = [jax.ShapeDtypeStruct((st.rows, D_MODEL), BF16)] * 2
    aliases = {}
    if cast is not None:
        c_in, c_args, c_out, c_shape = _half_cast_specs(cast, st.rows // tm)
        if cast.prev is not None:
            aliases = {len(in_specs) + 1: len(out_specs)}
        in_specs, args, out_specs, out_shape = in_specs + c_in, args + c_args, out_specs + [c_out], out_shape + [c_shape]
    return pl.pallas_call(
        functools.partial(_conv_in_kernel, tm=tm, cast=cast is not None),
        grid=(st.rows // tm,),
        in_specs=in_specs,
        out_specs=out_specs,
        out_shape=out_shape,
        input_output_aliases=aliases,
        compiler_params=_params("parallel"),
        name="conv_in",
    )(*args)


def _gmlp_in_kernel(x_ref, g_ref, sh_ref, sc_ref, w_ref, z_ref, *, tm):
    gsc, sh = _gain_shift(g_ref, sh_ref, sc_ref)
    for s in range(tm // SUB):
        rs = slice(s * SUB, (s + 1) * SUB)
        z_ref[rs, :] = _dot(_norm_mod(x_ref[rs, :], gsc, sh), w_ref[...]).astype(BF16)


def _gmlp_in(st, x, mod, norm1, w_in, layer, tm=512):
    n_out = w_in.shape[-1]
    row = lambda i: (i, 0)
    return pl.pallas_call(
        functools.partial(_gmlp_in_kernel, tm=tm),
        grid=(st.rows // tm,),
        in_specs=[
            pl.BlockSpec((tm, D_MODEL), row),
            _resident((1, D_MODEL), layer),
            _mod_spec(st, layer, 0, tm),
            _mod_spec(st, layer, 1, tm),
            _resident((D_MODEL, n_out)),
        ],
        out_specs=pl.BlockSpec((tm, n_out), row),
        out_shape=jax.ShapeDtypeStruct((st.rows, n_out), BF16),
        compiler_params=_params("parallel"),
        name="gmlp_in",
    )(x, norm1, mod, mod, w_in)


def _residual_proj(a, rs, x_ref, gate_ref, w_ref, gsc2, sh2, o_ref, h_ref):
    x1 = x_ref[rs, :] + gate_ref[0] * _dot(a, w_ref[...])
    o_ref[rs, :] = x1
    h_ref[rs, :] = _norm_mod(x1, gsc2, sh2)


def _proj_specs(st, layer, tm):
    row = lambda i: (i, 0)
    in_specs = [
        pl.BlockSpec((tm, D_MODEL), row),
        _mod_spec(st, layer, 2, tm),
        _resident((D_MODEL, D_MODEL)),
        _resident((1, D_MODEL), layer),
        _mod_spec(st, layer, 3, tm),
        _mod_spec(st, layer, 4, tm),
    ]
    out_specs = [pl.BlockSpec((tm, D_MODEL), row)] * 2
    out_shape = [jax.ShapeDtypeStruct((st.rows, D_MODEL), F32), jax.ShapeDtypeStruct((st.rows, D_MODEL), BF16)]
    return in_specs, out_specs, out_shape


def _conv_out_kernel(bg_ref, p_ref, lo_ref, hi_ref, cw_ref, x_ref, gate_ref, w_ref, g2_ref, sh2_ref, sc2_ref,
                     *rest, tm, seq, cast):
    if cast:
        o_ref, h_ref, dst_ref, pbuf_ref = rest[-4:]
        dst_ref[...] = rest[0][...].astype(BF16)
    else:
        o_ref, h_ref, pbuf_ref = rest
    halo = BF16_ROWS
    n_sub = tm // SUB
    gsc2, sh2 = _gain_shift(g2_ref, sh2_ref, sc2_ref)
    base = pl.program_id(0) * tm
    width = 4 * LANE
    for s in range(n_sub):
        r0 = s * SUB
        before = lo_ref[...] if s == 0 else p_ref[r0 - halo:r0, :]
        after = hi_ref[...] if s == n_sub - 1 else p_ref[r0 + SUB:r0 + SUB + halo, :]
        pbuf_ref[s, 0:halo, :] = jnp.where((base + r0) % seq == 0, 0.0, before.astype(F32))
        pbuf_ref[s, halo:halo + SUB, :] = p_ref[r0:r0 + SUB, :].astype(F32)
        pbuf_ref[s, halo + SUB:, :] = jnp.where((base + r0 + SUB) % seq == 0, 0.0, after.astype(F32))
        parts = []
        for lo in range(0, D_MODEL, width):
            cols = slice(lo, lo + width)
            prev = pbuf_ref[s, halo - 1:halo - 1 + SUB, cols]
            cur = pbuf_ref[s, halo:halo + SUB, cols]
            nxt = pbuf_ref[s, halo + 1:halo + 1 + SUB, cols]
            conv = prev * cw_ref[0:1, cols] + cur * cw_ref[1:2, cols] + nxt * cw_ref[2:3, cols]
            parts.append((bg_ref[r0:r0 + SUB, cols].astype(F32) * conv).astype(BF16))
        _residual_proj(jnp.concatenate(parts, axis=1), slice(r0, r0 + SUB), x_ref, gate_ref, w_ref, gsc2, sh2,
                       o_ref, h_ref)


def _conv_out(st, bg, p, conv_w, x, mod, w_out, norm2, layer, j, cast=None, tm=512):
    assert st.seq % SUB == 0 and tm % SUB == 0
    nb = tm // BF16_ROWS
    last_halo = st.rows // BF16_ROWS - 1
    row = lambda i: (i, 0)
    tail, out_specs, out_shape = _proj_specs(st, layer, tm)
    in_specs = [
        pl.BlockSpec((tm, D_MODEL), row),
        pl.BlockSpec((tm, D_MODEL), row),
        pl.BlockSpec((BF16_ROWS, D_MODEL), lambda i: (jnp.maximum(i * nb - 1, 0), 0)),
        pl.BlockSpec((BF16_ROWS, D_MODEL), lambda i: (jnp.minimum((i + 1) * nb, last_halo), 0)),
        _resident((CONV_WIDTH, D_MODEL), j),
    ] + tail
    args = [bg, p, p, p, conv_w, x, mod, w_out, norm2, mod, mod]
    aliases = {}
    if cast is not None:
        c_in, c_args, c_out, c_shape = _half_cast_specs(cast, st.rows // tm)
        if cast.prev is not None:
            aliases = {len(in_specs) + 1: len(out_specs)}
        in_specs, args, out_specs, out_shape = in_specs + c_in, args + c_args, out_specs + [c_out], out_shape + [c_shape]
    return pl.pallas_call(
        functools.partial(_conv_out_kernel, tm=tm, seq=st.seq, cast=cast is not None),
        grid=(st.rows // tm,),
        in_specs=in_specs,
        out_specs=out_specs,
        out_shape=out_shape,
        scratch_shapes=[pltpu.VMEM((tm // SUB, SUB + 2 * BF16_ROWS, D_MODEL), F32)],
        input_output_aliases=aliases,
        compiler_params=_params("parallel"),
        name="conv_out",
    )(*args)


def _gmlp_out_kernel(zu_ref, zv_ref, gv_ref, ws_ref, bs_ref, x_ref, gate_ref, w_ref, g2_ref, sh2_ref, sc2_ref,
                     o_ref, h_ref, a_ref, vn_ref, *, tm):
    gv = gv_ref[...]
    gsc2, sh2 = _gain_shift(g2_ref, sh2_ref, sc2_ref)
    for s in range(tm // SUB):
        chunks = range(s * SUB // CHUNK, (s + 1) * SUB // CHUNK)
        for c in chunks:
            rs = slice(c * CHUNK, (c + 1) * CHUNK)
            v = zv_ref[rs, :].astype(F32)
            ms = jnp.mean(v * v, axis=-1, keepdims=True)
            vn_ref[rs, :] = (v * lax.rsqrt(ms + EPS) * gv).astype(BF16)
        for g in range(GMLP_GROUPS):
            cols = slice(g * LANE, (g + 1) * LANE)
            rhs = jnp.concatenate([vn_ref[c * CHUNK:(c + 1) * CHUNK, cols] for c in chunks], axis=1)
            mixed = _dot(ws_ref[g], rhs)
            for n, c in enumerate(chunks):
                rs = slice(c * CHUNK, (c + 1) * CHUNK)
                vv = mixed[:, n * CHUNK:(n + 1) * CHUNK] + bs_ref[:, cols]
                a_ref[rs, cols] = (zu_ref[rs, cols].astype(F32) * vv).astype(BF16)
        rs = slice(s * SUB, (s + 1) * SUB)
        _residual_proj(a_ref[rs, :], rs, x_ref, gate_ref, w_ref, gsc2, sh2, o_ref, h_ref)


def _gmlp_out(st, z, g_v, w_s, b_full, x, mod, w_out, norm2, layer, j, tm=512):
    tail, out_specs, out_shape = _proj_specs(st, layer, tm)
    return pl.pallas_call(
        functools.partial(_gmlp_out_kernel, tm=tm),
        grid=(st.rows // tm,),
        in_specs=[
            pl.BlockSpec((tm, D_MODEL), lambda i: (i, 0)),
            pl.BlockSpec((tm, D_MODEL), lambda i: (i, 1)),
            _resident((1, D_MODEL), j),
            _resident((GMLP_GROUPS, CHUNK, CHUNK)),
            _resident((CHUNK, D_MODEL)),
        ] + tail,
        out_specs=out_specs,
        out_shape=out_shape,
        scratch_shapes=[pltpu.VMEM((tm, D_MODEL), BF16), pltpu.VMEM((tm, D_MODEL), BF16)],
        compiler_params=_params("parallel"),
        name="gmlp_out",
    )(z, z, g_v, w_s, b_full, x, mod, w_out, norm2, mod, mod)


def _attn_out_kernel(a_ref, x_ref, gate_ref, w_ref, g2_ref, sh2_ref, sc2_ref, o_ref, h_ref, *, tm):
    gsc2, sh2 = _gain_shift(g2_ref, sh2_ref, sc2_ref)
    for s in range(tm // SUB):
        rs = slice(s * SUB, (s + 1) * SUB)
        _residual_proj(a_ref[rs, :], rs, x_ref, gate_ref, w_ref, gsc2, sh2, o_ref, h_ref)


def _attn_out(st, a, x, mod, w_o, norm2, layer, tm=512):
    tail, out_specs, out_shape = _proj_specs(st, layer, tm)
    return pl.pallas_call(
        functools.partial(_attn_out_kernel, tm=tm),
        grid=(st.rows // tm,),
        in_specs=[pl.BlockSpec((tm, D_MODEL), lambda i: (i, 0))] + tail,
        out_specs=out_specs,
        out_shape=out_shape,
        compiler_params=_params("parallel"),
        name="attn_out",
    )(a, x, mod, w_o, norm2, mod, mod)


def _mla_proj_kernel(x_ref, g_ref, sh_ref, sc_ref, wqa_ref, gq_ref, wqn_ref, wqr_ref, wqrs_ref, wkv_ref, gkv_ref,
                     cos_ref, sin_ref, q_ref, ckv_ref, kpe_ref, kper_ref, *, tm, rope):
    gsc, sh = _gain_shift(g_ref, sh_ref, sc_ref)
    gq = gq_ref[...] * ((QK_NOPE + QK_ROPE) ** -0.5 * LOG2_E)
    first_half = lax.broadcasted_iota(jnp.int32, (1, LANE), 1) < QK_ROPE
    for s in range(tm // SUB):
        rs = slice(s * SUB, (s + 1) * SUB)
        h = _norm_mod(x_ref[rs, :], gsc, sh)
        qa = _dot(h, wqa_ref[...])
        qa = qa * lax.rsqrt(jnp.mean(qa * qa, axis=-1, keepdims=True) + EPS) * gq
        qa = qa.astype(BF16)
        qn = _dot(qa, wqn_ref[...])
        qr = _dot(qa, wqr_ref[...])
        kv = _dot(h, wkv_ref[...])
        c = kv[:, :KV_RANK]
        ckv_ref[rs, :] = c * lax.rsqrt(jnp.mean(c * c, axis=-1, keepdims=True) + EPS) * gkv_ref[...]
        kpe2 = kv[:, KV_RANK:]
        kpe = jnp.where(first_half, kpe2, 0.0)
        kpe_ref[rs, :] = kpe
        if rope:
            cos = cos_ref[rs, :]
            sin = sin_ref[rs, :]
            qrs = _dot(qa, wqrs_ref[...])
            rot = kpe2 * cos + pltpu.roll(kpe2, QK_ROPE, axis=1) * sin
            kper_ref[rs, :] = jnp.where(first_half, rot, 0.0).astype(BF16)
        else:
            kper_ref[rs, :] = kpe.astype(BF16)
        for pair in range(N_HEADS // 2):
            tile = slice(pair * LANE, (pair + 1) * LANE)
            rot = qr[:, tile] * cos + qrs[:, tile] * sin if rope else qr[:, tile]
            halves = (rot, pltpu.roll(rot, QK_ROPE, axis=1))
            for hd, val in zip((2 * pair, 2 * pair + 1), halves):
                lo = hd * HEAD_PAD
                q_ref[rs, lo:lo + LANE] = qn[:, hd * LANE:(hd + 1) * LANE].astype(BF16)
                q_ref[rs, lo + LANE:lo + HEAD_PAD] = jnp.where(first_half, val, 0.0).astype(BF16)


def _mla_proj(st, x, mod, norm1, wqa, gq, wqn, wqr, wqrs, wkv, gkv, cos_t, sin_t, layer, j, tm=512):
    per_seq = st.seq // tm
    rope_idx = (lambda i: (i % per_seq, 0)) if st.rope else (lambda i: (0, 0))
    row = lambda i: (i, 0)
    return pl.pallas_call(
        functools.partial(_mla_proj_kernel, tm=tm, rope=st.rope),
        grid=(st.rows // tm,),
        in_specs=[
            pl.BlockSpec((tm, D_MODEL), row),
            _resident((1, D_MODEL), layer),
            _mod_spec(st, layer, 0, tm),
            _mod_spec(st, layer, 1, tm),
            _resident(wqa.shape),
            _resident((1, Q_RANK), j),
            _resident(wqn.shape),
            _resident(wqr.shape),
            _resident(wqrs.shape),
            _resident(wkv.shape),
            _resident((1, KV_RANK), j),
            pl.BlockSpec((tm, LANE), rope_idx),
            pl.BlockSpec((tm, LANE), rope_idx),
        ],
        out_specs=[
            pl.BlockSpec((tm, N_HEADS * HEAD_PAD), row),
            pl.BlockSpec((tm, KV_RANK), row),
            pl.BlockSpec((tm, LANE), row),
            pl.BlockSpec((tm, LANE), row),
        ],
        out_shape=[
            jax.ShapeDtypeStruct((st.rows, N_HEADS * HEAD_PAD), BF16),
            jax.ShapeDtypeStruct((st.rows, KV_RANK), F32),
            jax.ShapeDtypeStruct((st.rows, LANE), F32),
            jax.ShapeDtypeStruct((st.rows, LANE), BF16),
        ],
        compiler_params=_params("parallel"),
        name="mla_proj",
    )(x, norm1, mod, mod, wqa, gq, wqn, wqr, wqrs, wkv, gkv, cos_t, sin_t)


def _dot_nt(a, b):
    return lax.dot_general(a, b, (((1,), (1,)), ((), ())), preferred_element_type=F32)


def _expand_keys(c, kpe, wk_ref, wvt_ref, k_ref, vt_ref):
    c = c.astype(BF16)
    kn = _dot(c, wk_ref[...])
    vt_ref[...] = _dot_nt(wvt_ref[...], c).astype(BF16)
    for hd in range(N_HEADS):
        lo = hd * HEAD_PAD
        k_ref[:, lo:lo + LANE] = kn[:, hd * LANE:(hd + 1) * LANE].astype(BF16)
        k_ref[:, lo + LANE:lo + HEAD_PAD] = kpe


def _kv_expand_kernel(*refs, past_blocks, seq_blocks):
    c_ref, kpe_ref, wk_ref, wvt_ref, k_ref, vt_ref = refs[-6:]
    c, kpe = c_ref[...], kpe_ref[...]
    if past_blocks:
        cached = pl.program_id(0) % (past_blocks + seq_blocks) < past_blocks
        c = jnp.where(cached, refs[0][...], c)
        kpe = jnp.where(cached, refs[1][...], kpe)
    _expand_keys(c, kpe, wk_ref, wvt_ref, k_ref, vt_ref)


def _kv_expand(ckv, kpe, wk, wvt, seq=None, past=None, tr=512):
    seq = ckv.shape[0] if seq is None else seq
    n_req = ckv.shape[0] // seq
    seq_blocks = seq // tr
    past_blocks = 0 if past is None else past[0].shape[0] // n_req // tr
    per_req = past_blocks + seq_blocks
    n_rows = n_req * per_req * tr
    row = lambda i: (i, 0)
    new_row = lambda i: ((i // per_req) * seq_blocks + jnp.maximum(i % per_req - past_blocks, 0), 0)
    past_row = lambda i: ((i // per_req) * past_blocks + jnp.minimum(i % per_req, past_blocks - 1), 0)
    past_specs = [] if past is None else [pl.BlockSpec((tr, KV_RANK), past_row), pl.BlockSpec((tr, LANE), past_row)]
    return pl.pallas_call(
        functools.partial(_kv_expand_kernel, past_blocks=past_blocks, seq_blocks=seq_blocks),
        grid=(n_rows // tr,),
        in_specs=past_specs + [
            pl.BlockSpec((tr, KV_RANK), new_row),
            pl.BlockSpec((tr, LANE), new_row),
            _resident(wk.shape),
            _resident(wvt.shape),
        ],
        out_specs=[
            pl.BlockSpec((tr, N_HEADS * HEAD_PAD), row),
            pl.BlockSpec((N_HEADS * V_DIM, tr), lambda i: (0, i)),
        ],
        out_shape=[
            jax.ShapeDtypeStruct((n_rows, N_HEADS * HEAD_PAD), BF16),
            jax.ShapeDtypeStruct((N_HEADS * V_DIM, n_rows), BF16),
        ],
        compiler_params=_params("parallel"),
        name="kv_expand",
    )(*(past or ()), ckv, kpe, wk, wvt)


def _attn_kernel(q_ref, k_ref, vt_ref, o_ref, *, heads, ahead):
    ones = jnp.ones((BF16_ROWS, k_ref.shape[0]), BF16)

    def scores(hd):
        cols = slice(hd * HEAD_PAD, (hd + 1) * HEAD_PAD)
        return _dot_nt(k_ref[:, cols], q_ref[:, cols])

    pending = [scores(hd) for hd in range(min(ahead, heads))]
    for hd in range(heads):
        st = pending.pop(0)
        if hd + ahead < heads:
            pending.append(scores(hd + ahead))
        p = jnp.exp2(st - jnp.max(st, axis=0, keepdims=True)).astype(BF16)
        lhs = jnp.concatenate([vt_ref[hd * V_DIM:(hd + 1) * V_DIM, :], ones], axis=0)
        ot = _dot(lhs, p)
        o_ref[:, hd * V_DIM:(hd + 1) * V_DIM] = (ot[:V_DIM] / ot[V_DIM:V_DIM + 1]).T.astype(BF16)


def _attention(q, k, vt, n_batch, lq, lk, heads, tq, ahead):
    nq = lq // tq
    return pl.pallas_call(
        functools.partial(_attn_kernel, heads=heads, ahead=ahead),
        grid=(n_batch, N_HEADS // heads, nq),
        in_specs=[
            pl.BlockSpec((tq, heads * HEAD_PAD), lambda b, g, i: (b * nq + i, g)),
            pl.BlockSpec((lk, heads * HEAD_PAD), lambda b, g, i: (b, g)),
            pl.BlockSpec((heads * V_DIM, lk), lambda b, g, i: (g, b)),
        ],
        out_specs=pl.BlockSpec((tq, heads * V_DIM), lambda b, g, i: (b * nq + i, g)),
        out_shape=jax.ShapeDtypeStruct((n_batch * lq, N_HEADS * V_DIM), BF16),
        compiler_params=_params("parallel", "parallel", "arbitrary"),
        name="attention",
    )(q, k, vt)


def _self_attn_kernel(q_ref, c_ref, kpe_ref, wk_ref, wvt_ref, o_ref, k_ref, vt_ref, *, ahead):
    _expand_keys(c_ref[...], kpe_ref[...], wk_ref, wvt_ref, k_ref, vt_ref)
    _attn_kernel(q_ref, k_ref, vt_ref, o_ref, heads=N_HEADS, ahead=ahead)


def _self_attention(q, ckv, kpe, wk, wvt, seq, ahead):
    row = lambda b: (b, 0)
    return pl.pallas_call(
        functools.partial(_self_attn_kernel, ahead=ahead),
        grid=(q.shape[0] // seq,),
        in_specs=[
            pl.BlockSpec((seq, N_HEADS * HEAD_PAD), row),
            pl.BlockSpec((seq, KV_RANK), row),
            pl.BlockSpec((seq, LANE), row),
            _resident(wk.shape),
            _resident(wvt.shape),
        ],
        out_specs=pl.BlockSpec((seq, N_HEADS * V_DIM), row),
        out_shape=jax.ShapeDtypeStruct((q.shape[0], N_HEADS * V_DIM), BF16),
        scratch_shapes=[pltpu.VMEM((seq, N_HEADS * HEAD_PAD), BF16), pltpu.VMEM((N_HEADS * V_DIM, seq), BF16)],
        compiler_params=_params("parallel"),
        name="self_attention",
    )(q, ckv, kpe, wk, wvt)


def _mlp_kernel(h_ref, x_ref, gate_ref, w1_ref, w2_ref, gf_ref, *rest, tm, final, n_cast):
    src_refs, o_ref, dst_refs = rest[:n_cast], rest[n_cast], rest[n_cast + 1:]
    f = pl.program_id(1)

    @pl.when(f == 0)
    def _():
        o_ref[...] = x_ref[...]

    hid = _dot(h_ref[...], w1_ref[...])
    hid = jnp.square(jnp.maximum(hid, 0.0)).astype(BF16)
    o_ref[...] += gate_ref[0] * _dot(hid, w2_ref[...])

    for src_ref, dst_ref in zip(src_refs, dst_refs):
        dst_ref[...] = src_ref[...].astype(BF16)

    if final:
        @pl.when(f == pl.num_programs(1) - 1)
        def _():
            gf = gf_ref[...]
            for s in range(tm // SUB):
                rs = slice(s * SUB, (s + 1) * SUB)
                y = o_ref[rs, :]
                o_ref[rs, :] = y * lax.rsqrt(jnp.mean(y * y, axis=-1, keepdims=True) + EPS) * gf


def _mlp(st, h, x, mod, w1, w2, final_g, layer, final, casts=(), tm=512, tf=1024):
    row = lambda i, f: (i, 0)
    nf = D_FF // tf
    steps = (st.rows // tm) * nf
    slab = lambda i, f: (i * nf + f, 0)
    cast_in, cast_out, cast_shape = [], [], []
    for w, l in casts:
        _, r, c = w.shape
        cast_in.append(pl.BlockSpec((None, r // steps, c), lambda i, f, l=l: (l, i * nf + f, 0)))
        cast_out.append(pl.BlockSpec((r // steps, c), slab))
        cast_shape.append(jax.ShapeDtypeStruct((r, c), BF16))
    out = pl.pallas_call(
        functools.partial(_mlp_kernel, tm=tm, final=final, n_cast=len(casts)),
        grid=(st.rows // tm, nf),
        in_specs=[
            pl.BlockSpec((tm, D_MODEL), row),
            pl.BlockSpec((tm, D_MODEL), row),
            _mod_spec(st, layer, 5, tm),
            pl.BlockSpec((D_MODEL, tf), lambda i, f: (0, f)),
            pl.BlockSpec((tf, D_MODEL), lambda i, f: (f, 0)),
            _resident((1, D_MODEL)),
        ] + cast_in,
        out_specs=[pl.BlockSpec((tm, D_MODEL), row)] + cast_out,
        out_shape=[jax.ShapeDtypeStruct((st.rows, D_MODEL), F32)] + cast_shape,
        compiler_params=_params("parallel", "arbitrary"),
        name="mlp",
    )(h, x, mod, w1, w2, final_g, *[w for w, _ in casts])
    return out[0], out[1:]


def _rope_tables():
    t = np.arange(DEC_SEQ)
    nf = QK_ROPE // 4
    inv = (1.0 / (np.float32(ROPE_THETA) ** (np.arange(nf, dtype=np.float32) / np.float32(nf)))).astype(np.float32)
    ang_r = (t // GRID_W).astype(np.float32)[:, None] * inv
    ang_c = (t % GRID_W).astype(np.float32)[:, None] * inv
    cr, sr, cc, sc = np.cos(ang_r), np.sin(ang_r), np.cos(ang_c), np.sin(ang_c)
    cos = np.tile(np.concatenate([cr, cr, cc, cc], axis=1), (1, LANE // QK_ROPE))
    sin = np.tile(np.concatenate([-sr, sr, -sc, sc], axis=1), (1, LANE // QK_ROPE))
    return jnp.asarray(cos, F32), jnp.asarray(sin, F32)


def _swap_rope_partners(w):
    q = QK_ROPE // 4
    return jnp.concatenate([w[..., q:2 * q], w[..., :q], w[..., 3 * q:], w[..., 2 * q:3 * q]], axis=-1)


def _mla_weights(w_q_b, w_kv_a, w_kv_b):
    wq = w_q_b.reshape(Q_RANK, N_HEADS, QK_NOPE + QK_ROPE)
    wq_nope = wq[..., :QK_NOPE].reshape(Q_RANK, N_HEADS * QK_NOPE)
    wq_rope = wq[..., QK_NOPE:].reshape(Q_RANK, N_HEADS * QK_ROPE)
    wq_rope_swap = _swap_rope_partners(wq[..., QK_NOPE:]).reshape(Q_RANK, N_HEADS * QK_ROPE)
    kpe_w = w_kv_a[:, KV_RANK:]
    wkv = jnp.concatenate([w_kv_a[:, :KV_RANK], kpe_w, _swap_rope_partners(kpe_w)], axis=1)
    wkvb = w_kv_b.reshape(KV_RANK, N_HEADS, QK_NOPE + V_DIM)
    wk = wkvb[..., :QK_NOPE].reshape(KV_RANK, N_HEADS * QK_NOPE)
    wvt = wkvb[..., QK_NOPE:].reshape(KV_RANK, N_HEADS * V_DIM).T
    casts = (wq_nope, wq_rope, wq_rope_swap, wkv, wk, wvt)
    return tuple(w.astype(BF16) for w in casts)


def kernel(x_prompt, x_sample, cache_ckv, cache_kpe, c, c_ctx, ada_w, ada_b, norm1, norm2, conv_w_in, conv_w, conv_w_out, gmlp_w_in, gmlp_g_v, gmlp_w_s, gmlp_b_s, gmlp_w_out, mla_w_q_a, mla_g_q, mla_w_q_b, mla_w_kv_a, mla_g_kv, mla_w_kv_b, mla_w_o, mlp_w1, mlp_w2, final_norm):
    D = D_MODEL
    streams = (PROMPT, SAMPLE)
    xs = [x_prompt.reshape(PROMPT.rows, D), x_sample.reshape(SAMPLE.rows, D)]
    cond = jnp.concatenate([c_ctx[None, :], c, jnp.zeros((N_GROUPS - 1 - DEC_BATCH, D), F32)], axis=0)
    mod = _modvec(cond, ada_w, ada_b)

    n1, n2 = norm1[:, None, :], norm2[:, None, :]
    final_g = final_norm[None, :]

    def big_weights(i):
        mix_in, mix_out = ((conv_w_in, conv_w_out), (gmlp_w_in, gmlp_w_out), (mla_w_q_a, mla_w_o))[i % N_MIXERS]
        return {"w1": (mlp_w1, i), "in": (mix_in, i // N_MIXERS), "w2": (mlp_w2, i), "out": (mix_out, i // N_MIXERS)}

    cast_split = (("w1", "in"), ("w2", "out"))
    bf = {name: w[l].astype(BF16) for name, (w, l) in big_weights(0).items() if name in ("in", "out")}

    cos_t, sin_t = _rope_tables()
    new_ckv, new_kpe = [], []
    for i in range(DEPTH):
        kind, j = i % N_MIXERS, i // N_MIXERS
        hs = [None, None]
        if kind == 0:
            host = "w1" not in bf
            for n, st in enumerate(streams):
                cast = HalfCast(mlp_w1, i, n, bf.get("w1")) if host else None
                bg, p, *half = _conv_in(st, xs[n], mod, n1, bf["in"], i, cast=cast)
                bf.update(zip(("w1",), half))
                cast = HalfCast(mlp_w2, i, n, bf.get("w2")) if host else None
                xs[n], hs[n], *half = _conv_out(st, bg, p, conv_w, xs[n], mod, bf["out"], n2, i, j, cast=cast)
                bf.update(zip(("w2",), half))
        elif kind == 1:
            b_full = jnp.repeat(gmlp_b_s[j].T, LANE, axis=1)
            w_s = gmlp_w_s[j].astype(BF16)
            for n, st in enumerate(streams):
                z = _gmlp_in(st, xs[n], mod, n1, bf["in"], i)
                xs[n], hs[n] = _gmlp_out(st, z, gmlp_g_v[:, None, :], w_s, b_full, xs[n], mod, bf["out"], n2, i, j)
        else:
            wqn, wqr, wqrs, wkv, wk, wvt = _mla_weights(mla_w_q_b[j], mla_w_kv_a[j], mla_w_kv_b[j])
            proj = [_mla_proj(st, xs[n], mod, n1, bf["in"], mla_g_q[:, None, :], wqn, wqr, wqrs, wkv, mla_g_kv[:, None, :],
                              cos_t, sin_t, i, j) for n, st in enumerate(streams)]
            q, ckv, kpe, kpe_rot = proj[0]
            new_ckv.append(ckv.reshape(BATCH, SEQ, KV_RANK))
            new_kpe.append(kpe[:, :QK_ROPE].reshape(BATCH, SEQ, QK_ROPE))
            o_p = _self_attention(q, ckv, kpe_rot, wk, wvt, SEQ, ahead=CTX_AHEAD)
            q, ckv, kpe, kpe_rot = proj[1]
            lk = PAST_LEN + DEC_SEQ
            cache_pe = jnp.pad(cache_kpe[:, j], ((0, 0), (0, 0), (0, LANE - QK_ROPE))).astype(BF16)
            past = (cache_ckv[:, j].reshape(DEC_BATCH * PAST_LEN, KV_RANK), cache_pe.reshape(DEC_BATCH * PAST_LEN, LANE))
            k_s, v_s = _kv_expand(ckv, kpe_rot, wk, wvt, seq=DEC_SEQ, past=past)
            o_s = _attention(q, k_s, v_s, DEC_BATCH, DEC_SEQ, lk, **ATTN_SAMPLE)
            for n, (st, o) in enumerate(zip(streams, (o_p, o_s))):
                xs[n], hs[n] = _attn_out(st, o, xs[n], mod, bf["out"], n2, i)
        nxt = big_weights(i + 1) if i + 1 < DEPTH else None
        bf_next = {}
        for n, st in enumerate(streams):
            names = cast_split[n] if nxt else ()
            xs[n], cast = _mlp(st, hs[n], xs[n], mod, bf["w1"], bf["w2"], final_g, i, final=(i == DEPTH - 1),
                               casts=[nxt[name] for name in names])
            bf_next.update(zip(names, cast))
        bf = bf_next
    y_prompt = xs[0].reshape(BATCH, SEQ, D)
    y_sample = xs[1].reshape(DEC_BATCH, DEC_SEQ, D)
    return (y_prompt, y_sample, jnp.stack(new_ckv, axis=1), jnp.stack(new_kpe, axis=1))
```

```python
import functools
from typing import NamedTuple

import jax
import jax.numpy as jnp
import numpy as np
from jax import lax
from jax.experimental import pallas as pl
from jax.experimental.pallas import tpu as pltpu

D_MODEL = 2048
BATCH = 32
SEQ = 256
DEPTH = 4
DEC_BATCH = 4
DEC_SEQ = 2048
PAST_LEN = 512
GRID_W = 64
N_MIXERS = 3
CONV_WIDTH = 3
CHUNK = 128
GMLP_GROUPS = 16
N_HEADS = 16
QK_NOPE = 128
QK_ROPE = 64
V_DIM = 128
Q_RANK = 512
KV_RANK = 512
ROPE_THETA = 10000.0
D_FF = 4 * D_MODEL
N_MOD = 6
EPS = 1e-6
LOG2_E = 1.4426950408889634

N_GROUPS = 8
HEAD_PAD = 256
LANE = 128
BF16_ROWS = 16
SUB = 256
VMEM_LIMIT = 56 * 1024 * 1024
ATTN_SAMPLE = dict(heads=4, tq=1024, ahead=1)
CTX_AHEAD = 16

BF16 = jnp.bfloat16
F32 = jnp.float32


class Stream(NamedTuple):
    rows: int
    seq: int
    group0: int
    group_rows: int
    rope: bool


PROMPT = Stream(BATCH * SEQ, SEQ, 0, BATCH * SEQ, False)
SAMPLE = Stream(DEC_BATCH * DEC_SEQ, DEC_SEQ, 1, DEC_SEQ, True)


def _params(*sem):
    return pltpu.CompilerParams(dimension_semantics=sem, vmem_limit_bytes=VMEM_LIMIT)


def _dot(a, b):
    return jnp.dot(a, b, preferred_element_type=F32)


def _resident(shape, layer=None):
    if layer is None:
        zeros = (0,) * len(shape)
        return pl.BlockSpec(shape, lambda *_: zeros, pipeline_mode=pl.Buffered(1))
    index = (layer,) + (0,) * len(shape)
    return pl.BlockSpec((None,) + tuple(shape), lambda *_: index, pipeline_mode=pl.Buffered(1))


def _mod_spec(st, layer, chunk, tm):
    def index(i, *_):
        return ((layer * N_MOD + chunk) * N_GROUPS + st.group0 + (i * tm) // st.group_rows, 0, 0)

    return pl.BlockSpec((1, 1, D_MODEL), index)


def _gain_shift(g_ref, sh_ref, sc_ref):
    return g_ref[...] * (1.0 + sc_ref[0]), sh_ref[0]


def _norm_mod(x, gsc, sh):
    ms = jnp.mean(x * x, axis=-1, keepdims=True)
    return (x * lax.rsqrt(ms + EPS) * gsc + sh).astype(BF16)


def _modvec_kernel(c_ref, w_ref, b_ref, o_ref):
    s = jax.nn.silu(c_ref[...]).astype(BF16)
    o_ref[0, 0] = _dot(s, w_ref[0].astype(BF16)) + b_ref[0]


def _modvec(cond, ada_w, ada_b, tn=1024):
    per_chunk = D_MODEL // tn
    out = pl.pallas_call(
        _modvec_kernel,
        grid=(DEPTH, N_MOD * per_chunk),
        in_specs=[
            pl.BlockSpec((N_GROUPS, D_MODEL), lambda l, j: (0, 0)),
            pl.BlockSpec((1, D_MODEL, tn), lambda l, j: (l, 0, j)),
            pl.BlockSpec((1, 1, tn), lambda l, j: (l, 0, j)),
        ],
        out_specs=pl.BlockSpec((1, 1, N_GROUPS, tn), lambda l, j: (l, j // per_chunk, 0, j % per_chunk)),
        out_shape=jax.ShapeDtypeStruct((DEPTH, N_MOD, N_GROUPS, D_MODEL), F32),
        compiler_params=_params("arbitrary", "arbitrary"),
        name="modvec",
    )(cond, ada_w, ada_b.reshape(DEPTH, 1, N_MOD * D_MODEL))
    return out.reshape(DEPTH * N_MOD * N_GROUPS, 1, D_MODEL)


class HalfCast(NamedTuple):
    w: jax.Array
    layer: int
    prev: jax.Array | None


def _half_cast_specs(cast, steps):
    _, r, c = cast.w.shape
    rows = r // (2 * steps)
    slab = lambda i: (i, 0)
    if cast.prev is None:
        in_specs = [pl.BlockSpec((None, rows, c), lambda i: (cast.layer, i, 0))]
        return in_specs, [cast.w], pl.BlockSpec((rows, c), slab), jax.ShapeDtypeStruct((r // 2, c), BF16)
    in_specs = [pl.BlockSpec((None, rows, c), lambda i: (cast.layer, steps + i, 0)), pl.BlockSpec((rows, c), slab)]
    out_spec = pl.BlockSpec((2, rows, c), lambda i: (0, i, 0))
    return in_specs, [cast.w, cast.prev], out_spec, jax.ShapeDtypeStruct((2, r // 2, c), BF16)


def _half_cast_step(refs, dst_ref):
    if len(refs) == 1:
        dst_ref[...] = refs[0][...].astype(BF16)
    else:
        dst_ref[0] = refs[1][...]
        dst_ref[1] = refs[0][...].astype(BF16)


def _conv_in_kernel(x_ref, g_ref, sh_ref, sc_ref, w_ref, *rest, tm, cast):
    bg_ref, p_ref = rest[-3:-1] if cast else rest
    gsc, sh = _gain_shift(g_ref, sh_ref, sc_ref)
    for s in range(tm // SUB):
        rs = slice(s * SUB, (s + 1) * SUB)
        z = _dot(_norm_mod(x_ref[rs, :], gsc, sh), w_ref[...])
        bg_ref[rs, :] = z[:, :D_MODEL].astype(BF16)
        p_ref[rs, :] = (z[:, D_MODEL:2 * D_MODEL] * z[:, 2 * D_MODEL:]).astype(BF16)
    if cast:
        _half_cast_step(rest[:-3], rest[-1])


def _conv_in(st, x, mod, norm1, w_in, layer, cast=None, tm=512):
    row = lambda i: (i, 0)
    in_specs = [
        pl.BlockSpec((tm, D_MODEL), row),
        _resident((1, D_MODEL), layer),
        _mod_spec(st, layer, 0, tm),
        _mod_spec(st, layer, 1, tm),
        _resident((D_MODEL, 3 * D_MODEL)),
    ]
    args = [x, norm1, mod, mod, w_in]
    out_specs = [pl.BlockSpec((tm, D_MODEL), row)] * 2
    out_shape = [jax.ShapeDtypeStruct((st.rows, D_MODEL), BF16)] * 2
    if cast is not None:
        c_in, c_args, c_out, c_shape = _half_cast_specs(cast, st.rows // tm)
        in_specs, args, out_specs, out_shape = in_specs + c_in, args + c_args, out_specs + [c_out], out_shape + [c_shape]
    return pl.pallas_call(
        functools.partial(_conv_in_kernel, tm=tm, cast=cast is not None),
        grid=(st.rows // tm,),
        in_specs=in_specs,
        out_specs=out_specs,
        out_shape=out_shape,
        compiler_params=_params("parallel"),
        name="conv_in",
    )(*args)


def _gmlp_in_kernel(x_ref, g_ref, sh_ref, sc_ref, w_ref, z_ref, *, tm):
    gsc, sh = _gain_shift(g_ref, sh_ref, sc_ref)
    for s in range(tm // SUB):
        rs = slice(s * SUB, (s + 1) * SUB)
        z_ref[rs, :] = _dot(_norm_mod(x_ref[rs, :], gsc, sh), w_ref[...]).astype(BF16)


def _gmlp_in(st, x, mod, norm1, w_in, layer, tm=512):
    n_out = w_in.shape[-1]
    row = lambda i: (i, 0)
    return pl.pallas_call(
        functools.partial(_gmlp_in_kernel, tm=tm),
        grid=(st.rows // tm,),
        in_specs=[
            pl.BlockSpec((tm, D_MODEL), row),
            _resident((1, D_MODEL), layer),
            _mod_spec(st, layer, 0, tm),
            _mod_spec(st, layer, 1, tm),
            _resident((D_MODEL, n_out)),
        ],
        out_specs=pl.BlockSpec((tm, n_out), row),
        out_shape=jax.ShapeDtypeStruct((st.rows, n_out), BF16),
        compiler_params=_params("parallel"),
        name="gmlp_in",
    )(x, norm1, mod, mod, w_in)


def _residual_proj(a, rs, x_ref, gate_ref, w_ref, gsc2, sh2, o_ref, h_ref):
    x1 = x_ref[rs, :] + gate_ref[0] * _dot(a, w_ref[...])
    o_ref[rs, :] = x1
    h_ref[rs, :] = _norm_mod(x1, gsc2, sh2)


def _proj_specs(st, layer, tm):
    row = lambda i: (i, 0)
    in_specs = [
        pl.BlockSpec((tm, D_MODEL), row),
        _mod_spec(st, layer, 2, tm),
        _resident((D_MODEL, D_MODEL)),
        _resident((1, D_MODEL), layer),
        _mod_spec(st, layer, 3, tm),
        _mod_spec(st, layer, 4, tm),
    ]
    out_specs = [pl.BlockSpec((tm, D_MODEL), row)] * 2
    out_shape = [jax.ShapeDtypeStruct((st.rows, D_MODEL), F32), jax.ShapeDtypeStruct((st.rows, D_MODEL), BF16)]
    return in_specs, out_specs, out_shape


def _conv_out_kernel(bg_ref, p_ref, lo_ref, hi_ref, cw_ref, x_ref, gate_ref, w_ref, g2_ref, sh2_ref, sc2_ref,
                     *rest, tm, seq, cast):
    if cast:
        o_ref, h_ref, dst_ref, pbuf_ref = rest[-4:]
        _half_cast_step(rest[:-4], dst_ref)
    else:
        o_ref, h_ref, pbuf_ref = rest
    halo = BF16_ROWS
    n_sub = tm // SUB
    gsc2, sh2 = _gain_shift(g2_ref, sh2_ref, sc2_ref)
    base = pl.program_id(0) * tm
    width = 4 * LANE
    for s in range(n_sub):
        r0 = s * SUB
        before = lo_ref[...] if s == 0 else p_ref[r0 - halo:r0, :]
        after = hi_ref[...] if s == n_sub - 1 else p_ref[r0 + SUB:r0 + SUB + halo, :]
        pbuf_ref[s, 0:halo, :] = jnp.where((base + r0) % seq == 0, 0.0, before.astype(F32))
        pbuf_ref[s, halo:halo + SUB, :] = p_ref[r0:r0 + SUB, :].astype(F32)
        pbuf_ref[s, halo + SUB:, :] = jnp.where((base + r0 + SUB) % seq == 0, 0.0, after.astype(F32))
        parts = []
        for lo in range(0, D_MODEL, width):
            cols = slice(lo, lo + width)
            prev = pbuf_ref[s, halo - 1:halo - 1 + SUB, cols]
            cur = pbuf_ref[s, halo:halo + SUB, cols]
            nxt = pbuf_ref[s, halo + 1:halo + 1 + SUB, cols]
            conv = prev * cw_ref[0:1, cols] + cur * cw_ref[1:2, cols] + nxt * cw_ref[2:3, cols]
            parts.append((bg_ref[r0:r0 + SUB, cols].astype(F32) * conv).astype(BF16))
        _residual_proj(jnp.concatenate(parts, axis=1), slice(r0, r0 + SUB), x_ref, gate_ref, w_ref, gsc2, sh2,
                       o_ref, h_ref)


def _conv_out(st, bg, p, conv_w, x, mod, w_out, norm2, layer, j, cast=None, tm=512):
    assert st.seq % SUB == 0 and tm % SUB == 0
    nb = tm // BF16_ROWS
    last_halo = st.rows // BF16_ROWS - 1
    row = lambda i: (i, 0)
    tail, out_specs, out_shape = _proj_specs(st, layer, tm)
    in_specs = [
        pl.BlockSpec((tm, D_MODEL), row),
        pl.BlockSpec((tm, D_MODEL), row),
        pl.BlockSpec((BF16_ROWS, D_MODEL), lambda i: (jnp.maximum(i * nb - 1, 0), 0)),
        pl.BlockSpec((BF16_ROWS, D_MODEL), lambda i: (jnp.minimum((i + 1) * nb, last_halo), 0)),
        _resident((CONV_WIDTH, D_MODEL), j),
    ] + tail
    args = [bg, p, p, p, conv_w, x, mod, w_out, norm2, mod, mod]
    if cast is not None:
        c_in, c_args, c_out, c_shape = _half_cast_specs(cast, st.rows // tm)
        in_specs, args, out_specs, out_shape = in_specs + c_in, args + c_args, out_specs + [c_out], out_shape + [c_shape]
    return pl.pallas_call(
        functools.partial(_conv_out_kernel, tm=tm, seq=st.seq, cast=cast is not None),
        grid=(st.rows // tm,),
        in_specs=in_specs,
        out_specs=out_specs,
        out_shape=out_shape,
        scratch_shapes=[pltpu.VMEM((tm // SUB, SUB + 2 * BF16_ROWS, D_MODEL), F32)],
        compiler_params=_params("parallel"),
        name="conv_out",
    )(*args)


def _gmlp_out_kernel(zu_ref, zv_ref, gv_ref, ws_ref, bs_ref, x_ref, gate_ref, w_ref, g2_ref, sh2_ref, sc2_ref,
                     o_ref, h_ref, a_ref, vn_ref, *, tm):
    gv = gv_ref[...]
    gsc2, sh2 = _gain_shift(g2_ref, sh2_ref, sc2_ref)
    for s in range(tm // SUB):
        chunks = range(s * SUB // CHUNK, (s + 1) * SUB // CHUNK)
        for c in chunks:
            rs = slice(c * CHUNK, (c + 1) * CHUNK)
            v = zv_ref[rs, :].astype(F32)
            ms = jnp.mean(v * v, axis=-1, keepdims=True)
            vn_ref[rs, :] = (v * lax.rsqrt(ms + EPS) * gv).astype(BF16)
        for g in range(GMLP_GROUPS):
            cols = slice(g * LANE, (g + 1) * LANE)
            rhs = jnp.concatenate([vn_ref[c * CHUNK:(c + 1) * CHUNK, cols] for c in chunks], axis=1)
            mixed = _dot(ws_ref[g], rhs)
            for n, c in enumerate(chunks):
                rs = slice(c * CHUNK, (c + 1) * CHUNK)
                vv = mixed[:, n * CHUNK:(n + 1) * CHUNK] + bs_ref[:, cols]
                a_ref[rs, cols] = (zu_ref[rs, cols].astype(F32) * vv).astype(BF16)
        rs = slice(s * SUB, (s + 1) * SUB)
        _residual_proj(a_ref[rs, :], rs, x_ref, gate_ref, w_ref, gsc2, sh2, o_ref, h_ref)


def _gmlp_out(st, z, g_v, w_s, b_full, x, mod, w_out, norm2, layer, j, tm=512):
    tail, out_specs, out_shape = _proj_specs(st, layer, tm)
    return pl.pallas_call(
        functools.partial(_gmlp_out_kernel, tm=tm),
        grid=(st.rows // tm,),
        in_specs=[
            pl.BlockSpec((tm, D_MODEL), lambda i: (i, 0)),
            pl.BlockSpec((tm, D_MODEL), lambda i: (i, 1)),
            _resident((1, D_MODEL), j),
            _resident((GMLP_GROUPS, CHUNK, CHUNK)),
            _resident((CHUNK, D_MODEL)),
        ] + tail,
        out_specs=out_specs,
        out_shape=out_shape,
        scratch_shapes=[pltpu.VMEM((tm, D_MODEL), BF16), pltpu.VMEM((tm, D_MODEL), BF16)],
        compiler_params=_params("parallel"),
        name="gmlp_out",
    )(z, z, g_v, w_s, b_full, x, mod, w_out, norm2, mod, mod)


def _attn_out_kernel(a_ref, x_ref, gate_ref, w_ref, g2_ref, sh2_ref, sc2_ref, o_ref, h_ref, *, tm):
    gsc2, sh2 = _gain_shift(g2_ref, sh2_ref, sc2_ref)
    for s in range(tm // SUB):
        rs = slice(s * SUB, (s + 1) * SUB)
        _residual_proj(a_ref[rs, :], rs, x_ref, gate_ref, w_ref, gsc2, sh2, o_ref, h_ref)


def _attn_out(st, a, x, mod, w_o, norm2, layer, tm=512):
    tail, out_specs, out_shape = _proj_specs(st, layer, tm)
    return pl.pallas_call(
        functools.partial(_attn_out_kernel, tm=tm),
        grid=(st.rows // tm,),
        in_specs=[pl.BlockSpec((tm, D_MODEL), lambda i: (i, 0))] + tail,
        out_specs=out_specs,
        out_shape=out_shape,
        compiler_params=_params("parallel"),
        name="attn_out",
    )(a, x, mod, w_o, norm2, mod, mod)


def _mla_proj_kernel(x_ref, g_ref, sh_ref, sc_ref, wqa_ref, gq_ref, wqn_ref, wqr_ref, wqrs_ref, wkv_ref, gkv_ref,
                     cos_ref, sin_ref, q_ref, ckv_ref, kpe_ref, kper_ref, *, tm, rope):
    gsc, sh = _gain_shift(g_ref, sh_ref, sc_ref)
    gq = gq_ref[...] * ((QK_NOPE + QK_ROPE) ** -0.5 * LOG2_E)
    first_half = lax.broadcasted_iota(jnp.int32, (1, LANE), 1) < QK_ROPE
    for s in range(tm // SUB):
        rs = slice(s * SUB, (s + 1) * SUB)
        h = _norm_mod(x_ref[rs, :], gsc, sh)
        qa = _dot(h, wqa_ref[...])
        qa = qa * lax.rsqrt(jnp.mean(qa * qa, axis=-1, keepdims=True) + EPS) * gq
        qa = qa.astype(BF16)
        qn = _dot(qa, wqn_ref[...])
        qr = _dot(qa, wqr_ref[...])
        kv = _dot(h, wkv_ref[...])
        c = kv[:, :KV_RANK]
        ckv_ref[rs, :] = c * lax.rsqrt(jnp.mean(c * c, axis=-1, keepdims=True) + EPS) * gkv_ref[...]
        kpe2 = kv[:, KV_RANK:]
        kpe = jnp.where(first_half, kpe2, 0.0)
        kpe_ref[rs, :] = kpe
        if rope:
            cos = cos_ref[rs, :]
            sin = sin_ref[rs, :]
            qrs = _dot(qa, wqrs_ref[...])
            rot = kpe2 * cos + pltpu.roll(kpe2, QK_ROPE, axis=1) * sin
            kper_ref[rs, :] = jnp.where(first_half, rot, 0.0).astype(BF16)
        else:
            kper_ref[rs, :] = kpe.astype(BF16)
        for pair in range(N_HEADS // 2):
            tile = slice(pair * LANE, (pair + 1) * LANE)
            rot = qr[:, tile] * cos + qrs[:, tile] * sin if rope else qr[:, tile]
            halves = (rot, pltpu.roll(rot, QK_ROPE, axis=1))
            for hd, val in zip((2 * pair, 2 * pair + 1), halves):
                lo = hd * HEAD_PAD
                q_ref[rs, lo:lo + LANE] = qn[:, hd * LANE:(hd + 1) * LANE].astype(BF16)
                q_ref[rs, lo + LANE:lo + HEAD_PAD] = jnp.where(first_half, val, 0.0).astype(BF16)


def _mla_proj(st, x, mod, norm1, wqa, gq, wqn, wqr, wqrs, wkv, gkv, cos_t, sin_t, layer, j, tm=512):
    per_seq = st.seq // tm
    rope_idx = (lambda i: (i % per_seq, 0)) if st.rope else (lambda i: (0, 0))
    row = lambda i: (i, 0)
    return pl.pallas_call(
        functools.partial(_mla_proj_kernel, tm=tm, rope=st.rope),
        grid=(st.rows // tm,),
        in_specs=[
            pl.BlockSpec((tm, D_MODEL), row),
            _resident((1, D_MODEL), layer),
            _mod_spec(st, layer, 0, tm),
            _mod_spec(st, layer, 1, tm),
            _resident(wqa.shape),
            _resident((1, Q_RANK), j),
            _resident(wqn.shape),
            _resident(wqr.shape),
            _resident(wqrs.shape),
            _resident(wkv.shape),
            _resident((1, KV_RANK), j),
            pl.BlockSpec((tm, LANE), rope_idx),
            pl.BlockSpec((tm, LANE), rope_idx),
        ],
        out_specs=[
            pl.BlockSpec((tm, N_HEADS * HEAD_PAD), row),
            pl.BlockSpec((tm, KV_RANK), row),
            pl.BlockSpec((tm, LANE), row),
            pl.BlockSpec((tm, LANE), row),
        ],
        out_shape=[
            jax.ShapeDtypeStruct((st.rows, N_HEADS * HEAD_PAD), BF16),
            jax.ShapeDtypeStruct((st.rows, KV_RANK), F32),
            jax.ShapeDtypeStruct((st.rows, LANE), F32),
            jax.ShapeDtypeStruct((st.rows, LANE), BF16),
        ],
        compiler_params=_params("parallel"),
        name="mla_proj",
    )(x, norm1, mod, mod, wqa, gq, wqn, wqr, wqrs, wkv, gkv, cos_t, sin_t)


def _dot_nt(a, b):
    return lax.dot_general(a, b, (((1,), (1,)), ((), ())), preferred_element_type=F32)


def _expand_keys(c, kpe, wk_ref, wvt_ref, k_ref, vt_ref):
    c = c.astype(BF16)
    kn = _dot(c, wk_ref[...])
    vt_ref[...] = _dot_nt(wvt_ref[...], c).astype(BF16)
    for hd in range(N_HEADS):
        lo = hd * HEAD_PAD
        k_ref[:, lo:lo + LANE] = kn[:, hd * LANE:(hd + 1) * LANE].astype(BF16)
        k_ref[:, lo + LANE:lo + HEAD_PAD] = kpe


def _kv_expand_kernel(*refs, past_blocks, seq_blocks):
    c_ref, kpe_ref, wk_ref, wvt_ref, k_ref, vt_ref = refs[-6:]
    c, kpe = c_ref[...], kpe_ref[...]
    if past_blocks:
        cached = pl.program_id(0) % (past_blocks + seq_blocks) < past_blocks
        c = jnp.where(cached, refs[0][...], c)
        kpe = jnp.where(cached, refs[1][...], kpe)
    _expand_keys(c, kpe, wk_ref, wvt_ref, k_ref, vt_ref)


def _kv_expand(ckv, kpe, wk, wvt, seq=None, past=None, tr=512):
    seq = ckv.shape[0] if seq is None else seq
    n_req = ckv.shape[0] // seq
    seq_blocks = seq // tr
    past_blocks = 0 if past is None else past[0].shape[0] // n_req // tr
    per_req = past_blocks + seq_blocks
    n_rows = n_req * per_req * tr
    row = lambda i: (i, 0)
    new_row = lambda i: ((i // per_req) * seq_blocks + jnp.maximum(i % per_req - past_blocks, 0), 0)
    past_row = lambda i: ((i // per_req) * past_blocks + jnp.minimum(i % per_req, past_blocks - 1), 0)
    past_specs = [] if past is None else [pl.BlockSpec((tr, KV_RANK), past_row), pl.BlockSpec((tr, LANE), past_row)]
    return pl.pallas_call(
        functools.partial(_kv_expand_kernel, past_blocks=past_blocks, seq_blocks=seq_blocks),
        grid=(n_rows // tr,),
        in_specs=past_specs + [
            pl.BlockSpec((tr, KV_RANK), new_row),
            pl.BlockSpec((tr, LANE), new_row),
            _resident(wk.shape),
            _resident(wvt.shape),
        ],
        out_specs=[
            pl.BlockSpec((tr, N_HEADS * HEAD_PAD), row),
            pl.BlockSpec((N_HEADS * V_DIM, tr), lambda i: (0, i)),
        ],
        out_shape=[
            jax.ShapeDtypeStruct((n_rows, N_HEADS * HEAD_PAD), BF16),
            jax.ShapeDtypeStruct((N_HEADS * V_DIM, n_rows), BF16),
        ],
        compiler_params=_params("parallel"),
        name="kv_expand",
    )(*(past or ()), ckv, kpe, wk, wvt)


def _attn_kernel(q_ref, k_ref, vt_ref, o_ref, *, heads, ahead):
    ones = jnp.ones((BF16_ROWS, k_ref.shape[0]), BF16)

    def scores(hd):
        cols = slice(hd * HEAD_PAD, (hd + 1) * HEAD_PAD)
        return _dot_nt(k_ref[:, cols], q_ref[:, cols])

    pending = [scores(hd) for hd in range(min(ahead, heads))]
    for hd in range(heads):
        st = pending.pop(0)
        if hd + ahead < heads:
            pending.append(scores(hd + ahead))
        p = jnp.exp2(st - jnp.max(st, axis=0, keepdims=True)).astype(BF16)
        lhs = jnp.concatenate([vt_ref[hd * V_DIM:(hd + 1) * V_DIM, :], ones], axis=0)
        ot = _dot(lhs, p)
        o_ref[:, hd * V_DIM:(hd + 1) * V_DIM] = (ot[:V_DIM] / ot[V_DIM:V_DIM + 1]).T.astype(BF16)


def _attention(q, k, vt, n_batch, lq, lk, heads, tq, ahead):
    nq = lq // tq
    return pl.pallas_call(
        functools.partial(_attn_kernel, heads=heads, ahead=ahead),
        grid=(n_batch, N_HEADS // heads, nq),
        in_specs=[
            pl.BlockSpec((tq, heads * HEAD_PAD), lambda b, g, i: (b * nq + i, g)),
            pl.BlockSpec((lk, heads * HEAD_PAD), lambda b, g, i: (b, g)),
            pl.BlockSpec((heads * V_DIM, lk), lambda b, g, i: (g, b)),
        ],
        out_specs=pl.BlockSpec((tq, heads * V_DIM), lambda b, g, i: (b * nq + i, g)),
        out_shape=jax.ShapeDtypeStruct((n_batch * lq, N_HEADS * V_DIM), BF16),
        compiler_params=_params("parallel", "parallel", "arbitrary"),
        name="attention",
    )(q, k, vt)


def _self_attn_kernel(q_ref, c_ref, kpe_ref, wk_ref, wvt_ref, o_ref, k_ref, vt_ref, *, ahead):
    _expand_keys(c_ref[...], kpe_ref[...], wk_ref, wvt_ref, k_ref, vt_ref)
    _attn_kernel(q_ref, k_ref, vt_ref, o_ref, heads=N_HEADS, ahead=ahead)


def _self_attention(q, ckv, kpe, wk, wvt, seq, ahead):
    row = lambda b: (b, 0)
    return pl.pallas_call(
        functools.partial(_self_attn_kernel, ahead=ahead),
        grid=(q.shape[0] // seq,),
        in_specs=[
            pl.BlockSpec((seq, N_HEADS * HEAD_PAD), row),
            pl.BlockSpec((seq, KV_RANK), row),
            pl.BlockSpec((seq, LANE), row),
            _resident(wk.shape),
            _resident(wvt.shape),
        ],
        out_specs=pl.BlockSpec((seq, N_HEADS * V_DIM), row),
        out_shape=jax.ShapeDtypeStruct((q.shape[0], N_HEADS * V_DIM), BF16),
        scratch_shapes=[pltpu.VMEM((seq, N_HEADS * HEAD_PAD), BF16), pltpu.VMEM((N_HEADS * V_DIM, seq), BF16)],
        compiler_params=_params("parallel"),
        name="self_attention",
    )(q, ckv, kpe, wk, wvt)


def _mlp_kernel(h_ref, x_ref, gate_ref, w1_ref, w2_ref, gf_ref, *rest, tm, final, n_cast):
    src_refs, o_ref, dst_refs = rest[:n_cast], rest[n_cast], rest[n_cast + 1:]
    f = pl.program_id(1)

    @pl.when(f == 0)
    def _():
        o_ref[...] = x_ref[...]

    hid = _dot(h_ref[...], w1_ref[...])
    hid = jnp.square(jnp.maximum(hid, 0.0)).astype(BF16)
    o_ref[...] += gate_ref[0] * _dot(hid, w2_ref[...])

    for src_ref, dst_ref in zip(src_refs, dst_refs):
        dst_ref[...] = src_ref[...].astype(BF16)

    if final:
        @pl.when(f == pl.num_programs(1) - 1)
        def _():
            gf = gf_ref[...]
            for s in range(tm // SUB):
                rs = slice(s * SUB, (s + 1) * SUB)
                y = o_ref[rs, :]
                o_ref[rs, :] = y * lax.rsqrt(jnp.mean(y * y, axis=-1, keepdims=True) + EPS) * gf


def _mlp(st, h, x, mod, w1, w2, final_g, layer, final, casts=(), tm=512, tf=1024):
    row = lambda i, f: (i, 0)
    nf = D_FF // tf
    steps = (st.rows // tm) * nf
    slab = lambda i, f: (i * nf + f, 0)
    cast_in, cast_out, cast_shape = [], [], []
    for w, l in casts:
        _, r, c = w.shape
        cast_in.append(pl.BlockSpec((None, r // steps, c), lambda i, f, l=l: (l, i * nf + f, 0)))
        cast_out.append(pl.BlockSpec((r // steps, c), slab))
        cast_shape.append(jax.ShapeDtypeStruct((r, c), BF16))
    out = pl.pallas_call(
        functools.partial(_mlp_kernel, tm=tm, final=final, n_cast=len(casts)),
        grid=(st.rows // tm, nf),
        in_specs=[
            pl.BlockSpec((tm, D_MODEL), row),
            pl.BlockSpec((tm, D_MODEL), row),
            _mod_spec(st, layer, 5, tm),
            pl.BlockSpec((D_MODEL, tf), lambda i, f: (0, f)),
            pl.BlockSpec((tf, D_MODEL), lambda i, f: (f, 0)),
            _resident((1, D_MODEL)),
        ] + cast_in,
        out_specs=[pl.BlockSpec((tm, D_MODEL), row)] + cast_out,
        out_shape=[jax.ShapeDtypeStruct((st.rows, D_MODEL), F32)] + cast_shape,
        compiler_params=_params("parallel", "arbitrary"),
        name="mlp",
    )(h, x, mod, w1, w2, final_g, *[w for w, _ in casts])
    return out[0], out[1:]


def _rope_tables():
    t = np.arange(DEC_SEQ)
    nf = QK_ROPE // 4
    inv = (1.0 / (np.float32(ROPE_THETA) ** (np.arange(nf, dtype=np.float32) / np.float32(nf)))).astype(np.float32)
    ang_r = (t // GRID_W).astype(np.float32)[:, None] * inv
    ang_c = (t % GRID_W).astype(np.float32)[:, None] * inv
    cr, sr, cc, sc = np.cos(ang_r), np.sin(ang_r), np.cos(ang_c), np.sin(ang_c)
    cos = np.tile(np.concatenate([cr, cr, cc, cc], axis=1), (1, LANE // QK_ROPE))
    sin = np.tile(np.concatenate([-sr, sr, -sc, sc], axis=1), (1, LANE // QK_ROPE))
    return jnp.asarray(cos, F32), jnp.asarray(sin, F32)


def _swap_rope_partners(w):
    q = QK_ROPE // 4
    return jnp.concatenate([w[..., q:2 * q], w[..., :q], w[..., 3 * q:], w[..., 2 * q:3 * q]], axis=-1)


def _mla_weights(w_q_b, w_kv_a, w_kv_b):
    wq = w_q_b.reshape(Q_RANK, N_HEADS, QK_NOPE + QK_ROPE)
    wq_nope = wq[..., :QK_NOPE].reshape(Q_RANK, N_HEADS * QK_NOPE)
    wq_rope = wq[..., QK_NOPE:].reshape(Q_RANK, N_HEADS * QK_ROPE)
    wq_rope_swap = _swap_rope_partners(wq[..., QK_NOPE:]).reshape(Q_RANK, N_HEADS * QK_ROPE)
    kpe_w = w_kv_a[:, KV_RANK:]
    wkv = jnp.concatenate([w_kv_a[:, :KV_RANK], kpe_w, _swap_rope_partners(kpe_w)], axis=1)
    wkvb = w_kv_b.reshape(KV_RANK, N_HEADS, QK_NOPE + V_DIM)
    wk = wkvb[..., :QK_NOPE].reshape(KV_RANK, N_HEADS * QK_NOPE)
    wvt = wkvb[..., QK_NOPE:].reshape(KV_RANK, N_HEADS * V_DIM).T
    casts = (wq_nope, wq_rope, wq_rope_swap, wkv, wk, wvt)
    return tuple(w.astype(BF16) for w in casts)


def kernel(x_prompt, x_sample, cache_ckv, cache_kpe, c, c_ctx, ada_w, ada_b, norm1, norm2, conv_w_in, conv_w, conv_w_out, gmlp_w_in, gmlp_g_v, gmlp_w_s, gmlp_b_s, gmlp_w_out, mla_w_q_a, mla_g_q, mla_w_q_b, mla_w_kv_a, mla_g_kv, mla_w_kv_b, mla_w_o, mlp_w1, mlp_w2, final_norm):
    D = D_MODEL
    streams = (PROMPT, SAMPLE)
    xs = [x_prompt.reshape(PROMPT.rows, D), x_sample.reshape(SAMPLE.rows, D)]
    cond = jnp.concatenate([c_ctx[None, :], c, jnp.zeros((N_GROUPS - 1 - DEC_BATCH, D), F32)], axis=0)
    mod = _modvec(cond, ada_w, ada_b)

    n1, n2 = norm1[:, None, :], norm2[:, None, :]
    final_g = final_norm[None, :]

    def big_weights(i):
        mix_in, mix_out = ((conv_w_in, conv_w_out), (gmlp_w_in, gmlp_w_out), (mla_w_q_a, mla_w_o))[i % N_MIXERS]
        return {"w1": (mlp_w1, i), "in": (mix_in, i // N_MIXERS), "w2": (mlp_w2, i), "out": (mix_out, i // N_MIXERS)}

    cast_split = (("w1", "in"), ("w2", "out"))
    bf = {name: w[l].astype(BF16) for name, (w, l) in big_weights(0).items() if name in ("in", "out")}

    cos_t, sin_t = _rope_tables()
    new_ckv, new_kpe = [], []
    for i in range(DEPTH):
        kind, j = i % N_MIXERS, i // N_MIXERS
        hs = [None, None]
        if kind == 0:
            host = "w1" not in bf
            for n, st in enumerate(streams):
                cast = HalfCast(mlp_w1, i, bf.get("w1")) if host else None
                bg, p, *half = _conv_in(st, xs[n], mod, n1, bf["in"], i, cast=cast)
                bf.update(zip(("w1",), half))
                cast = HalfCast(mlp_w2, i, bf.get("w2")) if host else None
                xs[n], hs[n], *half = _conv_out(st, bg, p, conv_w, xs[n], mod, bf["out"], n2, i, j, cast=cast)
                bf.update(zip(("w2",), half))
            if host:
                bf["w1"], bf["w2"] = bf["w1"].reshape(mlp_w1.shape[1:]), bf["w2"].reshape(mlp_w2.shape[1:])
        elif kind == 1:
            b_full = jnp.repeat(gmlp_b_s[j].T, LANE, axis=1)
            w_s = gmlp_w_s[j].astype(BF16)
            for n, st in enumerate(streams):
                z = _gmlp_in(st, xs[n], mod, n1, bf["in"], i)
                xs[n], hs[n] = _gmlp_out(st, z, gmlp_g_v[:, None, :], w_s, b_full, xs[n], mod, bf["out"], n2, i, j)
        else:
            wqn, wqr, wqrs, wkv, wk, wvt = _mla_weights(mla_w_q_b[j], mla_w_kv_a[j], mla_w_kv_b[j])
            proj = [_mla_proj(st, xs[n], mod, n1, bf["in"], mla_g_q[:, None, :], wqn, wqr, wqrs, wkv, mla_g_kv[:, None, :],
                              cos_t, sin_t, i, j) for n, st in enumerate(streams)]
            q, ckv, kpe, kpe_rot = proj[0]
            new_ckv.append(ckv.reshape(BATCH, SEQ, KV_RANK))
            new_kpe.append(kpe[:, :QK_ROPE].reshape(BATCH, SEQ, QK_ROPE))
            o_p = _self_attention(q, ckv, kpe_rot, wk, wvt, SEQ, ahead=CTX_AHEAD)
            q, ckv, kpe, kpe_rot = proj[1]
            lk = PAST_LEN + DEC_SEQ
            cache_pe = jnp.pad(cache_kpe[:, j], ((0, 0), (0, 0), (0, LANE - QK_ROPE))).astype(BF16)
            past = (cache_ckv[:, j].reshape(DEC_BATCH * PAST_LEN, KV_RANK), cache_pe.reshape(DEC_BATCH * PAST_LEN, LANE))
            k_s, v_s = _kv_expand(ckv, kpe_rot, wk, wvt, seq=DEC_SEQ, past=past)
            o_s = _attention(q, k_s, v_s, DEC_BATCH, DEC_SEQ, lk, **ATTN_SAMPLE)
            for n, (st, o) in enumerate(zip(streams, (o_p, o_s))):
                xs[n], hs[n] = _attn_out(st, o, xs[n], mod, bf["out"], n2, i)
        nxt = big_weights(i + 1) if i + 1 < DEPTH else None
        bf_next = {}
        for n, st in enumerate(streams):
            names = cast_split[n] if nxt else ()
            xs[n], cast = _mlp(st, hs[n], xs[n], mod, bf["w1"], bf["w2"], final_g, i, final=(i == DEPTH - 1),
                               casts=[nxt[name] for name in names])
            bf_next.update(zip(names, cast))
        bf = bf_next
    y_prompt = xs[0].reshape(BATCH, SEQ, D)
    y_sample = xs[1].reshape(DEC_BATCH, DEC_SEQ, D)
    return (y_prompt, y_sample, jnp.stack(new_ckv, axis=1), jnp.stack(new_kpe, axis=1))
```

```python
import functools
from typing import NamedTuple

import jax
import jax.numpy as jnp
import numpy as np
from jax import lax
from jax.experimental import pallas as pl
from jax.experimental.pallas import tpu as pltpu

D_MODEL = 2048
BATCH = 32
SEQ = 256
DEPTH = 4
DEC_BATCH = 4
DEC_SEQ = 2048
PAST_LEN = 512
GRID_W = 64
N_MIXERS = 3
CONV_WIDTH = 3
CHUNK = 128
GMLP_GROUPS = 16
N_HEADS = 16
QK_NOPE = 128
QK_ROPE = 64
V_DIM = 128
Q_RANK = 512
KV_RANK = 512
ROPE_THETA = 10000.0
D_FF = 4 * D_MODEL
N_MOD = 6
EPS = 1e-6
LOG2_E = 1.4426950408889634

N_GROUPS = 8
HEAD_PAD = 256
LANE = 128
BF16_ROWS = 16
SUB = 256
K_PIECE = 256
VMEM_LIMIT = 56 * 1024 * 1024
ATTN_SAMPLE = dict(heads=4, tq=1024, ahead=1)
CTX_AHEAD = 16

BF16 = jnp.bfloat16
F32 = jnp.float32


class Stream(NamedTuple):
    rows: int
    seq: int
    group0: int
    group_rows: int
    rope: bool


PROMPT = Stream(BATCH * SEQ, SEQ, 0, BATCH * SEQ, False)
SAMPLE = Stream(DEC_BATCH * DEC_SEQ, DEC_SEQ, 1, DEC_SEQ, True)


def _params(*sem):
    return pltpu.CompilerParams(dimension_semantics=sem, vmem_limit_bytes=VMEM_LIMIT)


def _dot(a, b):
    return jnp.dot(a, b, preferred_element_type=F32)


def _resident(shape, layer=None):
    if layer is None:
        zeros = (0,) * len(shape)
        return pl.BlockSpec(shape, lambda *_: zeros, pipeline_mode=pl.Buffered(1))
    index = (layer,) + (0,) * len(shape)
    return pl.BlockSpec((None,) + tuple(shape), lambda *_: index, pipeline_mode=pl.Buffered(1))


def _mod_spec(st, layer, chunk, tm):
    def index(i, *_):
        return ((layer * N_MOD + chunk) * N_GROUPS + st.group0 + (i * tm) // st.group_rows, 0, 0)

    return pl.BlockSpec((1, 1, D_MODEL), index)


def _gain_shift(g_ref, sh_ref, sc_ref):
    return g_ref[...] * (1.0 + sc_ref[0]), sh_ref[0]


def _norm_mod(x, gsc, sh):
    ms = jnp.mean(x * x, axis=-1, keepdims=True)
    return (x * lax.rsqrt(ms + EPS) * gsc + sh).astype(BF16)


def _modvec_kernel(c_ref, w_ref, b_ref, o_ref):
    s = jax.nn.silu(c_ref[...]).astype(BF16)
    o_ref[0, 0] = _dot(s, w_ref[0].astype(BF16)) + b_ref[0]


def _modvec(cond, ada_w, ada_b, tn=1024):
    per_chunk = D_MODEL // tn
    out = pl.pallas_call(
        _modvec_kernel,
        grid=(DEPTH, N_MOD * per_chunk),
        in_specs=[
            pl.BlockSpec((N_GROUPS, D_MODEL), lambda l, j: (0, 0)),
            pl.BlockSpec((1, D_MODEL, tn), lambda l, j: (l, 0, j)),
            pl.BlockSpec((1, 1, tn), lambda l, j: (l, 0, j)),
        ],
        out_specs=pl.BlockSpec((1, 1, N_GROUPS, tn), lambda l, j: (l, j // per_chunk, 0, j % per_chunk)),
        out_shape=jax.ShapeDtypeStruct((DEPTH, N_MOD, N_GROUPS, D_MODEL), F32),
        compiler_params=_params("arbitrary", "arbitrary"),
        name="modvec",
    )(cond, ada_w, ada_b.reshape(DEPTH, 1, N_MOD * D_MODEL))
    return out.reshape(DEPTH * N_MOD * N_GROUPS, 1, D_MODEL)


class HalfCast(NamedTuple):
    w: jax.Array
    layer: int
    prev: jax.Array | None


def _half_cast_specs(cast, steps):
    _, r, c = cast.w.shape
    rows = r // (2 * steps)
    slab = lambda i: (i, 0)
    if cast.prev is None:
        in_specs = [pl.BlockSpec((None, rows, c), lambda i: (cast.layer, i, 0))]
        return in_specs, [cast.w], pl.BlockSpec((rows, c), slab), jax.ShapeDtypeStruct((r // 2, c), BF16)
    in_specs = [pl.BlockSpec((None, rows, c), lambda i: (cast.layer, steps + i, 0)), pl.BlockSpec((rows, c), slab)]
    out_spec = pl.BlockSpec((2, rows, c), lambda i: (0, i, 0))
    return in_specs, [cast.w, cast.prev], out_spec, jax.ShapeDtypeStruct((2, r // 2, c), BF16)


def _half_cast_step(refs, dst_ref):
    if len(refs) == 1:
        dst_ref[...] = refs[0][...].astype(BF16)
    else:
        dst_ref[0] = refs[1][...]
        dst_ref[1] = refs[0][...].astype(BF16)


def _conv_in_kernel(x_ref, g_ref, sh_ref, sc_ref, w_ref, *rest, tm, cast):
    bg_ref, p_ref = rest[-3:-1] if cast else rest
    gsc, sh = _gain_shift(g_ref, sh_ref, sc_ref)
    for s in range(tm // SUB):
        rs = slice(s * SUB, (s + 1) * SUB)
        z = _dot(_norm_mod(x_ref[rs, :], gsc, sh), w_ref[...])
        bg_ref[rs, :] = z[:, :D_MODEL].astype(BF16)
        p_ref[rs, :] = (z[:, D_MODEL:2 * D_MODEL] * z[:, 2 * D_MODEL:]).astype(BF16)
    if cast:
        _half_cast_step(rest[:-3], rest[-1])


def _conv_in(st, x, mod, norm1, w_in, layer, cast=None, tm=512):
    row = lambda i: (i, 0)
    in_specs = [
        pl.BlockSpec((tm, D_MODEL), row),
        _resident((1, D_MODEL), layer),
        _mod_spec(st, layer, 0, tm),
        _mod_spec(st, layer, 1, tm),
        _resident((D_MODEL, 3 * D_MODEL)),
    ]
    args = [x, norm1, mod, mod, w_in]
    out_specs = [pl.BlockSpec((tm, D_MODEL), row)] * 2
    out_shape = [jax.ShapeDtypeStruct((st.rows, D_MODEL), BF16)] * 2
    if cast is not None:
        c_in, c_args, c_out, c_shape = _half_cast_specs(cast, st.rows // tm)
        in_specs, args, out_specs, out_shape = in_specs + c_in, args + c_args, out_specs + [c_out], out_shape + [c_shape]
    return pl.pallas_call(
        functools.partial(_conv_in_kernel, tm=tm, cast=cast is not None),
        grid=(st.rows // tm,),
        in_specs=in_specs,
        out_specs=out_specs,
        out_shape=out_shape,
        compiler_params=_params("parallel"),
        name="conv_in",
    )(*args)


def _gmlp_in_kernel(x_ref, g_ref, sh_ref, sc_ref, w_ref, z_ref, *, tm):
    gsc, sh = _gain_shift(g_ref, sh_ref, sc_ref)
    for s in range(tm // SUB):
        rs = slice(s * SUB, (s + 1) * SUB)
        z_ref[rs, :] = _dot(_norm_mod(x_ref[rs, :], gsc, sh), w_ref[...]).astype(BF16)


def _gmlp_in(st, x, mod, norm1, w_in, layer, tm=512):
    n_out = w_in.shape[-1]
    row = lambda i: (i, 0)
    return pl.pallas_call(
        functools.partial(_gmlp_in_kernel, tm=tm),
        grid=(st.rows // tm,),
        in_specs=[
            pl.BlockSpec((tm, D_MODEL), row),
            _resident((1, D_MODEL), layer),
            _mod_spec(st, layer, 0, tm),
            _mod_spec(st, layer, 1, tm),
            _resident((D_MODEL, n_out)),
        ],
        out_specs=pl.BlockSpec((tm, n_out), row),
        out_shape=jax.ShapeDtypeStruct((st.rows, n_out), BF16),
        compiler_params=_params("parallel"),
        name="gmlp_in",
    )(x, norm1, mod, mod, w_in)


def _residual_proj(a, rs, x_ref, gate_ref, w_ref, gsc2, sh2, o_ref, h_ref):
    if isinstance(a, (list, tuple)):
        y, lo = None, 0
        for piece in a:
            part = _dot(piece, w_ref[lo:lo + piece.shape[1], :])
            y, lo = part if y is None else y + part, lo + piece.shape[1]
    else:
        y = _dot(a, w_ref[...])
    x1 = x_ref[rs, :] + gate_ref[0] * y
    o_ref[rs, :] = x1
    h_ref[rs, :] = _norm_mod(x1, gsc2, sh2)


def _proj_specs(st, layer, tm):
    row = lambda i: (i, 0)
    in_specs = [
        pl.BlockSpec((tm, D_MODEL), row),
        _mod_spec(st, layer, 2, tm),
        _resident((D_MODEL, D_MODEL)),
        _resident((1, D_MODEL), layer),
        _mod_spec(st, layer, 3, tm),
        _mod_spec(st, layer, 4, tm),
    ]
    out_specs = [pl.BlockSpec((tm, D_MODEL), row)] * 2
    out_shape = [jax.ShapeDtypeStruct((st.rows, D_MODEL), F32), jax.ShapeDtypeStruct((st.rows, D_MODEL), BF16)]
    return in_specs, out_specs, out_shape


def _conv_out_kernel(bg_ref, p_ref, lo_ref, hi_ref, cw_ref, x_ref, gate_ref, w_ref, g2_ref, sh2_ref, sc2_ref,
                     *rest, tm, seq, cast):
    if cast:
        o_ref, h_ref, dst_ref, pbuf_ref = rest[-4:]
        _half_cast_step(rest[:-4], dst_ref)
    else:
        o_ref, h_ref, pbuf_ref = rest
    halo = BF16_ROWS
    n_sub = tm // SUB
    gsc2, sh2 = _gain_shift(g2_ref, sh2_ref, sc2_ref)
    base = pl.program_id(0) * tm
    width = K_PIECE
    for s in range(n_sub):
        r0 = s * SUB
        before = lo_ref[...] if s == 0 else p_ref[r0 - halo:r0, :]
        after = hi_ref[...] if s == n_sub - 1 else p_ref[r0 + SUB:r0 + SUB + halo, :]
        pbuf_ref[s, 0:halo, :] = jnp.where((base + r0) % seq == 0, 0.0, before.astype(F32))
        pbuf_ref[s, halo:halo + SUB, :] = p_ref[r0:r0 + SUB, :].astype(F32)
        pbuf_ref[s, halo + SUB:, :] = jnp.where((base + r0 + SUB) % seq == 0, 0.0, after.astype(F32))
        parts = []
        for lo in range(0, D_MODEL, width):
            cols = slice(lo, lo + width)
            prev = pbuf_ref[s, halo - 1:halo - 1 + SUB, cols]
            cur = pbuf_ref[s, halo:halo + SUB, cols]
            nxt = pbuf_ref[s, halo + 1:halo + 1 + SUB, cols]
            conv = prev * cw_ref[0:1, cols] + cur * cw_ref[1:2, cols] + nxt * cw_ref[2:3, cols]
            parts.append((bg_ref[r0:r0 + SUB, cols].astype(F32) * conv).astype(BF16))
        _residual_proj(parts, slice(r0, r0 + SUB), x_ref, gate_ref, w_ref, gsc2, sh2, o_ref, h_ref)


def _conv_out(st, bg, p, conv_w, x, mod, w_out, norm2, layer, j, cast=None, tm=512):
    assert st.seq % SUB == 0 and tm % SUB == 0
    nb = tm // BF16_ROWS
    last_halo = st.rows // BF16_ROWS - 1
    row = lambda i: (i, 0)
    tail, out_specs, out_shape = _proj_specs(st, layer, tm)
    in_specs = [
        pl.BlockSpec((tm, D_MODEL), row),
        pl.BlockSpec((tm, D_MODEL), row),
        pl.BlockSpec((BF16_ROWS, D_MODEL), lambda i: (jnp.maximum(i * nb - 1, 0), 0)),
        pl.BlockSpec((BF16_ROWS, D_MODEL), lambda i: (jnp.minimum((i + 1) * nb, last_halo), 0)),
        _resident((CONV_WIDTH, D_MODEL), j),
    ] + tail
    args = [bg, p, p, p, conv_w, x, mod, w_out, norm2, mod, mod]
    if cast is not None:
        c_in, c_args, c_out, c_shape = _half_cast_specs(cast, st.rows // tm)
        in_specs, args, out_specs, out_shape = in_specs + c_in, args + c_args, out_specs + [c_out], out_shape + [c_shape]
    return pl.pallas_call(
        functools.partial(_conv_out_kernel, tm=tm, seq=st.seq, cast=cast is not None),
        grid=(st.rows // tm,),
        in_specs=in_specs,
        out_specs=out_specs,
        out_shape=out_shape,
        scratch_shapes=[pltpu.VMEM((tm // SUB, SUB + 2 * BF16_ROWS, D_MODEL), F32)],
        compiler_params=_params("parallel"),
        name="conv_out",
    )(*args)


def _gmlp_out_kernel(zu_ref, zv_ref, gv_ref, ws_ref, bs_ref, x_ref, gate_ref, w_ref, g2_ref, sh2_ref, sc2_ref,
                     o_ref, h_ref, a_ref, vn_ref, *, tm):
    gv = gv_ref[...]
    gsc2, sh2 = _gain_shift(g2_ref, sh2_ref, sc2_ref)
    for s in range(tm // SUB):
        chunks = range(s * SUB // CHUNK, (s + 1) * SUB // CHUNK)
        for c in chunks:
            rs = slice(c * CHUNK, (c + 1) * CHUNK)
            v = zv_ref[rs, :].astype(F32)
            ms = jnp.mean(v * v, axis=-1, keepdims=True)
            vn_ref[rs, :] = (v * lax.rsqrt(ms + EPS) * gv).astype(BF16)
        for g in range(GMLP_GROUPS):
            cols = slice(g * LANE, (g + 1) * LANE)
            rhs = jnp.concatenate([vn_ref[c * CHUNK:(c + 1) * CHUNK, cols] for c in chunks], axis=1)
            mixed = _dot(ws_ref[g], rhs)
            for n, c in enumerate(chunks):
                rs = slice(c * CHUNK, (c + 1) * CHUNK)
                vv = mixed[:, n * CHUNK:(n + 1) * CHUNK] + bs_ref[:, cols]
                a_ref[rs, cols] = (zu_ref[rs, cols].astype(F32) * vv).astype(BF16)
        rs = slice(s * SUB, (s + 1) * SUB)
        _residual_proj(a_ref[rs, :], rs, x_ref, gate_ref, w_ref, gsc2, sh2, o_ref, h_ref)


def _gmlp_out(st, z, g_v, w_s, b_full, x, mod, w_out, norm2, layer, j, tm=512):
    tail, out_specs, out_shape = _proj_specs(st, layer, tm)
    return pl.pallas_call(
        functools.partial(_gmlp_out_kernel, tm=tm),
        grid=(st.rows // tm,),
        in_specs=[
            pl.BlockSpec((tm, D_MODEL), lambda i: (i, 0)),
            pl.BlockSpec((tm, D_MODEL), lambda i: (i, 1)),
            _resident((1, D_MODEL), j),
            _resident((GMLP_GROUPS, CHUNK, CHUNK)),
            _resident((CHUNK, D_MODEL)),
        ] + tail,
        out_specs=out_specs,
        out_shape=out_shape,
        scratch_shapes=[pltpu.VMEM((tm, D_MODEL), BF16), pltpu.VMEM((tm, D_MODEL), BF16)],
        compiler_params=_params("parallel"),
        name="gmlp_out",
    )(z, z, g_v, w_s, b_full, x, mod, w_out, norm2, mod, mod)


def _attn_out_kernel(a_ref, x_ref, gate_ref, w_ref, g2_ref, sh2_ref, sc2_ref, o_ref, h_ref, *, tm):
    gsc2, sh2 = _gain_shift(g2_ref, sh2_ref, sc2_ref)
    for s in range(tm // SUB):
        rs = slice(s * SUB, (s + 1) * SUB)
        _residual_proj(a_ref[rs, :], rs, x_ref, gate_ref, w_ref, gsc2, sh2, o_ref, h_ref)


def _attn_out(st, a, x, mod, w_o, norm2, layer, tm=512):
    tail, out_specs, out_shape = _proj_specs(st, layer, tm)
    return pl.pallas_call(
        functools.partial(_attn_out_kernel, tm=tm),
        grid=(st.rows // tm,),
        in_specs=[pl.BlockSpec((tm, D_MODEL), lambda i: (i, 0))] + tail,
        out_specs=out_specs,
        out_shape=out_shape,
        compiler_params=_params("parallel"),
        name="attn_out",
    )(a, x, mod, w_o, norm2, mod, mod)


def _mla_proj_kernel(x_ref, g_ref, sh_ref, sc_ref, wqa_ref, gq_ref, wqn_ref, wqr_ref, wqrs_ref, wkv_ref, gkv_ref,
                     cos_ref, sin_ref, q_ref, ckv_ref, kpe_ref, kper_ref, *, tm, rope):
    gsc, sh = _gain_shift(g_ref, sh_ref, sc_ref)
    gq = gq_ref[...] * ((QK_NOPE + QK_ROPE) ** -0.5 * LOG2_E)
    first_half = lax.broadcasted_iota(jnp.int32, (1, LANE), 1) < QK_ROPE
    for s in range(tm // SUB):
        rs = slice(s * SUB, (s + 1) * SUB)
        h = _norm_mod(x_ref[rs, :], gsc, sh)
        qa = _dot(h, wqa_ref[...])
        qa = qa * lax.rsqrt(jnp.mean(qa * qa, axis=-1, keepdims=True) + EPS) * gq
        qa = qa.astype(BF16)
        qn = _dot(qa, wqn_ref[...])
        qr = _dot(qa, wqr_ref[...])
        kv = _dot(h, wkv_ref[...])
        c = kv[:, :KV_RANK]
        ckv_ref[rs, :] = c * lax.rsqrt(jnp.mean(c * c, axis=-1, keepdims=True) + EPS) * gkv_ref[...]
        kpe2 = kv[:, KV_RANK:]
        kpe = jnp.where(first_half, kpe2, 0.0)
        kpe_ref[rs, :] = kpe
        if rope:
            cos = cos_ref[rs, :]
            sin = sin_ref[rs, :]
            qrs = _dot(qa, wqrs_ref[...])
            rot = kpe2 * cos + pltpu.roll(kpe2, QK_ROPE, axis=1) * sin
            kper_ref[rs, :] = jnp.where(first_half, rot, 0.0).astype(BF16)
        else:
            kper_ref[rs, :] = kpe.astype(BF16)
        for pair in range(N_HEADS // 2):
            tile = slice(pair * LANE, (pair + 1) * LANE)
            rot = qr[:, tile] * cos + qrs[:, tile] * sin if rope else qr[:, tile]
            halves = (rot, pltpu.roll(rot, QK_ROPE, axis=1))
            for hd, val in zip((2 * pair, 2 * pair + 1), halves):
                lo = hd * HEAD_PAD
                q_ref[rs, lo:lo + LANE] = qn[:, hd * LANE:(hd + 1) * LANE].astype(BF16)
                q_ref[rs, lo + LANE:lo + HEAD_PAD] = jnp.where(first_half, val, 0.0).astype(BF16)


def _mla_proj(st, x, mod, norm1, wqa, gq, wqn, wqr, wqrs, wkv, gkv, cos_t, sin_t, layer, j, tm=512):
    per_seq = st.seq // tm
    rope_idx = (lambda i: (i % per_seq, 0)) if st.rope else (lambda i: (0, 0))
    row = lambda i: (i, 0)
    return pl.pallas_call(
        functools.partial(_mla_proj_kernel, tm=tm, rope=st.rope),
        grid=(st.rows // tm,),
        in_specs=[
            pl.BlockSpec((tm, D_MODEL), row),
            _resident((1, D_MODEL), layer),
            _mod_spec(st, layer, 0, tm),
            _mod_spec(st, layer, 1, tm),
            _resident(wqa.shape),
            _resident((1, Q_RANK), j),
            _resident(wqn.shape),
            _resident(wqr.shape),
            _resident(wqrs.shape),
            _resident(wkv.shape),
            _resident((1, KV_RANK), j),
            pl.BlockSpec((tm, LANE), rope_idx),
            pl.BlockSpec((tm, LANE), rope_idx),
        ],
        out_specs=[
            pl.BlockSpec((tm, N_HEADS * HEAD_PAD), row),
            pl.BlockSpec((tm, KV_RANK), row),
            pl.BlockSpec((tm, LANE), row),
            pl.BlockSpec((tm, LANE), row),
        ],
        out_shape=[
            jax.ShapeDtypeStruct((st.rows, N_HEADS * HEAD_PAD), BF16),
            jax.ShapeDtypeStruct((st.rows, KV_RANK), F32),
            jax.ShapeDtypeStruct((st.rows, LANE), F32),
            jax.ShapeDtypeStruct((st.rows, LANE), BF16),
        ],
        compiler_params=_params("parallel"),
        name="mla_proj",
    )(x, norm1, mod, mod, wqa, gq, wqn, wqr, wqrs, wkv, gkv, cos_t, sin_t)


def _dot_nt(a, b):
    return lax.dot_general(a, b, (((1,), (1,)), ((), ())), preferred_element_type=F32)


def _expand_keys(c, kpe, wk_ref, wvt_ref, k_ref, vt_ref):
    c = c.astype(BF16)
    kn = _dot(c, wk_ref[...])
    vt_ref[...] = _dot_nt(wvt_ref[...], c).astype(BF16)
    for hd in range(N_HEADS):
        lo = hd * HEAD_PAD
        k_ref[:, lo:lo + LANE] = kn[:, hd * LANE:(hd + 1) * LANE].astype(BF16)
        k_ref[:, lo + LANE:lo + HEAD_PAD] = kpe


def _kv_expand_kernel(*refs, past_blocks, seq_blocks):
    c_ref, kpe_ref, wk_ref, wvt_ref, k_ref, vt_ref = refs[-6:]
    c, kpe = c_ref[...], kpe_ref[...]
    if past_blocks:
        cached = pl.program_id(0) % (past_blocks + seq_blocks) < past_blocks
        c = jnp.where(cached, refs[0][...], c)
        kpe = jnp.where(cached, refs[1][...], kpe)
    _expand_keys(c, kpe, wk_ref, wvt_ref, k_ref, vt_ref)


def _kv_expand(ckv, kpe, wk, wvt, seq=None, past=None, tr=512):
    seq = ckv.shape[0] if seq is None else seq
    n_req = ckv.shape[0] // seq
    seq_blocks = seq // tr
    past_blocks = 0 if past is None else past[0].shape[0] // n_req // tr
    per_req = past_blocks + seq_blocks
    n_rows = n_req * per_req * tr
    row = lambda i: (i, 0)
    new_row = lambda i: ((i // per_req) * seq_blocks + jnp.maximum(i % per_req - past_blocks, 0), 0)
    past_row = lambda i: ((i // per_req) * past_blocks + jnp.minimum(i % per_req, past_blocks - 1), 0)
    past_specs = [] if past is None else [pl.BlockSpec((tr, KV_RANK), past_row), pl.BlockSpec((tr, LANE), past_row)]
    return pl.pallas_call(
        functools.partial(_kv_expand_kernel, past_blocks=past_blocks, seq_blocks=seq_blocks),
        grid=(n_rows // tr,),
        in_specs=past_specs + [
            pl.BlockSpec((tr, KV_RANK), new_row),
            pl.BlockSpec((tr, LANE), new_row),
            _resident(wk.shape),
            _resident(wvt.shape),
        ],
        out_specs=[
            pl.BlockSpec((tr, N_HEADS * HEAD_PAD), row),
            pl.BlockSpec((N_HEADS * V_DIM, tr), lambda i: (0, i)),
        ],
        out_shape=[
            jax.ShapeDtypeStruct((n_rows, N_HEADS * HEAD_PAD), BF16),
            jax.ShapeDtypeStruct((N_HEADS * V_DIM, n_rows), BF16),
        ],
        compiler_params=_params("parallel"),
        name="kv_expand",
    )(*(past or ()), ckv, kpe, wk, wvt)


def _attn_kernel(q_ref, k_ref, vt_ref, o_ref, *, heads, ahead):
    ones = jnp.ones((BF16_ROWS, k_ref.shape[0]), BF16)

    def scores(hd):
        cols = slice(hd * HEAD_PAD, (hd + 1) * HEAD_PAD)
        return _dot_nt(k_ref[:, cols], q_ref[:, cols])

    pending = [scores(hd) for hd in range(min(ahead, heads))]
    for hd in range(heads):
        st = pending.pop(0)
        if hd + ahead < heads:
            pending.append(scores(hd + ahead))
        p = jnp.exp2(st - jnp.max(st, axis=0, keepdims=True)).astype(BF16)
        lhs = jnp.concatenate([vt_ref[hd * V_DIM:(hd + 1) * V_DIM, :], ones], axis=0)
        ot = _dot(lhs, p)
        o_ref[:, hd * V_DIM:(hd + 1) * V_DIM] = (ot[:V_DIM] / ot[V_DIM:V_DIM + 1]).T.astype(BF16)


def _attention(q, k, vt, n_batch, lq, lk, heads, tq, ahead):
    nq = lq // tq
    return pl.pallas_call(
        functools.partial(_attn_kernel, heads=heads, ahead=ahead),
        grid=(n_batch, N_HEADS // heads, nq),
        in_specs=[
            pl.BlockSpec((tq, heads * HEAD_PAD), lambda b, g, i: (b * nq + i, g)),
            pl.BlockSpec((lk, heads * HEAD_PAD), lambda b, g, i: (b, g)),
            pl.BlockSpec((heads * V_DIM, lk), lambda b, g, i: (g, b)),
        ],
        out_specs=pl.BlockSpec((tq, heads * V_DIM), lambda b, g, i: (b * nq + i, g)),
        out_shape=jax.ShapeDtypeStruct((n_batch * lq, N_HEADS * V_DIM), BF16),
        compiler_params=_params("parallel", "parallel", "arbitrary"),
        name="attention",
    )(q, k, vt)


def _self_attn_kernel(q_ref, c_ref, kpe_ref, wk_ref, wvt_ref, o_ref, k_ref, vt_ref, *, ahead):
    _expand_keys(c_ref[...], kpe_ref[...], wk_ref, wvt_ref, k_ref, vt_ref)
    _attn_kernel(q_ref, k_ref, vt_ref, o_ref, heads=N_HEADS, ahead=ahead)


def _self_attention(q, ckv, kpe, wk, wvt, seq, ahead):
    row = lambda b: (b, 0)
    return pl.pallas_call(
        functools.partial(_self_attn_kernel, ahead=ahead),
        grid=(q.shape[0] // seq,),
        in_specs=[
            pl.BlockSpec((seq, N_HEADS * HEAD_PAD), row),
            pl.BlockSpec((seq, KV_RANK), row),
            pl.BlockSpec((seq, LANE), row),
            _resident(wk.shape),
            _resident(wvt.shape),
        ],
        out_specs=pl.BlockSpec((seq, N_HEADS * V_DIM), row),
        out_shape=jax.ShapeDtypeStruct((q.shape[0], N_HEADS * V_DIM), BF16),
        scratch_shapes=[pltpu.VMEM((seq, N_HEADS * HEAD_PAD), BF16), pltpu.VMEM((N_HEADS * V_DIM, seq), BF16)],
        compiler_params=_params("parallel"),
        name="self_attention",
    )(q, ckv, kpe, wk, wvt)


def _mlp_kernel(h_ref, x_ref, gate_ref, w1_ref, w2_ref, gf_ref, *rest, tm, final, n_cast):
    src_refs, o_ref, dst_refs = rest[:n_cast], rest[n_cast], rest[n_cast + 1:]
    f = pl.program_id(1)

    @pl.when(f == 0)
    def _():
        o_ref[...] = x_ref[...]

    hid = _dot(h_ref[...], w1_ref[...])
    hid = jnp.square(jnp.maximum(hid, 0.0)).astype(BF16)
    o_ref[...] += gate_ref[0] * _dot(hid, w2_ref[...])

    for src_ref, dst_ref in zip(src_refs, dst_refs):
        dst_ref[...] = src_ref[...].astype(BF16)

    if final:
        @pl.when(f == pl.num_programs(1) - 1)
        def _():
            gf = gf_ref[...]
            for s in range(tm // SUB):
                rs = slice(s * SUB, (s + 1) * SUB)
                y = o_ref[rs, :]
                o_ref[rs, :] = y * lax.rsqrt(jnp.mean(y * y, axis=-1, keepdims=True) + EPS) * gf


def _mlp(st, h, x, mod, w1, w2, final_g, layer, final, casts=(), tm=512, tf=1024):
    row = lambda i, f: (i, 0)
    nf = D_FF // tf
    steps = (st.rows // tm) * nf
    slab = lambda i, f: (i * nf + f, 0)
    cast_in, cast_out, cast_shape = [], [], []
    for w, l in casts:
        _, r, c = w.shape
        cast_in.append(pl.BlockSpec((None, r // steps, c), lambda i, f, l=l: (l, i * nf + f, 0)))
        cast_out.append(pl.BlockSpec((r // steps, c), slab))
        cast_shape.append(jax.ShapeDtypeStruct((r, c), BF16))
    out = pl.pallas_call(
        functools.partial(_mlp_kernel, tm=tm, final=final, n_cast=len(casts)),
        grid=(st.rows // tm, nf),
        in_specs=[
            pl.BlockSpec((tm, D_MODEL), row),
            pl.BlockSpec((tm, D_MODEL), row),
            _mod_spec(st, layer, 5, tm),
            pl.BlockSpec((D_MODEL, tf), lambda i, f: (0, f)),
            pl.BlockSpec((tf, D_MODEL), lambda i, f: (f, 0)),
            _resident((1, D_MODEL)),
        ] + cast_in,
        out_specs=[pl.BlockSpec((tm, D_MODEL), row)] + cast_out,
        out_shape=[jax.ShapeDtypeStruct((st.rows, D_MODEL), F32)] + cast_shape,
        compiler_params=_params("parallel", "arbitrary"),
        name="mlp",
    )(h, x, mod, w1, w2, final_g, *[w for w, _ in casts])
    return out[0], out[1:]


def _rope_tables():
    t = np.arange(DEC_SEQ)
    nf = QK_ROPE // 4
    inv = (1.0 / (np.float32(ROPE_THETA) ** (np.arange(nf, dtype=np.float32) / np.float32(nf)))).astype(np.float32)
    ang_r = (t // GRID_W).astype(np.float32)[:, None] * inv
    ang_c = (t % GRID_W).astype(np.float32)[:, None] * inv
    cr, sr, cc, sc = np.cos(ang_r), np.sin(ang_r), np.cos(ang_c), np.sin(ang_c)
    cos = np.tile(np.concatenate([cr, cr, cc, cc], axis=1), (1, LANE // QK_ROPE))
    sin = np.tile(np.concatenate([-sr, sr, -sc, sc], axis=1), (1, LANE // QK_ROPE))
    return jnp.asarray(cos, F32), jnp.asarray(sin, F32)


def _swap_rope_partners(w):
    q = QK_ROPE // 4
    return jnp.concatenate([w[..., q:2 * q], w[..., :q], w[..., 3 * q:], w[..., 2 * q:3 * q]], axis=-1)


def _mla_weights(w_q_b, w_kv_a, w_kv_b):
    wq = w_q_b.reshape(Q_RANK, N_HEADS, QK_NOPE + QK_ROPE)
    wq_nope = wq[..., :QK_NOPE].reshape(Q_RANK, N_HEADS * QK_NOPE)
    wq_rope = wq[..., QK_NOPE:].reshape(Q_RANK, N_HEADS * QK_ROPE)
    wq_rope_swap = _swap_rope_partners(wq[..., QK_NOPE:]).reshape(Q_RANK, N_HEADS * QK_ROPE)
    kpe_w = w_kv_a[:, KV_RANK:]
    wkv = jnp.concatenate([w_kv_a[:, :KV_RANK], kpe_w, _swap_rope_partners(kpe_w)], axis=1)
    wkvb = w_kv_b.reshape(KV_RANK, N_HEADS, QK_NOPE + V_DIM)
    wk = wkvb[..., :QK_NOPE].reshape(KV_RANK, N_HEADS * QK_NOPE)
    wvt = wkvb[..., QK_NOPE:].reshape(KV_RANK, N_HEADS * V_DIM).T
    casts = (wq_nope, wq_rope, wq_rope_swap, wkv, wk, wvt)
    return tuple(w.astype(BF16) for w in casts)


def kernel(x_prompt, x_sample, cache_ckv, cache_kpe, c, c_ctx, ada_w, ada_b, norm1, norm2, conv_w_in, conv_w, conv_w_out, gmlp_w_in, gmlp_g_v, gmlp_w_s, gmlp_b_s, gmlp_w_out, mla_w_q_a, mla_g_q, mla_w_q_b, mla_w_kv_a, mla_g_kv, mla_w_kv_b, mla_w_o, mlp_w1, mlp_w2, final_norm):
    D = D_MODEL
    streams = (PROMPT, SAMPLE)
    xs = [x_prompt.reshape(PROMPT.rows, D), x_sample.reshape(SAMPLE.rows, D)]
    cond = jnp.concatenate([c_ctx[None, :], c, jnp.zeros((N_GROUPS - 1 - DEC_BATCH, D), F32)], axis=0)
    mod = _modvec(cond, ada_w, ada_b)

    n1, n2 = norm1[:, None, :], norm2[:, None, :]
    final_g = final_norm[None, :]

    def big_weights(i):
        mix_in, mix_out = ((conv_w_in, conv_w_out), (gmlp_w_in, gmlp_w_out), (mla_w_q_a, mla_w_o))[i % N_MIXERS]
        return {"w1": (mlp_w1, i), "in": (mix_in, i // N_MIXERS), "w2": (mlp_w2, i), "out": (mix_out, i // N_MIXERS)}

    cast_split = (("w1", "in"), ("w2", "out"))
    bf = {name: w[l].astype(BF16) for name, (w, l) in big_weights(0).items() if name in ("in", "out")}

    cos_t, sin_t = _rope_tables()
    new_ckv, new_kpe = [], []
    for i in range(DEPTH):
        kind, j = i % N_MIXERS, i // N_MIXERS
        hs = [None, None]
        if kind == 0:
            host = "w1" not in bf
            for n, st in enumerate(streams):
                cast = HalfCast(mlp_w1, i, bf.get("w1")) if host else None
                bg, p, *half = _conv_in(st, xs[n], mod, n1, bf["in"], i, cast=cast)
                bf.update(zip(("w1",), half))
                cast = HalfCast(mlp_w2, i, bf.get("w2")) if host else None
                xs[n], hs[n], *half = _conv_out(st, bg, p, conv_w, xs[n], mod, bf["out"], n2, i, j, cast=cast)
                bf.update(zip(("w2",), half))
            if host:
                bf["w1"], bf["w2"] = bf["w1"].reshape(mlp_w1.shape[1:]), bf["w2"].reshape(mlp_w2.shape[1:])
        elif kind == 1:
            b_full = jnp.repeat(gmlp_b_s[j].T, LANE, axis=1)
            w_s = gmlp_w_s[j].astype(BF16)
            for n, st in enumerate(streams):
                z = _gmlp_in(st, xs[n], mod, n1, bf["in"], i)
                xs[n], hs[n] = _gmlp_out(st, z, gmlp_g_v[:, None, :], w_s, b_full, xs[n], mod, bf["out"], n2, i, j)
        else:
            wqn, wqr, wqrs, wkv, wk, wvt = _mla_weights(mla_w_q_b[j], mla_w_kv_a[j], mla_w_kv_b[j])
            proj = [_mla_proj(st, xs[n], mod, n1, bf["in"], mla_g_q[:, None, :], wqn, wqr, wqrs, wkv, mla_g_kv[:, None, :],
                              cos_t, sin_t, i, j) for n, st in enumerate(streams)]
            q, ckv, kpe, kpe_rot = proj[0]
            new_ckv.append(ckv.reshape(BATCH, SEQ, KV_RANK))
            new_kpe.append(kpe[:, :QK_ROPE].reshape(BATCH, SEQ, QK_ROPE))
            o_p = _self_attention(q, ckv, kpe_rot, wk, wvt, SEQ, ahead=CTX_AHEAD)
            q, ckv, kpe, kpe_rot = proj[1]
            lk = PAST_LEN + DEC_SEQ
            cache_pe = jnp.pad(cache_kpe[:, j], ((0, 0), (0, 0), (0, LANE - QK_ROPE))).astype(BF16)
            past = (cache_ckv[:, j].reshape(DEC_BATCH * PAST_LEN, KV_RANK), cache_pe.reshape(DEC_BATCH * PAST_LEN, LANE))
            k_s, v_s = _kv_expand(ckv, kpe_rot, wk, wvt, seq=DEC_SEQ, past=past)
            o_s = _attention(q, k_s, v_s, DEC_BATCH, DEC_SEQ, lk, **ATTN_SAMPLE)
            for n, (st, o) in enumerate(zip(streams, (o_p, o_s))):
                xs[n], hs[n] = _attn_out(st, o, xs[n], mod, bf["out"], n2, i)
        nxt = big_weights(i + 1) if i + 1 < DEPTH else None
        bf_next = {}
        for n, st in enumerate(streams):
            names = cast_split[n] if nxt else ()
            xs[n], cast = _mlp(st, hs[n], xs[n], mod, bf["w1"], bf["w2"], final_g, i, final=(i == DEPTH - 1),
                               casts=[nxt[name] for name in names])
            bf_next.update(zip(names, cast))
        bf = bf_next
    y_prompt = xs[0].reshape(BATCH, SEQ, D)
    y_sample = xs[1].reshape(DEC_BATCH, DEC_SEQ, D)
    return (y_prompt, y_sample, jnp.stack(new_ckv, axis=1), jnp.stack(new_kpe, axis=1))
```
